```python
import functools
import jax, jax.numpy as jnp
from jax import lax
import numpy as np

D_MODEL = 1024
BATCH = 32
SEQ = 2048
DEPTH = 2
DEC_BATCH = 128
DEC_SEQ = 1
PAST_LEN = 16384
PAGE_SIZE = 128

N_ATTN_LAYERS = (DEPTH + 1) // 2
N_REC_LAYERS = DEPTH // 2
MLA_HEADS = 8
MLA_NOPE = 64
MLA_ROPE = 32
MLA_V = 64
MLA_Q_LORA = D_MODEL // 4
MLA_KV_LORA = D_MODEL // 8
MLA_ROW = MLA_KV_LORA + MLA_ROPE
MLA_SCALE = (MLA_NOPE + MLA_ROPE) ** -0.5
DSA_HEADS = 8
DSA_KV_HEADS = 2
DSA_HEAD_DIM = 64
DSA_GROUP = DSA_HEADS // DSA_KV_HEADS
DSA_SCALE = DSA_HEAD_DIM ** -0.5
IDX_HEADS = 4
IDX_DIM = 64
IDX_SCALE = IDX_DIM ** -0.5
IDX_W_SCALE = IDX_HEADS ** -0.5
TOPK_MAX = 256
EVEN_SIZES = (MLA_Q_LORA, MLA_ROW, DSA_HEADS * DSA_HEAD_DIM, DSA_KV_HEADS * DSA_HEAD_DIM,
              DSA_KV_HEADS * DSA_HEAD_DIM, IDX_HEADS * IDX_DIM, IDX_DIM, IDX_HEADS)
EVEN_IN = sum(EVEN_SIZES)
EVEN_SPLITS = tuple(int(s) for s in np.cumsum(EVEN_SIZES)[:-1])
MIX_EVEN = MLA_HEADS * MLA_V + DSA_HEADS * DSA_HEAD_DIM
HG_HEADS = 8
HG_EXPAND = 128
HG_F = HG_HEADS * HG_EXPAND
HG_HEAD_V = D_MODEL // HG_HEADS
HG_CHUNK = 32
HG_Q_SCALE = HG_EXPAND ** -0.5
ODD_IN = 2 * HG_F + 2 * D_MODEL
D_FF = ((8 * D_MODEL + 3 * 256 - 1) // (3 * 256)) * 256
ROPE_THETA = 10000.0
EPS = 1e-6
Q_BLOCK = 128
DSA_Q_BLOCK = 64

kernel_name = 'hybrid_mla_dsa_hgrn2_adaln_step'


def rmsnorm(x, g):
    xf = x.astype(jnp.float32)
    y = xf * lax.rsqrt(jnp.mean(xf * xf, axis=-1, keepdims=True) + EPS)
    return (y * g.astype(jnp.float32)).astype(x.dtype)


def rope(x, pos):
    half = x.shape[-1] // 2
    inv_freq = ROPE_THETA ** (-jnp.arange(half, dtype=jnp.float32) / half)
    ang = pos.astype(jnp.float32)[:, None] * inv_freq[None, :]
    cos, sin = jnp.cos(ang)[:, None, :], jnp.sin(ang)[:, None, :]
    xf = x.astype(jnp.float32)
    x1, x2 = xf[..., :half], xf[..., half:]
    return jnp.concatenate([x1 * cos - x2 * sin, x2 * cos + x1 * sin], axis=-1).astype(x.dtype)


def adaln(c, w, b):
    m = jax.nn.silu(c) @ w + b
    return jnp.split(m[:, None, :], 6, axis=-1)


def swiglu(h, wg, wu, wd):
    return (jax.nn.silu(h @ wg) * (h @ wu)) @ wd


def _n_blocks(t, blk):
    return t // blk if t % blk == 0 else 1


def _to_blocks(a, nb):
    b, t = a.shape[:2]
    return a.reshape((b, nb, t // nb) + a.shape[2:]).swapaxes(0, 1)


def _from_blocks(a):
    a = a.swapaxes(0, 1)
    return a.reshape((a.shape[0], a.shape[1] * a.shape[2]) + a.shape[3:])


def even_project(h, pos, w_in, q_norm_g, kv_norm_g, w_q_nope, w_q_rope, w_uk):
    B, T, _ = h.shape
    zq, zkv, dq, dk, dv, iq, ik, iw = jnp.split(h @ w_in, EVEN_SPLITS, axis=-1)
    cq = rmsnorm(zq, q_norm_g)
    q_nope = jnp.einsum('btr,rhd->bthd', cq, w_q_nope)
    q_rope = rope(jnp.einsum('btr,rhe->bthe', cq, w_q_rope), pos)
    q_lat = jnp.einsum('bthd,rhd->bthr', q_nope, w_uk)
    ckv = rmsnorm(zkv[..., :MLA_KV_LORA], kv_norm_g)
    kr = rope(zkv[..., MLA_KV_LORA:][:, :, None, :], pos)[:, :, 0]
    mla_row = jnp.concatenate([ckv, kr], axis=-1)
    dq = rope(dq.reshape(B, T, DSA_HEADS, DSA_HEAD_DIM), pos)
    dk = rope(dk.reshape(B, T, DSA_KV_HEADS, DSA_HEAD_DIM), pos)
    dv = dv.reshape(B, T, DSA_KV_HEADS, DSA_HEAD_DIM)
    dsa_row = jnp.stack([dk, dv], axis=2)
    iq = rope(iq.reshape(B, T, IDX_HEADS, IDX_DIM), pos)
    ik = rope(ik[:, :, None, :], pos)[:, :, 0]
    iw = iw * IDX_W_SCALE
    return q_lat, q_rope, mla_row, dq, dsa_row, iq, ik, iw


def mla_attend(q_lat, q_rope, q_pos, kv_rows, k_pos):
    ckv, kr = kv_rows[..., :MLA_KV_LORA], kv_rows[..., MLA_KV_LORA:]
    nb = _n_blocks(q_lat.shape[1], Q_BLOCK)

    def block(args):
        ql, qr, qp = args
        s = (jnp.einsum('bqhr,bkr->bhqk', ql, ckv).astype(jnp.float32)
             + jnp.einsum('bqhe,bke->bhqk', qr, kr).astype(jnp.float32)) * MLA_SCALE
        s = jnp.where(k_pos[None, None, None, :] <= qp[None, None, :, None], s, -jnp.inf)
        p = jax.nn.softmax(s, axis=-1).astype(ckv.dtype)
        return jnp.einsum('bhqk,bkr->bqhr', p, ckv)

    out = lax.map(block, (_to_blocks(q_lat, nb), _to_blocks(q_rope, nb), q_pos.reshape(nb, -1)))
    return _from_blocks(out)


def dsa_attend(q, q_pos, iq, iw, idx_keys, fetch_kv, n_sel):
    B, T, H, HD = q.shape
    k_pos = jnp.arange(idx_keys.shape[1])
    nb = _n_blocks(T, DSA_Q_BLOCK)

    def block(args):
        qb, iqb, iwb, qp = args
        nq = qb.shape[1]
        dots = jnp.einsum('bqnd,bld->bqnl', iqb, idx_keys).astype(jnp.float32) * IDX_SCALE
        score = jnp.einsum('bqn,bqnl->bql', iwb.astype(jnp.float32), jax.nn.relu(dots))
        score = jnp.where(k_pos[None, None, :] <= qp[None, :, None], score, -jnp.inf)
        _, sel = lax.top_k(score, n_sel)
        valid = sel <= qp[None, :, None]
        kv = fetch_kv(sel)
        qg = qb.reshape(B, nq, DSA_KV_HEADS, DSA_GROUP, HD)
        s = jnp.einsum('bqgnd,bqkgd->bqgnk', qg, kv[:, :, :, 0]).astype(jnp.float32) * DSA_SCALE
        s = jnp.where(valid[:, :, None, None, :], s, -jnp.inf)
        p = jax.nn.softmax(s, axis=-1).astype(kv.dtype)
        o = jnp.einsum('bqgnk,bqkgd->bqgnd', p, kv[:, :, :, 1])
        return o.reshape(B, nq, H, HD)

    out = lax.map(block, (_to_blocks(q, nb), _to_blocks(iq, nb), _to_blocks(iw, nb), q_pos.reshape(nb, -1)))
    return _from_blocks(out)


def local_fetch(rows, sel):
    return jax.vmap(lambda r, i: r[i])(rows, sel)


def paged_fetch(cache, layer, page_table, new_rows, sel):
    b = sel.shape[0]
    past_len = page_table.shape[1] * PAGE_SIZE
    in_past = sel < past_len
    ps = jnp.minimum(sel, past_len - 1)
    phys = jnp.take_along_axis(page_table, (ps // PAGE_SIZE).reshape(b, -1), axis=1).reshape(sel.shape)
    rows_past = cache[layer, phys, ps % PAGE_SIZE]
    ns = jnp.clip(sel - past_len, 0, new_rows.shape[1] - 1)
    rows_new = jax.vmap(lambda r, i: r[i])(new_rows, ns)
    mask = in_past.reshape(in_past.shape + (1,) * (rows_past.ndim - in_past.ndim))
    return jnp.where(mask, rows_past, rows_new)


def hgrn_chunked(q, k, v, log_f, s0):
    B, T, H, K = q.shape
    V = v.shape[-1]
    c = HG_CHUNK if T % HG_CHUNK == 0 else T
    n = T // c
    to_chunks = lambda a: a.reshape(B, n, c, H, a.shape[-1]).transpose(1, 0, 3, 2, 4)
    tri = jnp.tril(jnp.ones((c, c), dtype=bool))

    def step(S, inp):
        qx, kx, vx, gx = inp
        G = jnp.cumsum(gx, axis=2)
        diff = G[:, :, :, None, :] - G[:, :, None, :, :]
        decay = jnp.exp(jnp.where(tri[None, None, :, :, None], diff, -jnp.inf))
        A = jnp.einsum('bhtk,bhsk,bhtsk->bhts', qx, kx, decay)
        o = jnp.einsum('bhts,bhsv->bhtv', A, vx) + jnp.einsum('bhtk,bhkv->bhtv', qx * jnp.exp(G), S)
        gl = G[:, :, -1]
        S_new = jnp.exp(gl)[..., None] * S + jnp.einsum('bhsk,bhsv->bhkv', kx * jnp.exp(gl[:, :, None, :] - G), vx)
        return S_new, o

    S, o = lax.scan(step, s0, tuple(to_chunks(a) for a in (q, k, v, log_f)))
    return o.transpose(1, 0, 3, 2, 4).reshape(B, T, H, V), S


def hgrn_recurrent(q, k, v, log_f, s0):
    def step(S, inp):
        qt, kt, vt, gt = inp
        S = jnp.exp(gt)[..., None] * S + kt[..., None] * vt[..., None, :]
        return S, jnp.einsum('bhk,bhkv->bhv', qt, S)

    S, o = lax.scan(step, s0, tuple(a.swapaxes(0, 1) for a in (q, k, v, log_f)))
    return o.swapaxes(0, 1), S


def setup_inputs(seed: int = 0) -> dict:
    key = jax.random.key(seed)
    ks = iter(jax.random.split(key, 40))
    nrm = lambda shape, scale: jax.random.normal(next(ks), shape, jnp.float32) * scale
    gain = lambda shape: 1.0 + nrm(shape, 0.02)
    n_pages = PAST_LEN // PAGE_SIZE
    n_pool = (DEC_BATCH * n_pages * 5) // 4
    page_table = jax.random.permutation(next(ks), n_pool)[: DEC_BATCH * n_pages]
    page_table = page_table.reshape(DEC_BATCH, n_pages).astype(jnp.int32)
    return {
        'x_prompt': nrm((BATCH, SEQ, D_MODEL), 1.0),
        'x_sample': nrm((DEC_BATCH, DEC_SEQ, D_MODEL), 1.0),
        'cache_mla': nrm((N_ATTN_LAYERS, n_pool, PAGE_SIZE, MLA_ROW), 1.0),
        'cache_dsa_kv': nrm((N_ATTN_LAYERS, n_pool, PAGE_SIZE, 2, DSA_KV_HEADS, DSA_HEAD_DIM), 1.0),
        'cache_idx': nrm((N_ATTN_LAYERS, n_pool, PAGE_SIZE, IDX_DIM), 1.0),
        'state_hgrn': nrm((N_REC_LAYERS, DEC_BATCH, HG_HEADS, HG_EXPAND, HG_HEAD_V), 0.5),
        'page_table': page_table,
        'c_prompt': nrm((BATCH, D_MODEL), 1.0),
        'c_sample': nrm((DEC_BATCH, D_MODEL), 1.0),
        'w_ada': nrm((DEPTH, D_MODEL, 6 * D_MODEL), 0.5 * D_MODEL ** -0.5),
        'b_ada': nrm((DEPTH, 6 * D_MODEL), 0.02),
        'norm_mix_g': gain((DEPTH, D_MODEL)),
        'norm_ffn_g': gain((DEPTH, D_MODEL)),
        'w_in_even': nrm((N_ATTN_LAYERS, D_MODEL, EVEN_IN), D_MODEL ** -0.5),
        'mla_q_norm_g': gain((N_ATTN_LAYERS, MLA_Q_LORA)),
        'mla_kv_norm_g': gain((N_ATTN_LAYERS, MLA_KV_LORA)),
        'mla_w_q_nope': nrm((N_ATTN_LAYERS, MLA_Q_LORA, MLA_HEADS, MLA_NOPE), MLA_Q_LORA ** -0.5),
        'mla_w_q_rope': nrm((N_ATTN_LAYERS, MLA_Q_LORA, MLA_HEADS, MLA_ROPE), MLA_Q_LORA ** -0.5),
        'mla_w_uk': nrm((N_ATTN_LAYERS, MLA_KV_LORA, MLA_HEADS, MLA_NOPE), MLA_KV_LORA ** -0.5),
        'mla_w_uv': nrm((N_ATTN_LAYERS, MLA_KV_LORA, MLA_HEADS, MLA_V), MLA_KV_LORA ** -0.5),
        'w_out_even': nrm((N_ATTN_LAYERS, MIX_EVEN, D_MODEL), MIX_EVEN ** -0.5),
        'w_in_odd': nrm((N_REC_LAYERS, D_MODEL, ODD_IN), D_MODEL ** -0.5),
        'hgrn_lower_bounds': nrm((DEPTH, HG_F), 0.5),
        'hgrn_out_norm_g': gain((N_REC_LAYERS, HG_HEAD_V)),
        'w_out_odd': nrm((N_REC_LAYERS, D_MODEL, D_MODEL), D_MODEL ** -0.5),
        'w_ffn_gate': nrm((DEPTH, D_MODEL, D_FF), D_MODEL ** -0.5),
        'w_ffn_up': nrm((DEPTH, D_MODEL, D_FF), D_MODEL ** -0.5),
        'w_ffn_down': nrm((DEPTH, D_FF, D_MODEL), D_FF ** -0.5),
        'final_norm_g': gain((D_MODEL,)),
    }


def reference(x_prompt, x_sample, cache_mla, cache_dsa_kv, cache_idx, state_hgrn, page_table, c_prompt, c_sample,
              w_ada, b_ada, norm_mix_g, norm_ffn_g, w_in_even, mla_q_norm_g, mla_kv_norm_g, mla_w_q_nope,
              mla_w_q_rope, mla_w_uk, mla_w_uv, w_out_even, w_in_odd, hgrn_lower_bounds, hgrn_out_norm_g,
              w_out_odd, w_ffn_gate, w_ffn_up, w_ffn_down, final_norm_g):
    lb_all = jax.nn.softmax(hgrn_lower_bounds.astype(jnp.float32), axis=0)
    lb_all = jnp.cumsum(lb_all, axis=0) - lb_all[0]
    n_dec, n_pages = page_table.shape
    past_len = n_pages * PAGE_SIZE

    def run(x, c, sample):
        B, T, _ = x.shape
        pos = (past_len if sample else 0) + jnp.arange(T)
        rows_mla, rows_dsa, rows_idx, states = [], [], [], []
        for l in range(DEPTH):
            sh1, sc1, g1, sh2, sc2, g2 = adaln(c, w_ada[l], b_ada[l])
            h = rmsnorm(x, norm_mix_g[l]) * (1 + sc1) + sh1
            if l % 2 == 0:
                la = l // 2
                q_lat, q_rope, mla_row, dq, dsa_row, iq, ik, iw = even_project(
                    h, pos, w_in_even[la], mla_q_norm_g[la], mla_kv_norm_g[la],
                    mla_w_q_nope[la], mla_w_q_rope[la], mla_w_uk[la])
                if sample:
                    mla_keys = jnp.concatenate([cache_mla[la, page_table].reshape(n_dec, past_len, MLA_ROW), mla_row], axis=1)
                    idx_keys = jnp.concatenate([cache_idx[la, page_table].reshape(n_dec, past_len, IDX_DIM), ik], axis=1)
                    fetch = functools.partial(paged_fetch, cache_dsa_kv, la, page_table, dsa_row)
                else:
                    mla_keys, idx_keys = mla_row, ik
                    fetch = functools.partial(local_fetch, dsa_row)
                n_keys = mla_keys.shape[1]
                lat = mla_attend(q_lat, q_rope, pos, mla_keys, jnp.arange(n_keys))
                mla_o = jnp.einsum('bthr,rhv->bthv', lat, mla_w_uv[la]).reshape(B, T, MLA_HEADS * MLA_V)
                dsa_o = dsa_attend(dq, pos, iq, iw, idx_keys, fetch, min(TOPK_MAX, n_keys // 4))
                mix = jnp.concatenate([mla_o, dsa_o.reshape(B, T, DSA_HEADS * DSA_HEAD_DIM)], axis=-1) @ w_out_even[la]
                rows_mla.append(mla_row)
                rows_dsa.append(dsa_row)
                rows_idx.append(ik)
            else:
                lr = l // 2
                zq, zf, zi, zg = jnp.split((h @ w_in_odd[lr]).astype(jnp.float32),
                                           [HG_F, 2 * HG_F, 2 * HG_F + D_MODEL], axis=-1)
                q = jax.nn.silu(zq).reshape(B, T, HG_HEADS, HG_EXPAND) * HG_Q_SCALE
                zf = zf.reshape(B, T, HG_HEADS, HG_EXPAND)
                lb = lb_all[l].reshape(HG_HEADS, HG_EXPAND)
                log_f = jnp.log(lb + (1 - lb) * jax.nn.sigmoid(zf))
                k = (1 - lb) * jax.nn.sigmoid(-zf)
                v = zi.reshape(B, T, HG_HEADS, HG_HEAD_V)
                if sample:
                    o, S = hgrn_recurrent(q, k, v, log_f, state_hgrn[lr].astype(jnp.float32))
                else:
                    o, S = hgrn_chunked(q, k, v, log_f, jnp.zeros((B, HG_HEADS, HG_EXPAND, HG_HEAD_V), jnp.float32))
                o = rmsnorm(o, hgrn_out_norm_g[lr]) * jax.nn.silu(zg.reshape(B, T, HG_HEADS, HG_HEAD_V))
                mix = o.reshape(B, T, D_MODEL).astype(x.dtype) @ w_out_odd[lr]
                states.append(S.astype(x.dtype))
            x = x + g1 * mix
            h = rmsnorm(x, norm_ffn_g[l]) * (1 + sc2) + sh2
            x = x + g2 * swiglu(h, w_ffn_gate[l], w_ffn_up[l], w_ffn_down[l])
        return (rmsnorm(x, final_norm_g), jnp.stack(rows_mla), jnp.stack(rows_dsa),
                jnp.stack(rows_idx), jnp.stack(states))

    y_prompt, mla_p, dsa_p, idx_p, hg_p = run(x_prompt, c_prompt, False)
    y_sample, mla_s, dsa_s, idx_s, hg_s = run(x_sample, c_sample, True)
    return (y_prompt, y_sample, mla_p, mla_s, dsa_p, dsa_s, idx_p, idx_s, hg_p, hg_s)
```

```python
import functools

import jax
import jax.numpy as jnp
import numpy as np
from jax import lax
from jax.experimental import pallas as pl
from jax.experimental.pallas import tpu as pltpu

F32 = jnp.float32
BF16 = jnp.bfloat16
I32 = jnp.int32

EPS = 1e-6
ROPE_THETA = 10000.0
TOPK_MAX = 256
NEG = -1e30
INT_MIN = -(2 ** 31)
LANES = 128
MIB = 1024 * 1024

NT = (((1,), (1,)), ((), ()))
TN = (((0,), (0,)), ((), ()))


def _cp(sem, vmem_mib=48):
    return pltpu.CompilerParams(dimension_semantics=sem, vmem_limit_bytes=vmem_mib * MIB)


def _sigmoid(x):
    return 1.0 / (1.0 + jnp.exp(-x))


def _silu(x):
    return x * _sigmoid(x)


def _rms(x, g):
    return x * lax.rsqrt(jnp.mean(x * x, axis=-1, keepdims=True) + EPS) * g


def _dot(a, b):
    return jnp.dot(a, b, preferred_element_type=F32)


def _rope_cols(x, cos, sin_signed, half):
    outs = []
    lane = lax.broadcasted_iota(I32, (x.shape[0], LANES), 1)
    first = (lane % (2 * half)) < half
    for c in range(x.shape[1] // LANES):
        xc = x[:, c * LANES:(c + 1) * LANES]
        rot = jnp.where(first, pltpu.roll(xc, LANES - half, 1), pltpu.roll(xc, half, 1))
        outs.append(xc * cos + rot * sin_signed)
    return outs[0] if len(outs) == 1 else jnp.concatenate(outs, axis=1)


def _ada_kernel(c_ref, w_ref, b_ref, o_ref):
    a = _silu(c_ref[...]).astype(BF16)
    o_ref[...] = _dot(a, w_ref[...].astype(BF16)) + b_ref[...]


def _adaln(c_all, w_ada, b_ada):
    depth, d, n = w_ada.shape
    bc = c_all.shape[0]
    tn = n // 4
    return pl.pallas_call(
        _ada_kernel,
        grid=(depth, n // tn),
        in_specs=[pl.BlockSpec((bc, d), lambda l, j: (0, 0)),
                  pl.BlockSpec((None, d, tn), lambda l, j: (l, 0, j)),
                  pl.BlockSpec((None, 1, tn), lambda l, j: (l, 0, j))],
        out_specs=pl.BlockSpec((None, bc, tn), lambda l, j: (l, 0, j)),
        out_shape=jax.ShapeDtypeStruct((depth, bc, n), F32),
        compiler_params=_cp(("parallel", "parallel")),
        name="adaln",
    )(c_all, w_ada, b_ada.reshape(depth, 1, n))


def _absorb_kernel(a_ref, b_ref, o_ref):
    o_ref[...] = lax.dot_general(a_ref[...].astype(BF16), b_ref[...].astype(BF16), NT,
                                 preferred_element_type=F32)


def _absorb(w_q_nope, w_uk):
    a = jnp.transpose(w_q_nope, (1, 0, 2))
    b = jnp.transpose(w_uk, (1, 0, 2))
    h, rq, dn = a.shape
    rkv = b.shape[1]
    out = pl.pallas_call(
        _absorb_kernel,
        grid=(h,),
        in_specs=[pl.BlockSpec((None, rq, dn), lambda i: (i, 0, 0)),
                  pl.BlockSpec((None, rkv, dn), lambda i: (i, 0, 0))],
        out_specs=pl.BlockSpec((None, rq, rkv), lambda i: (i, 0, 0)),
        out_shape=jax.ShapeDtypeStruct((h, rq, rkv), F32),
        compiler_params=_cp(("parallel",)),
        name="mla_absorb",
    )(a, b)
    return jnp.transpose(out, (1, 0, 2)).reshape(rq, h * rkv)


class EvenDims:
    def __init__(self, q_lora, kv_lora, mla_rope, mla_heads, dsa_heads, dsa_kv_heads, dsa_hd, idx_heads, idx_dim):
        assert q_lora % LANES == 0 and kv_lora == LANES and mla_rope <= LANES
        assert dsa_hd == 64 and idx_dim == 64 and dsa_kv_heads == 2 and idx_heads <= 8
        self.q_lora, self.kv_lora, self.mla_rope, self.mla_heads = q_lora, kv_lora, mla_rope, mla_heads
        self.dsa_heads, self.dsa_kv_heads, self.dsa_hd = dsa_heads, dsa_kv_heads, dsa_hd
        self.idx_heads, self.idx_dim = idx_heads, idx_dim
        self.o_zq = 0
        self.o_ckv = q_lora
        self.o_kr = self.o_ckv + kv_lora
        self.o_dq = self.o_kr + LANES
        self.o_dk = self.o_dq + dsa_heads * dsa_hd
        self.o_dv = self.o_dk + dsa_kv_heads * dsa_hd
        self.o_iq = self.o_dv + dsa_kv_heads * dsa_hd
        self.o_ik = self.o_iq + idx_heads * idx_dim
        self.n = self.o_ik + LANES
        self.mla_row = kv_lora + mla_rope


def _even_kernel(x_ref, sc_ref, sh_ref, g_ref, w_ref, qg_ref, kvg_ref, wqr_ref, wabs_ref,
                 c64_ref, s64_ref, c32_ref, s32_ref,
                 qcat_ref, row_ref, kv16_ref, dsarow_ref, dq_ref, dk_ref, dvt_ref, iq_ref, ik_ref, ik16_ref,
                 iwt_ref, *, dm, mla_scale, dsa_scale, idx_scale, idx_w_scale):
    h = _rms(x_ref[...], g_ref[...]) * (1.0 + sc_ref[0]) + sh_ref[0]
    z = _dot(h.astype(BF16), w_ref[...])
    c64, s64, c32, s32 = c64_ref[...], s64_ref[...], c32_ref[...], s32_ref[...]
    tm = z.shape[0]

    cq = _rms(z[:, dm.o_zq:dm.o_zq + dm.q_lora], qg_ref[...]).astype(BF16)
    qlat = _dot(cq, wabs_ref[...]) * mla_scale
    qrope = _rope_cols(_dot(cq, wqr_ref[...]), c32, s32, dm.mla_rope // 2) * mla_scale
    for hd in range(dm.mla_heads):
        qcat_ref[0, hd, :, 0:dm.kv_lora] = qlat[:, hd * dm.kv_lora:(hd + 1) * dm.kv_lora].astype(BF16)
        qcat_ref[0, hd, :, dm.kv_lora:dm.mla_row] = qrope[:, hd * dm.mla_rope:(hd + 1) * dm.mla_rope].astype(BF16)

    ckv = _rms(z[:, dm.o_ckv:dm.o_ckv + dm.kv_lora], kvg_ref[...])
    kr = _rope_cols(z[:, dm.o_kr:dm.o_kr + LANES], c32, s32, dm.mla_rope // 2)[:, 0:dm.mla_rope]
    row_ref[:, 0:dm.kv_lora] = ckv
    row_ref[:, dm.kv_lora:dm.mla_row] = kr
    kv16_ref[:, 0:dm.kv_lora] = ckv.astype(BF16)
    kv16_ref[:, dm.kv_lora:dm.mla_row] = kr.astype(BF16)

    hd_ = dm.dsa_hd
    dq = _rope_cols(z[:, dm.o_dq:dm.o_dk], c64, s64, hd_ // 2) * dsa_scale
    for hh in range(dm.dsa_heads):
        dq_ref[0, hh] = dq[:, hh * hd_:(hh + 1) * hd_].astype(BF16)
    dk = _rope_cols(z[:, dm.o_dk:dm.o_dv], c64, s64, hd_ // 2)
    dv = z[:, dm.o_dv:dm.o_iq]
    dsarow_ref[:, 0:LANES] = dk
    dsarow_ref[:, LANES:2 * LANES] = dv
    for gg in range(dm.dsa_kv_heads):
        dk_ref[0, gg] = dk[:, gg * hd_:(gg + 1) * hd_].astype(BF16)
    for c in range(tm // LANES):
        dvt_ref[0, c] = dv[c * LANES:(c + 1) * LANES, :].T.astype(BF16)

    iq = _rope_cols(z[:, dm.o_iq:dm.o_ik], c64, s64, dm.idx_dim // 2) * idx_scale
    for hh in range(dm.idx_heads):
        iq_ref[0, hh] = iq[:, hh * dm.idx_dim:(hh + 1) * dm.idx_dim].astype(BF16)
    last = z[:, dm.o_ik:dm.o_ik + LANES]
    ik = _rope_cols(last, c64, s64, dm.idx_dim // 2)[:, 0:dm.idx_dim]
    ik_ref[...] = ik
    ik16_ref[...] = ik.astype(BF16)
    iwt_ref[0] = last.T[dm.idx_dim:dm.idx_dim + 8, :] * idx_w_scale


def _even_project(x2d, sc, sh, g, w_in, qg, kvg, wqr, wabs, tabs, dm, bq, tq, tm, per_row, scales):
    m, d = x2d.shape
    nt = tq // tm
    nh, kvh, ih = dm.mla_heads, dm.dsa_kv_heads, dm.idx_heads
    row2 = lambda i: (i, 0)
    hm4 = lambda i: (i // nt, 0, i % nt, 0)
    if per_row:
        mod_spec = pl.BlockSpec((1, tm, d), lambda i: (0, i, 0))
    else:
        mod_spec = pl.BlockSpec((1, 1, d), lambda i: (i // nt, 0, 0))
    const = lambda shape: pl.BlockSpec(shape, lambda i: tuple(0 for _ in shape))
    tab_spec = pl.BlockSpec((tm, LANES), lambda i: (i % nt, 0))
    out_shapes = (
        jax.ShapeDtypeStruct((bq, nh, tq, dm.mla_row), BF16),
        jax.ShapeDtypeStruct((m, dm.mla_row), F32),
        jax.ShapeDtypeStruct((m, dm.mla_row), BF16),
        jax.ShapeDtypeStruct((m, 2 * LANES), F32),
        jax.ShapeDtypeStruct((bq, dm.dsa_heads, tq, dm.dsa_hd), BF16),
        jax.ShapeDtypeStruct((bq, kvh, tq, dm.dsa_hd), BF16),
        jax.ShapeDtypeStruct((bq, tq // LANES, LANES, LANES), BF16),
        jax.ShapeDtypeStruct((bq, ih, tq, dm.idx_dim), BF16),
        jax.ShapeDtypeStruct((m, dm.idx_dim), F32),
        jax.ShapeDtypeStruct((m, dm.idx_dim), BF16),
        jax.ShapeDtypeStruct((bq, 8, tq), F32),
    )
    out_specs = (
        pl.BlockSpec((1, nh, tm, dm.mla_row), hm4),
        pl.BlockSpec((tm, dm.mla_row), row2),
        pl.BlockSpec((tm, dm.mla_row), row2),
        pl.BlockSpec((tm, 2 * LANES), row2),
        pl.BlockSpec((1, dm.dsa_heads, tm, dm.dsa_hd), hm4),
        pl.BlockSpec((1, kvh, tm, dm.dsa_hd), hm4),
        pl.BlockSpec((1, tm // LANES, LANES, LANES), lambda i: (i // nt, i % nt, 0, 0)),
        pl.BlockSpec((1, ih, tm, dm.idx_dim), hm4),
        pl.BlockSpec((tm, dm.idx_dim), row2),
        pl.BlockSpec((tm, dm.idx_dim), row2),
        pl.BlockSpec((1, 8, tm), lambda i: (i // nt, 0, i % nt)),
    )
    kern = functools.partial(_even_kernel, dm=dm, mla_scale=scales[0], dsa_scale=scales[1],
                             idx_scale=scales[2], idx_w_scale=scales[3])
    return pl.pallas_call(
        kern,
        grid=(m // tm,),
        in_specs=[pl.BlockSpec((tm, d), row2), mod_spec, mod_spec, const((1, d)), const(w_in.shape),
                  const(qg.shape), const(kvg.shape), const(wqr.shape), const(wabs.shape),
                  tab_spec, tab_spec, tab_spec, tab_spec],
        out_specs=out_specs,
        out_shape=out_shapes,
        compiler_params=_cp(("parallel",), 56),
        name="even_project",
    )(x2d, sc, sh, g, w_in, qg, kvg, wqr, wabs, *tabs)


def _mla_kernel(q_ref, kv_ref, wuv_ref, o_ref, m_ref, l_ref, acc_ref, *, tq, nh, r, vd):
    qi, ki = pl.program_id(1), pl.program_id(2)

    @pl.when(ki == 0)
    def _():
        m_ref[...] = jnp.full(m_ref.shape, NEG, F32)
        l_ref[...] = jnp.zeros(l_ref.shape, F32)
        acc_ref[...] = jnp.zeros(acc_ref.shape, F32)

    def step(diag):
        q = q_ref[0].reshape(nh * tq, q_ref.shape[-1])
        kv = kv_ref[0]
        s = lax.dot_general(q, kv, NT, preferred_element_type=F32)
        if diag:
            row = lax.broadcasted_iota(I32, s.shape, 0) % tq
            col = lax.broadcasted_iota(I32, s.shape, 1)
            s = jnp.where(col <= row, s, NEG)
        m_prev = m_ref[...]
        m_new = jnp.maximum(m_prev, jnp.max(s, axis=-1, keepdims=True))
        alpha = jnp.exp(m_prev - m_new)
        p = jnp.exp(s - m_new)
        l_ref[...] = alpha * l_ref[...] + jnp.sum(p, axis=-1, keepdims=True)
        acc_ref[...] = alpha * acc_ref[...] + _dot(p.astype(BF16), kv[:, 0:r])
        m_ref[...] = m_new

    @pl.when(ki < qi)
    def _():
        step(False)

    @pl.when(ki == qi)
    def _():
        step(True)
        lat = (acc_ref[...] / l_ref[...]).astype(BF16)
        for hd in range(nh):
            o_ref[0, :, hd * vd:(hd + 1) * vd] = _dot(lat[hd * tq:(hd + 1) * tq], wuv_ref[hd]).astype(BF16)


def _mla_prompt(qcat, kv16, wuv, tq):
    b, nh, t, dq = qcat.shape
    r, vd = wuv.shape[1], wuv.shape[2]
    nq = t // tq
    kern = functools.partial(_mla_kernel, tq=tq, nh=nh, r=r, vd=vd)
    return pl.pallas_call(
        kern,
        grid=(b, nq, nq),
        in_specs=[pl.BlockSpec((1, nh, tq, dq), lambda bi, qi, ki: (bi, 0, qi, 0)),
                  pl.BlockSpec((1, tq, dq), lambda bi, qi, ki: (bi, jnp.minimum(ki, qi), 0)),
                  pl.BlockSpec(wuv.shape, lambda bi, qi, ki: (0, 0, 0))],
        out_specs=pl.BlockSpec((1, tq, nh * vd), lambda bi, qi, ki: (bi, qi, 0)),
        out_shape=jax.ShapeDtypeStruct((b, t, nh * vd), BF16),
        scratch_shapes=[pltpu.VMEM((nh * tq, 1), F32), pltpu.VMEM((nh * tq, 1), F32),
                        pltpu.VMEM((nh * tq, r), F32)],
        compiler_params=_cp(("parallel", "parallel", "arbitrary")),
        name="mla_prompt",
    )(qcat, kv16, wuv)


def _sort_key(score):
    b = pltpu.bitcast(score, I32)
    return jnp.where(b < 0, (b ^ jnp.int32(0x7FFFFFFF)) + 1, b)


def _topk_select(read_keys, read_pos, nch, rows, width, nsel, pos_bits):
    def count(pred):
        def body(j, c):
            return c + pred(j).astype(I32)
        c = lax.fori_loop(0, nch, body, jnp.zeros((rows, width), I32))
        return jnp.sum(c, axis=0, keepdims=True)

    def bit_body(b, thr):
        cand = thr + lax.shift_left(jnp.int32(1), 31 - b)
        cnt = count(lambda j: read_keys(j) >= cand)
        return jnp.where(cnt >= nsel, cand, thr)

    thr = lax.fori_loop(0, 32, bit_body, jnp.full((1, width), INT_MIN, I32))
    need = nsel - count(lambda j: read_keys(j) > thr)

    def jbit_body(b, jc):
        cand = jc + lax.shift_left(jnp.int32(1), pos_bits - 1 - b)
        cnt = count(lambda j: (read_keys(j) == thr) & (read_pos(j) < cand))
        return jnp.where(cnt <= need, cand, jc)

    jcut = lax.fori_loop(0, pos_bits, jbit_body, jnp.zeros((1, width), I32))
    return thr, jcut


def _selected(keys, pos, thr, jcut):
    return ((keys > thr) | ((keys == thr) & (pos < jcut))) & (keys != INT_MIN)


def _dsa_kernel(iq_ref, iwt_ref, ik_ref, dq_ref, dk_ref, dvt_ref, o_ref, keys_ref, bias_ref, s_ref,
                *, tq, nsel, ih, gsz, kvh, hd, pos_bits):
    i = pl.program_id(1)
    nch = i + 1
    row = lax.broadcasted_iota(I32, (tq, tq), 0)
    qpos = i * tq + lax.broadcasted_iota(I32, (tq, tq), 1)

    def off(j):
        return pl.multiple_of(j * tq, tq)

    def score_chunk(j, carry):
        ikc = ik_ref[0, pl.ds(off(j), tq), :]
        acc = jnp.zeros((tq, tq), F32)
        for hh in range(ih):
            d = lax.dot_general(ikc, iq_ref[0, hh], NT, preferred_element_type=F32)
            acc = acc + iwt_ref[0, hh:hh + 1, :] * jnp.maximum(d, 0.0)
        valid = (j * tq + row) <= qpos
        keys_ref[pl.ds(off(j), tq), :] = jnp.where(valid, _sort_key(acc), INT_MIN)
        return carry

    lax.fori_loop(0, nch, score_chunk, 0)

    read_keys = lambda j: keys_ref[pl.ds(off(j), tq), :]
    read_pos = lambda j: j * tq + row
    thr, jcut = _topk_select(read_keys, read_pos, nch, tq, tq, nsel, pos_bits)

    def bias_chunk(j, carry):
        sel = _selected(read_keys(j), read_pos(j), thr, jcut)
        bias_ref[pl.ds(off(j), tq), :] = jnp.where(sel, 0.0, NEG)
        return carry

    lax.fori_loop(0, nch, bias_chunk, 0)

    outs = []
    for g in range(kvh):
        qs = dq_ref[0, g * gsz:(g + 1) * gsz].reshape(gsz * tq, hd)

        def pass_a(j, m):
            s = lax.dot_general(dk_ref[0, g, pl.ds(off(j), tq), :], qs, NT, preferred_element_type=F32)
            b = bias_ref[pl.ds(off(j), tq), :]
            s = s + jnp.concatenate([b] * gsz, axis=1)
            s_ref[pl.ds(off(j), tq), :] = s
            return jnp.maximum(m, jnp.max(s, axis=0, keepdims=True))

        m = lax.fori_loop(0, nch, pass_a, jnp.full((1, gsz * tq), NEG, F32))

        def pass_b(j, carry):
            l, acc = carry
            p = jnp.exp(s_ref[pl.ds(off(j), tq), :] - m)
            l = l + jnp.sum(p, axis=0, keepdims=True)
            acc = acc + _dot(dvt_ref[0, j, g * hd:(g + 1) * hd, :], p.astype(BF16))
            return l, acc

        l, acc = lax.fori_loop(0, nch, pass_b,
                               (jnp.zeros((1, gsz * tq), F32), jnp.zeros((hd, gsz * tq), F32)))
        ot = acc / l
        for hh in range(gsz):
            outs.append(ot[:, hh * tq:(hh + 1) * tq])
    o_ref[0] = jnp.concatenate(outs, axis=0).T.astype(BF16)


def _dsa_prompt(iq, iwt, ik16, dq, dk, dvt, nsel, tq):
    b, ih, t, idim = iq.shape
    nhd, hd = dq.shape[1], dq.shape[3]
    kvh = dk.shape[1]
    gsz = nhd // kvh
    ik3 = ik16.reshape(b, t, idim)
    pos_bits = int(t).bit_length()
    kern = functools.partial(_dsa_kernel, tq=tq, nsel=nsel, ih=ih, gsz=gsz, kvh=kvh, hd=hd, pos_bits=pos_bits)
    return pl.pallas_call(
        kern,
        grid=(b, t // tq),
        in_specs=[pl.BlockSpec((1, ih, tq, idim), lambda bi, i: (bi, 0, i, 0)),
                  pl.BlockSpec((1, 8, tq), lambda bi, i: (bi, 0, i)),
                  pl.BlockSpec((1, t, idim), lambda bi, i: (bi, 0, 0)),
                  pl.BlockSpec((1, nhd, tq, hd), lambda bi, i: (bi, 0, i, 0)),
                  pl.BlockSpec((1, kvh, t, hd), lambda bi, i: (bi, 0, 0, 0)),
                  pl.BlockSpec((1, t // LANES, LANES, LANES), lambda bi, i: (bi, 0, 0, 0))],
        out_specs=pl.BlockSpec((1, tq, nhd * hd), lambda bi, i: (bi, i, 0)),
        out_shape=jax.ShapeDtypeStruct((b, t, nhd * hd), BF16),
        scratch_shapes=[pltpu.VMEM((t, tq), I32), pltpu.VMEM((t, tq), F32), pltpu.VMEM((t, gsz * tq), F32)],
        compiler_params=_cp(("parallel", "arbitrary")),
        name="dsa_prompt",
    )(iq, iwt, ik3, dq, dk, dvt)


def _ffn_kernel(*refs, n_mix, final):
    x_ref = refs[0]
    mix_refs = refs[1:1 + 2 * n_mix]
    (g1_ref, nfg_ref, sc_ref, sh_ref, g2_ref, wg_ref, wu_ref, wd_ref, fg_ref,
     o_ref, x1_ref, hb_ref, acc_ref) = refs[1 + 2 * n_mix:]
    f = pl.program_id(1)

    @pl.when(f == 0)
    def _():
        mix = _dot(mix_refs[0][...], mix_refs[1][...])
        for k in range(1, n_mix):
            mix = mix + _dot(mix_refs[2 * k][...], mix_refs[2 * k + 1][...])
        x1 = x_ref[...] + g1_ref[0] * mix
        x1_ref[...] = x1
        hb_ref[...] = (_rms(x1, nfg_ref[...]) * (1.0 + sc_ref[0]) + sh_ref[0]).astype(BF16)
        acc_ref[...] = jnp.zeros(acc_ref.shape, F32)

    hb = hb_ref[...]
    act = _silu(_dot(hb, wg_ref[...])) * _dot(hb, wu_ref[...])
    acc_ref[...] += _dot(act.astype(BF16), wd_ref[...])

    @pl.when(f == pl.num_programs(1) - 1)
    def _():
        y = x1_ref[...] + g2_ref[0] * acc_ref[...]
        if final:
            y = _rms(y, fg_ref[...])
        o_ref[...] = y


def _mix_ffn(x2d, mixes, g1, nfg, sc2, sh2, g2, wg, wu, wd, fg, tq, tm, tf, per_row, final):
    m, d = x2d.shape
    dff = wg.shape[1]
    nt = tq // tm
    if per_row:
        mod_spec = pl.BlockSpec((1, tm, d), lambda i, f: (0, i, 0))
    else:
        mod_spec = pl.BlockSpec((1, 1, d), lambda i, f: (i // nt, 0, 0))
    row = lambda i, f: (i, 0)
    const2 = lambda shape: pl.BlockSpec(shape, lambda i, f: (0, 0))
    in_specs = [pl.BlockSpec((tm, d), row)]
    args = [x2d]
    for a, w in mixes:
        in_specs += [pl.BlockSpec((tm, a.shape[1]), row), const2(w.shape)]
        args += [a, w]
    in_specs += [mod_spec, const2((1, d)), mod_spec, mod_spec, mod_spec,
                 pl.BlockSpec((d, tf), lambda i, f: (0, f)), pl.BlockSpec((d, tf), lambda i, f: (0, f)),
                 pl.BlockSpec((tf, d), lambda i, f: (f, 0)), const2((1, d))]
    args += [g1, nfg, sc2, sh2, g2, wg, wu, wd, fg]
    kern = functools.partial(_ffn_kernel, n_mix=len(mixes), final=final)
    return pl.pallas_call(
        kern,
        grid=(m // tm, dff // tf),
        in_specs=in_specs,
        out_specs=pl.BlockSpec((tm, d), row),
        out_shape=jax.ShapeDtypeStruct((m, d), F32),
        scratch_shapes=[pltpu.VMEM((tm, d), F32), pltpu.VMEM((tm, d), BF16), pltpu.VMEM((tm, d), F32)],
        compiler_params=_cp(("parallel", "arbitrary"), 56),
        name="mix_ffn",
    )(*args)


def _normmod_mm_kernel(x_ref, sc_ref, sh_ref, g_ref, w_ref, o_ref):
    h = _rms(x_ref[...], g_ref[...]) * (1.0 + sc_ref[0]) + sh_ref[0]
    o_ref[...] = _dot(h.astype(BF16), w_ref[...])


def _normmod_mm(x2d, sc, sh, g, w, tq, tm, tn, per_row):
    m, d = x2d.shape
    n = w.shape[1]
    nt = tq // tm
    if per_row:
        mod_spec = pl.BlockSpec((1, tm, d), lambda i, j: (0, i, 0))
    else:
        mod_spec = pl.BlockSpec((1, 1, d), lambda i, j: (i // nt, 0, 0))
    return pl.pallas_call(
        _normmod_mm_kernel,
        grid=(m // tm, n // tn),
        in_specs=[pl.BlockSpec((tm, d), lambda i, j: (i, 0)), mod_spec, mod_spec,
                  pl.BlockSpec((1, d), lambda i, j: (0, 0)), pl.BlockSpec((d, tn), lambda i, j: (0, j))],
        out_specs=pl.BlockSpec((tm, tn), lambda i, j: (i, j)),
        out_shape=jax.ShapeDtypeStruct((m, n), F32),
        compiler_params=_cp(("parallel", "parallel")),
        name="odd_in_proj",
    )(x2d, sc, sh, g, w)


def _hgrn_consts(c):
    nlev = int(np.log2(c))
    t = np.arange(c)
    tril = (t[None, :] <= t[:, None]).astype(np.float32)
    mats = [tril]
    masks = [np.eye(c, dtype=np.float32)]
    for lev in range(1, nlev + 1):
        w = 2 ** (lev - 1)
        piv = (t // (2 * w)) * (2 * w) + w - 1
        mats.append(tril - tril[piv])
        same = (t[:, None] // (2 * w)) == (t[None, :] // (2 * w))
        right = ((t // w) % 2 == 1)[:, None]
        left = ((t // w) % 2 == 0)[None, :]
        masks.append((same & right & left).astype(np.float32))
    return np.concatenate(mats, axis=0), np.stack(masks), nlev


def _split3(x):
    a = x.astype(BF16)
    r = x - a.astype(F32)
    b = r.astype(BF16)
    c = (r - b.astype(F32)).astype(BF16)
    return a, b, c


def _hgrn_kernel(zq_ref, zf_ref, zi_ref, zg_ref, lb_ref, og_ref, cm_ref, bm_ref, o_ref, st_ref, s_ref,
                 *, c, nlev, q_scale):
    ci = pl.program_id(2)

    @pl.when(ci == 0)
    def _():
        s_ref[...] = jnp.zeros(s_ref.shape, F32)

    lb = lb_ref[0]
    zf = zf_ref[...]
    e = jnp.exp(-jnp.abs(zf))
    inv = 1.0 / (1.0 + e)
    sig_pos = jnp.where(zf >= 0, inv, e * inv)
    sig_neg = jnp.where(zf >= 0, e * inv, inv)
    logf = jnp.log(lb + (1.0 - lb) * sig_pos)
    k = (1.0 - lb) * sig_neg
    q = _silu(zq_ref[...]) * q_scale
    v = zi_ref[...].astype(BF16)

    a3 = jnp.concatenate(_split3(logf), axis=1)
    gd = _dot(cm_ref[...], a3)
    kk = logf.shape[1]
    gd = gd[:, 0:kk] + gd[:, kk:2 * kk] + gd[:, 2 * kk:3 * kk]
    gcum = gd[0:c]
    glast = gcum[c - 1:c]

    amat = bm_ref[0] * lax.dot_general(q.astype(BF16), k.astype(BF16), NT, preferred_element_type=F32)
    for lev in range(1, nlev + 1):
        ed = jnp.exp(-jnp.abs(gd[lev * c:(lev + 1) * c]))
        amat = amat + bm_ref[lev] * lax.dot_general((q * ed).astype(BF16), (k * ed).astype(BF16), NT,
                                                    preferred_element_type=F32)
    s_prev = s_ref[...]
    o = _dot(amat.astype(BF16), v) + _dot((q * jnp.exp(gcum)).astype(BF16), s_prev.astype(BF16))
    kdt = (k * jnp.exp(glast - gcum)).T.astype(BF16)
    decay_col = jnp.broadcast_to(jnp.exp(glast), (kk, kk)).T
    s_new = decay_col * s_prev + _dot(kdt, v)
    s_ref[...] = s_new

    o_ref[...] = (_rms(o, og_ref[...]) * _silu(zg_ref[...])).astype(BF16)

    @pl.when(ci == pl.num_programs(2) - 1)
    def _():
        st_ref[...] = s_new


def _hgrn_prompt(z, lb, og, b, t, nh, c, q_scale):
    m = z.shape[0]
    kk = lb.shape[-1]
    vd = og.shape[-1]
    nc = t // c
    cm, bm, nlev = _hgrn_consts(c)
    cm = jnp.asarray(cm, BF16)
    bm = jnp.asarray(bm, F32)
    seg = lambda s: pl.BlockSpec((c, kk), lambda bi, h, ci: (bi * nc + ci, s * nh + h))
    kern = functools.partial(_hgrn_kernel, c=c, nlev=nlev, q_scale=q_scale)
    return pl.pallas_call(
        kern,
        grid=(b, nh, nc),
        in_specs=[seg(0), seg(1), seg(2), seg(3),
                  pl.BlockSpec((1, 1, kk), lambda bi, h, ci: (h, 0, 0)),
                  pl.BlockSpec((1, vd), lambda bi, h, ci: (0, 0)),
                  pl.BlockSpec(cm.shape, lambda bi, h, ci: (0, 0)),
                  pl.BlockSpec(bm.shape, lambda bi, h, ci: (0, 0, 0))],
        out_specs=(pl.BlockSpec((c, vd), lambda bi, h, ci: (bi * nc + ci, h)),
                   pl.BlockSpec((None, None, kk, vd), lambda bi, h, ci: (bi, h, 0, 0))),
        out_shape=(jax.ShapeDtypeStruct((m, nh * vd), BF16),
                   jax.ShapeDtypeStruct((b, nh, kk, vd), F32)),
        scratch_shapes=[pltpu.VMEM((kk, vd), F32)],
        compiler_params=_cp(("parallel", "parallel", "arbitrary")),
        name="hgrn_prompt",
    )(z, z, z, z, lb, og, cm, bm)


def _hgrn_step_kernel(z_ref, lb_ref, og_ref, s_ref, o_ref, st_ref, *, nh, kk, vd, bt, q_scale):
    z = z_ref[...]
    hf = nh * kk
    for h in range(nh):
        lb = lb_ref[h]
        zf = z[:, hf + h * kk:hf + (h + 1) * kk]
        e = jnp.exp(-jnp.abs(zf))
        inv = 1.0 / (1.0 + e)
        f = lb + (1.0 - lb) * jnp.where(zf >= 0, inv, e * inv)
        k = (1.0 - lb) * jnp.where(zf >= 0, e * inv, inv)
        q = _silu(z[:, h * kk:(h + 1) * kk]) * q_scale
        v = z[:, 2 * hf + h * vd:2 * hf + (h + 1) * vd]
        gate = _silu(z[:, 2 * hf + nh * vd + h * vd:2 * hf + nh * vd + (h + 1) * vd])
        ft, kt, qt = f.T, k.T, q.T
        for j in range(bt):
            s_new = ft[:, j:j + 1] * s_ref[j, h] + kt[:, j:j + 1] * v[j:j + 1, :]
            st_ref[j, h] = s_new
            o = jnp.sum(qt[:, j:j + 1] * s_new, axis=0, keepdims=True)
            o_ref[j:j + 1, h * vd:(h + 1) * vd] = (_rms(o, og_ref[...]) * gate[j:j + 1, :]).astype(BF16)


def _hgrn_step(z, lb, og, state, nh, q_scale):
    bsz = z.shape[0]
    kk, vd = state.shape[2], state.shape[3]
    bt = 8
    kern = functools.partial(_hgrn_step_kernel, nh=nh, kk=kk, vd=vd, bt=bt, q_scale=q_scale)
    return pl.pallas_call(
        kern,
        grid=(bsz // bt,),
        in_specs=[pl.BlockSpec((bt, z.shape[1]), lambda i: (i, 0)),
                  pl.BlockSpec(lb.shape, lambda i: (0, 0, 0)),
                  pl.BlockSpec((1, vd), lambda i: (0, 0)),
                  pl.BlockSpec((bt, nh, kk, vd), lambda i: (i, 0, 0, 0))],
        out_specs=(pl.BlockSpec((bt, nh * vd), lambda i: (i, 0)),
                   pl.BlockSpec((bt, nh, kk, vd), lambda i: (i, 0, 0, 0))),
        out_shape=(jax.ShapeDtypeStruct((bsz, nh * vd), BF16),
                   jax.ShapeDtypeStruct(state.shape, F32)),
        compiler_params=_cp(("parallel",)),
        name="hgrn_step",
    )(z, lb, og, state)


def _mla_dec_kernel(pt_ref, q_ref, new_ref, wuv_ref, *rest, pp, r, vd, nh):
    pages = rest[:pp]
    o_ref, m_ref, l_ref, acc_ref = rest[pp:]
    g = pl.program_id(1)
    q = q_ref[...]

    @pl.when(g == 0)
    def _():
        new = new_ref[...]
        s = lax.dot_general(q, new, NT, preferred_element_type=F32)
        s = jnp.where(lax.broadcasted_iota(I32, s.shape, 1) == 0, s, NEG)
        m = jnp.max(s, axis=-1, keepdims=True)
        p = jnp.exp(s - m)
        m_ref[...] = m
        l_ref[...] = jnp.sum(p, axis=-1, keepdims=True)
        acc_ref[...] = _dot(p.astype(BF16), new[:, 0:r])

    kvs = [pg[...].astype(BF16) for pg in pages]
    s = jnp.concatenate([_dot(q, kv) for kv in kvs], axis=1)
    m_prev = m_ref[...]
    m_new = jnp.maximum(m_prev, jnp.max(s, axis=-1, keepdims=True))
    alpha = jnp.exp(m_prev - m_new)
    p = jnp.exp(s - m_new).astype(BF16)
    l_ref[...] = alpha * l_ref[...] + jnp.sum(p.astype(F32), axis=-1, keepdims=True)
    acc = alpha * acc_ref[...]
    psz = kvs[0].shape[1]
    for i, kv in enumerate(kvs):
        acc = acc + lax.dot_general(p[:, i * psz:(i + 1) * psz], kv[0:r, :], NT, preferred_element_type=F32)
    acc_ref[...] = acc
    m_ref[...] = m_new

    @pl.when(g == pl.num_programs(1) - 1)
    def _():
        lat = (acc_ref[...] / l_ref[...]).astype(BF16)
        for hd in range(nh):
            o_ref[0:1, hd * vd:(hd + 1) * vd] = _dot(lat[hd:hd + 1], wuv_ref[hd])
        o_ref[1:8, :] = jnp.zeros((7, nh * vd), F32)


def _mla_decode(page_table, qdec, newrow, wuv, cache, pp):
    bsz, nh, dq = qdec.shape
    n_pages = page_table.shape[1]
    psz = cache.shape[2]
    r, vd = wuv.shape[1], wuv.shape[2]
    pt = page_table.reshape(-1)

    def page_map(k):
        return lambda b, g, ptr: (ptr[b * n_pages + g * pp + k], 0, 0)

    kern = functools.partial(_mla_dec_kernel, pp=pp, r=r, vd=vd, nh=nh)
    grid_spec = pltpu.PrefetchScalarGridSpec(
        num_scalar_prefetch=1,
        grid=(bsz, n_pages // pp),
        in_specs=[pl.BlockSpec((None, nh, dq), lambda b, g, ptr: (b, 0, 0)),
                  pl.BlockSpec((None, 8, dq), lambda b, g, ptr: (b, 0, 0)),
                  pl.BlockSpec(wuv.shape, lambda b, g, ptr: (0, 0, 0))]
                 + [pl.BlockSpec((None, dq, psz), page_map(k)) for k in range(pp)],
        out_specs=pl.BlockSpec((None, 8, nh * vd), lambda b, g, ptr: (b, 0, 0)),
        scratch_shapes=[pltpu.VMEM((nh, 1), F32), pltpu.VMEM((nh, 1), F32), pltpu.VMEM((nh, r), F32)],
    )
    out = pl.pallas_call(
        kern,
        grid_spec=grid_spec,
        out_shape=jax.ShapeDtypeStruct((bsz, 8, nh * vd), F32),
        compiler_params=_cp(("parallel", "arbitrary")),
        name="mla_decode",
    )(pt, qdec, newrow, wuv, *([cache] * pp))
    return out[:, 0, :]


def _idx_dec_kernel(pt_ref, iq_ref, iw_ref, iknew_ref, *rest, pp, n_pages, nsel, pos_bits):
    pages = rest[:pp]
    bias_ref, keys_ref = rest[pp:]
    g = pl.program_id(1)
    iq = iq_ref[...]
    iw = iw_ref[...]
    psz = pages[0].shape[1]

    def score(d):
        return jnp.sum(iw * jnp.maximum(d, 0.0), axis=0, keepdims=True)

    @pl.when(g == 0)
    def _():
        dnew = lax.dot_general(iq, iknew_ref[...], NT, preferred_element_type=F32)
        snew = _sort_key(score(dnew))
        slot = (lax.broadcasted_iota(I32, (8, psz), 0) == 0) & (lax.broadcasted_iota(I32, (8, psz), 1) == 0)
        keys_ref[n_pages:n_pages + 8, :] = jnp.where(slot, jnp.broadcast_to(snew, (8, psz)), INT_MIN)

    rows = jnp.concatenate([_sort_key(score(_dot(iq, pg[...].astype(BF16)))) for pg in pages], axis=0)
    keys_ref[pl.ds(pl.multiple_of(g * pp, pp), pp), :] = rows

    @pl.when(g == pl.num_programs(1) - 1)
    def _():
        rows = keys_ref.shape[0]
        pos = (lax.broadcasted_iota(I32, (rows, psz), 0) * psz + lax.broadcasted_iota(I32, (rows, psz), 1))
        keys = keys_ref[...]

        def count(pred):
            return jnp.sum(jnp.sum(pred.astype(I32), axis=0, keepdims=True), axis=1, keepdims=True)

        def bit_body(b, thr):
            cand = thr + lax.shift_left(jnp.int32(1), 31 - b)
            return jnp.where(count(keys >= cand) >= nsel, cand, thr)

        thr = lax.fori_loop(0, 32, bit_body, jnp.full((1, 1), INT_MIN, I32))
        need = nsel - count(keys > thr)

        def jbit_body(b, jc):
            cand = jc + lax.shift_left(jnp.int32(1), pos_bits - 1 - b)
            cnt = count((keys == thr) & (pos < cand))
            return jnp.where(cnt <= need, cand, jc)

        jcut = lax.fori_loop(0, pos_bits, jbit_body, jnp.zeros((1, 1), I32))
        bias_ref[...] = jnp.where(_selected(keys, pos, thr, jcut), 0.0, NEG)


def _idx_decode(page_table, iqd, iwd, iknew, cache, nsel, pp):
    assert pp % 8 == 0
    bsz = iqd.shape[0]
    idim = iqd.shape[2]
    n_pages = page_table.shape[1]
    psz = cache.shape[2]
    rows = n_pages + 8
    pos_bits = int(rows * psz).bit_length()
    pt = page_table.reshape(-1)

    def page_map(k):
        return lambda b, g, ptr: (ptr[b * n_pages + g * pp + k], 0, 0)

    kern = functools.partial(_idx_dec_kernel, pp=pp, n_pages=n_pages, nsel=nsel, pos_bits=pos_bits)
    grid_spec = pltpu.PrefetchScalarGridSpec(
        num_scalar_prefetch=1,
        grid=(bsz, n_pages // pp),
        in_specs=[pl.BlockSpec((None, 8, idim), lambda b, g, ptr: (b, 0, 0)),
                  pl.BlockSpec((None, 8, 1), lambda b, g, ptr: (b, 0, 0)),
                  pl.BlockSpec((None, psz, idim), lambda b, g, ptr: (b, 0, 0))]
                 + [pl.BlockSpec((None, idim, psz), page_map(k)) for k in range(pp)],
        out_specs=pl.BlockSpec((None, rows, psz), lambda b, g, ptr: (b, 0, 0)),
        scratch_shapes=[pltpu.VMEM((rows, psz), I32)],
    )
    return pl.pallas_call(
        kern,
        grid_spec=grid_spec,
        out_shape=jax.ShapeDtypeStruct((bsz, rows, psz), F32),
        compiler_params=_cp(("parallel", "arbitrary")),
        name="idx_decode",
    )(pt, iqd, iwd, iknew, *([cache] * pp))


def _dsa_dec_kernel(pt_ref, q_ref, new_ref, bias_ref, *rest, pp, n_pages, hd, gsz):
    pages = rest[:pp]
    o_ref, m_ref, l_ref, acc_ref = rest[pp:]
    g = pl.program_id(1)
    q = q_ref[...]
    kw = 2 * hd

    @pl.when(g == 0)
    def _():
        new = new_ref[...]
        s = lax.dot_general(q, new[:, 0:kw], NT, preferred_element_type=F32)
        s = s + bias_ref[n_pages:n_pages + 1, 0:8]
        m = jnp.max(s, axis=-1, keepdims=True)
        p = jnp.exp(s - m)
        m_ref[...] = m
        l_ref[...] = jnp.sum(p, axis=-1, keepdims=True)
        acc_ref[...] = _dot(p.astype(BF16), new[:, kw:2 * kw])

    kvs = [pg[...].astype(BF16) for pg in pages]
    bias = bias_ref[pl.ds(pl.multiple_of(g * pp, pp), pp), :]
    s = jnp.concatenate(
        [_dot(q, kv[0:kw, :]) + bias[i:i + 1, :] for i, kv in enumerate(kvs)], axis=1)
    m_prev = m_ref[...]
    m_new = jnp.maximum(m_prev, jnp.max(s, axis=-1, keepdims=True))
    alpha = jnp.exp(m_prev - m_new)
    p = jnp.exp(s - m_new).astype(BF16)
    l_ref[...] = alpha * l_ref[...] + jnp.sum(p.astype(F32), axis=-1, keepdims=True)
    acc = alpha * acc_ref[...]
    psz = kvs[0].shape[1]
    for i, kv in enumerate(kvs):
        acc = acc + lax.dot_general(p[:, i * psz:(i + 1) * psz], kv[kw:2 * kw, :], NT,
                                    preferred_element_type=F32)
    acc_ref[...] = acc
    m_ref[...] = m_new

    @pl.when(g == pl.num_programs(1) - 1)
    def _():
        o = acc_ref[...] / l_ref[...]
        rowi = lax.broadcasted_iota(I32, o.shape, 0)
        o_ref[...] = jnp.where(rowi < gsz, o, pltpu.roll(o, hd, 1))


def _dsa_decode(page_table, qbd, newkv, bias, cache, pp, hd, gsz):
    bsz = qbd.shape[0]
    n_pages = page_table.shape[1]
    psz = cache.shape[2]
    pt = page_table.reshape(-1)

    def page_map(k):
        return lambda b, g, ptr: (ptr[b * n_pages + g * pp + k], 0, 0)

    kern = functools.partial(_dsa_dec_kernel, pp=pp, n_pages=n_pages, hd=hd, gsz=gsz)
    grid_spec = pltpu.PrefetchScalarGridSpec(
        num_scalar_prefetch=1,
        grid=(bsz, n_pages // pp),
        in_specs=[pl.BlockSpec((None, 8, 2 * hd), lambda b, g, ptr: (b, 0, 0)),
                  pl.BlockSpec((None, 8, 4 * hd), lambda b, g, ptr: (b, 0, 0)),
                  pl.BlockSpec((None, n_pages + 8, psz), lambda b, g, ptr: (b, 0, 0))]
                 + [pl.BlockSpec((None, 4 * hd, psz), page_map(k)) for k in range(pp)],
        out_specs=pl.BlockSpec((None, 8, 2 * hd), lambda b, g, ptr: (b, 0, 0)),
        scratch_shapes=[pltpu.VMEM((8, 1), F32), pltpu.VMEM((8, 1), F32), pltpu.VMEM((8, 2 * hd), F32)],
    )
    return pl.pallas_call(
        kern,
        grid_spec=grid_spec,
        out_shape=jax.ShapeDtypeStruct((bsz, 8, 2 * hd), F32),
        compiler_params=_cp(("parallel", "arbitrary")),
        name="dsa_decode",
    )(pt, qbd, newkv, bias, *([cache] * pp))


def _rope_tables(pos, half):
    inv_freq = ROPE_THETA ** (-jnp.arange(half, dtype=F32) / half)
    ang = pos.astype(F32)[:, None] * inv_freq[None, :]
    cos, sin = jnp.cos(ang), jnp.sin(ang)
    reps = LANES // (2 * half)
    return (jnp.tile(jnp.concatenate([cos, cos], axis=1), (1, reps)),
            jnp.tile(jnp.concatenate([-sin, sin], axis=1), (1, reps)))


def _pick(n, pref):
    for c in pref:
        if n % c == 0:
            return c
    return n


def kernel(x_prompt, x_sample, cache_mla, cache_dsa_kv, cache_idx, state_hgrn, page_table, c_prompt, c_sample,
           w_ada, b_ada, norm_mix_g, norm_ffn_g, w_in_even, mla_q_norm_g, mla_kv_norm_g, mla_w_q_nope,
           mla_w_q_rope, mla_w_uk, mla_w_uv, w_out_even, w_in_odd, hgrn_lower_bounds, hgrn_out_norm_g,
           w_out_odd, w_ffn_gate, w_ffn_up, w_ffn_down, final_norm_g):
    depth, d, _ = w_ada.shape
    bp, tp, _ = x_prompt.shape
    bs, ts, _ = x_sample.shape
    assert ts == 1
    q_lora, mla_heads, mla_nope = mla_w_q_nope.shape[1:]
    mla_rope = mla_w_q_rope.shape[3]
    kv_lora, _, mla_v = mla_w_uv.shape[1:]
    dsa_kv_heads, dsa_hd = cache_dsa_kv.shape[4:]
    idx_dim = cache_idx.shape[3]
    n_pages = page_table.shape[1]
    psz = cache_mla.shape[2]
    past_len = n_pages * psz
    sizes_tail = w_in_even.shape[2] - (q_lora + kv_lora + mla_rope + 2 * dsa_kv_heads * dsa_hd + idx_dim)
    dsa_heads = (w_out_even.shape[1] - mla_heads * mla_v) // dsa_hd
    idx_heads = (sizes_tail - dsa_heads * dsa_hd) // (idx_dim + 1)
    dm = EvenDims(q_lora, kv_lora, mla_rope, mla_heads, dsa_heads, dsa_kv_heads, dsa_hd, idx_heads, idx_dim)
    gsz = dsa_heads // dsa_kv_heads
    hg_f = hgrn_lower_bounds.shape[1]
    hg_heads = state_hgrn.shape[2]
    hg_k = state_hgrn.shape[3]
    hg_v = state_hgrn.shape[4]
    scales = ((mla_nope + mla_rope) ** -0.5, dsa_hd ** -0.5, idx_dim ** -0.5, idx_heads ** -0.5)
    hg_q_scale = hg_k ** -0.5

    lb_all = jax.nn.softmax(hgrn_lower_bounds.astype(F32), axis=0)
    lb_all = jnp.cumsum(lb_all, axis=0) - lb_all[0]

    c_all = jnp.concatenate([c_prompt, c_sample], axis=0)
    mods = _adaln(c_all, w_ada, b_ada)

    def mods_for(l, sample):
        mm = mods[l, bp:] if sample else mods[l, :bp]
        parts = [mm[:, k * d:(k + 1) * d] for k in range(6)]
        return [p[None] if sample else p[:, None, :] for p in parts]

    bf = lambda a: a.astype(BF16)
    row1 = lambda a: a.reshape(1, -1).astype(F32)

    def even_weights(la):
        w = w_in_even[la]
        splits = np.cumsum([q_lora, kv_lora, mla_rope, dsa_heads * dsa_hd, dsa_kv_heads * dsa_hd,
                            dsa_kv_heads * dsa_hd, idx_heads * idx_dim, idx_dim])
        wq, wckv, wkr, wdq, wdk, wdv, wiq, wik, wiw = jnp.split(w, splits, axis=1)
        padc = lambda a, n: jnp.pad(a, ((0, 0), (0, n - a.shape[1])))
        w_cat = jnp.concatenate([wq, wckv, padc(wkr, LANES), wdq, wdk, wdv, wiq,
                                 padc(jnp.concatenate([wik, wiw], axis=1), LANES)], axis=1)
        assert w_cat.shape[1] == dm.n
        wabs = _absorb(mla_w_q_nope[la], mla_w_uk[la])
        wqr = mla_w_q_rope[la].reshape(q_lora, mla_heads * mla_rope)
        wuv = jnp.transpose(mla_w_uv[la], (1, 0, 2))
        return bf(w_cat), bf(wabs), bf(wqr), bf(wuv)

    def run(x, sample):
        b, t, _ = x.shape
        m = b * t
        x2d = x.reshape(m, d)
        if sample:
            bq, tq, per_row = 1, m, True
            tm = _pick(m, (128,))
            pos = jnp.full((m,), past_len, I32)
        else:
            bq, tq, per_row = b, t, False
            tm = _pick(t, (512, 256, 128))
            pos = jnp.arange(t)
        tabs = (*_rope_tables(pos, dsa_hd // 2), *_rope_tables(pos, mla_rope // 2))
        rows_mla, rows_dsa, rows_idx, states = [], [], [], []
        for l in range(depth):
            sh1, sc1, g1, sh2, sc2, g2 = mods_for(l, sample)
            nmg, nfg = row1(norm_mix_g[l]), row1(norm_ffn_g[l])
            final = l == depth - 1
            if l % 2 == 0:
                la = l // 2
                w_cat, wabs, wqr, wuv = even_weights(la)
                (qcat, mrow, kv16, dsarow, dq, dk, dvt, iq, ik, ik16, iwt) = _even_project(
                    x2d, sc1, sh1, nmg, w_cat, row1(mla_q_norm_g[la]), row1(mla_kv_norm_g[la]), wqr, wabs,
                    tabs, dm, bq, tq, tm, per_row, scales)
                if sample:
                    nsel = min(TOPK_MAX, (past_len + t) // 4)
                    pp = _pick(n_pages, (16, 8))
                    pad8 = lambda a: jnp.pad(a[:, None, :], ((0, 0), (0, 7), (0, 0)))
                    padp = lambda a: jnp.pad(a[:, None, :], ((0, 0), (0, psz - 1), (0, 0)))
                    qdec = jnp.transpose(qcat[0], (1, 0, 2))
                    page_t = lambda c: jnp.transpose(c, (0, 2, 1))
                    mla_o = _mla_decode(page_table, qdec, pad8(kv16), wuv, page_t(cache_mla[la]), pp)
                    iqd = jnp.pad(jnp.transpose(iq[0], (1, 0, 2)), ((0, 0), (0, 8 - idx_heads), (0, 0)))
                    iwd = jnp.transpose(iwt[0])[:, :, None]
                    iwd = jnp.where(jnp.arange(8)[None, :, None] < idx_heads, iwd, 0.0)
                    bias = _idx_decode(page_table, iqd, iwd, padp(ik16), page_t(cache_idx[la]), nsel, pp)
                    dqd = jnp.transpose(dq[0], (1, 0, 2))
                    zeros = jnp.zeros_like(dqd[:, :gsz])
                    qbd = jnp.concatenate(
                        [jnp.concatenate([dqd[:, :gsz], zeros], axis=2),
                         jnp.concatenate([zeros, dqd[:, gsz:]], axis=2)], axis=1)
                    cache_kv = jnp.transpose(cache_dsa_kv[la], (0, 2, 3, 4, 1)).reshape(-1, 4 * dsa_hd, psz)
                    dsa_o8 = _dsa_decode(page_table, qbd, pad8(bf(dsarow)), bias, cache_kv, pp, dsa_hd, gsz)
                    dsa_o = dsa_o8[:, :, :dsa_hd].reshape(b, dsa_heads * dsa_hd)
                    mla_o, dsa_o = bf(mla_o), bf(dsa_o)
                else:
                    nsel = min(TOPK_MAX, t // 4)
                    mla_o = _mla_prompt(qcat, kv16.reshape(b, t, -1), wuv, _pick(t, (256, 128))).reshape(m, -1)
                    dsa_o = _dsa_prompt(iq, iwt, ik16, dq, dk, dvt, nsel, LANES).reshape(m, -1)
                n_mla = mla_heads * mla_v
                mixes = [(mla_o, bf(w_out_even[la][:n_mla])), (dsa_o, bf(w_out_even[la][n_mla:]))]
                rows_mla.append(mrow.reshape(b, t, -1))
                rows_dsa.append(dsarow.reshape(b, t, 2, dsa_kv_heads, dsa_hd))
                rows_idx.append(ik.reshape(b, t, -1))
            else:
                lr = l // 2
                z = _normmod_mm(x2d, sc1, sh1, nmg, bf(w_in_odd[lr]), tq, tm, _pick(w_in_odd.shape[2], (2048,)),
                                per_row)
                lb = lb_all[l].reshape(hg_heads, 1, hg_k)
                og = row1(hgrn_out_norm_g[lr])
                if sample:
                    o_g, st = _hgrn_step(z, lb, og, state_hgrn[lr].astype(F32), hg_heads, hg_q_scale)
                else:
                    o_g, st = _hgrn_prompt(z, lb, og, b, t, hg_heads, _pick(t, (128,)), hg_q_scale)
                mixes = [(o_g, bf(w_out_odd[lr]))]
                states.append(st)
            dff = w_ffn_gate.shape[2]
            x2d = _mix_ffn(x2d, mixes, g1, nfg, sc2, sh2, g2, bf(w_ffn_gate[l]), bf(w_ffn_up[l]),
                           bf(w_ffn_down[l]), row1(final_norm_g), tq, tm, _pick(dff, (1408, 1024, 512, 256)),
                           per_row, final)
        return (x2d.reshape(b, t, d), jnp.stack(rows_mla), jnp.stack(rows_dsa), jnp.stack(rows_idx),
                jnp.stack(states))

    y_p, mla_p, dsa_p, idx_p, hg_p = run(x_prompt, False)
    y_s, mla_s, dsa_s, idx_s, hg_s = run(x_sample, True)
    return (y_p, y_s, mla_p, mla_s, dsa_p, dsa_s, idx_p, idx_s, hg_p, hg_s)
```

```python
import functools

import jax
import jax.numpy as jnp
import numpy as np
from jax import lax
from jax.experimental import pallas as pl
from jax.experimental.pallas import tpu as pltpu

F32 = jnp.float32
BF16 = jnp.bfloat16
I32 = jnp.int32

EPS = 1e-6
ROPE_THETA = 10000.0
TOPK_MAX = 256
NEG = -1e30
INT_MIN = -(2 ** 31)
LANES = 128
DSA_KEY_CHUNK = 512
MIB = 1024 * 1024

NT = (((1,), (1,)), ((), ()))
TN = (((0,), (0,)), ((), ()))


def _cp(sem, vmem_mib=48):
    return pltpu.CompilerParams(dimension_semantics=sem, vmem_limit_bytes=vmem_mib * MIB)


def _sigmoid(x):
    return 1.0 / (1.0 + jnp.exp(-x))


def _silu(x):
    return x * _sigmoid(x)


def _rms(x, g):
    return x * lax.rsqrt(jnp.mean(x * x, axis=-1, keepdims=True) + EPS) * g


def _dot(a, b):
    return jnp.dot(a, b, preferred_element_type=F32)


def _rope_cols(x, cos, sin_signed, half):
    outs = []
    lane = lax.broadcasted_iota(I32, (x.shape[0], LANES), 1)
    first = (lane % (2 * half)) < half
    for c in range(x.shape[1] // LANES):
        xc = x[:, c * LANES:(c + 1) * LANES]
        rot = jnp.where(first, pltpu.roll(xc, LANES - half, 1), pltpu.roll(xc, half, 1))
        outs.append(xc * cos + rot * sin_signed)
    return outs[0] if len(outs) == 1 else jnp.concatenate(outs, axis=1)


def _ada_kernel(c_ref, w_ref, b_ref, o_ref):
    a = _silu(c_ref[...]).astype(BF16)
    o_ref[...] = _dot(a, w_ref[...].astype(BF16)) + b_ref[...]


def _adaln(c_all, w_ada, b_ada):
    depth, d, n = w_ada.shape
    bc = c_all.shape[0]
    tn = n // 4
    return pl.pallas_call(
        _ada_kernel,
        grid=(depth, n // tn),
        in_specs=[pl.BlockSpec((bc, d), lambda l, j: (0, 0)),
                  pl.BlockSpec((None, d, tn), lambda l, j: (l, 0, j)),
                  pl.BlockSpec((None, 1, tn), lambda l, j: (l, 0, j))],
        out_specs=pl.BlockSpec((None, bc, tn), lambda l, j: (l, 0, j)),
        out_shape=jax.ShapeDtypeStruct((depth, bc, n), F32),
        compiler_params=_cp(("parallel", "parallel")),
        name="adaln",
    )(c_all, w_ada, b_ada.reshape(depth, 1, n))


def _absorb_kernel(a_ref, b_ref, o_ref):
    o_ref[...] = lax.dot_general(a_ref[...].astype(BF16), b_ref[...].astype(BF16), NT,
                                 preferred_element_type=F32)


def _absorb(w_q_nope, w_uk):
    a = jnp.transpose(w_q_nope, (1, 0, 2))
    b = jnp.transpose(w_uk, (1, 0, 2))
    h, rq, dn = a.shape
    rkv = b.shape[1]
    out = pl.pallas_call(
        _absorb_kernel,
        grid=(h,),
        in_specs=[pl.BlockSpec((None, rq, dn), lambda i: (i, 0, 0)),
                  pl.BlockSpec((None, rkv, dn), lambda i: (i, 0, 0))],
        out_specs=pl.BlockSpec((None, rq, rkv), lambda i: (i, 0, 0)),
        out_shape=jax.ShapeDtypeStruct((h, rq, rkv), F32),
        compiler_params=_cp(("parallel",)),
        name="mla_absorb",
    )(a, b)
    return jnp.transpose(out, (1, 0, 2)).reshape(rq, h * rkv)


class EvenDims:
    def __init__(self, q_lora, kv_lora, mla_rope, mla_heads, dsa_heads, dsa_kv_heads, dsa_hd, idx_heads, idx_dim):
        assert q_lora % LANES == 0 and kv_lora == LANES and mla_rope <= LANES
        assert dsa_hd == 64 and idx_dim == 64 and dsa_kv_heads == 2 and idx_heads <= 8
        self.q_lora, self.kv_lora, self.mla_rope, self.mla_heads = q_lora, kv_lora, mla_rope, mla_heads
        self.dsa_heads, self.dsa_kv_heads, self.dsa_hd = dsa_heads, dsa_kv_heads, dsa_hd
        self.idx_heads, self.idx_dim = idx_heads, idx_dim
        self.o_zq = 0
        self.o_ckv = q_lora
        self.o_kr = self.o_ckv + kv_lora
        self.o_dq = self.o_kr + LANES
        self.o_dk = self.o_dq + dsa_heads * dsa_hd
        self.o_dv = self.o_dk + dsa_kv_heads * dsa_hd
        self.o_iq = self.o_dv + dsa_kv_heads * dsa_hd
        self.o_ik = self.o_iq + idx_heads * idx_dim
        self.n = self.o_ik + LANES
        self.mla_row = kv_lora + mla_rope


def _even_kernel(x_ref, sc_ref, sh_ref, g_ref, w_ref, qg_ref, kvg_ref, wqr_ref, wabs_ref,
                 c64_ref, s64_ref, c32_ref, s32_ref,
                 qcat_ref, row_ref, kv16_ref, dsarow_ref, dq_ref, dk_ref, dvt_ref, iq_ref, ik_ref, ik16_ref,
                 iwt_ref, *, dm, mla_scale, dsa_scale, idx_scale, idx_w_scale):
    h = _rms(x_ref[...], g_ref[...]) * (1.0 + sc_ref[0]) + sh_ref[0]
    z = _dot(h.astype(BF16), w_ref[...])
    c64, s64, c32, s32 = c64_ref[...], s64_ref[...], c32_ref[...], s32_ref[...]
    tm = z.shape[0]

    cq = _rms(z[:, dm.o_zq:dm.o_zq + dm.q_lora], qg_ref[...]).astype(BF16)
    qlat = _dot(cq, wabs_ref[...]) * mla_scale
    qrope = _rope_cols(_dot(cq, wqr_ref[...]), c32, s32, dm.mla_rope // 2) * mla_scale
    for hd in range(dm.mla_heads):
        qcat_ref[0, hd, :, 0:dm.kv_lora] = qlat[:, hd * dm.kv_lora:(hd + 1) * dm.kv_lora].astype(BF16)
        qcat_ref[0, hd, :, dm.kv_lora:dm.mla_row] = qrope[:, hd * dm.mla_rope:(hd + 1) * dm.mla_rope].astype(BF16)

    ckv = _rms(z[:, dm.o_ckv:dm.o_ckv + dm.kv_lora], kvg_ref[...])
    kr = _rope_cols(z[:, dm.o_kr:dm.o_kr + LANES], c32, s32, dm.mla_rope // 2)[:, 0:dm.mla_rope]
    row_ref[:, 0:dm.kv_lora] = ckv
    row_ref[:, dm.kv_lora:dm.mla_row] = kr
    kv16_ref[:, 0:dm.kv_lora] = ckv.astype(BF16)
    kv16_ref[:, dm.kv_lora:dm.mla_row] = kr.astype(BF16)

    hd_ = dm.dsa_hd
    dq = _rope_cols(z[:, dm.o_dq:dm.o_dk], c64, s64, hd_ // 2) * dsa_scale
    for hh in range(dm.dsa_heads):
        dq_ref[0, hh] = dq[:, hh * hd_:(hh + 1) * hd_].astype(BF16)
    dk = _rope_cols(z[:, dm.o_dk:dm.o_dv], c64, s64, hd_ // 2)
    dv = z[:, dm.o_dv:dm.o_iq]
    dsarow_ref[:, 0:LANES] = dk
    dsarow_ref[:, LANES:2 * LANES] = dv
    for gg in range(dm.dsa_kv_heads):
        dk_ref[0, gg] = dk[:, gg * hd_:(gg + 1) * hd_].astype(BF16)
    vck = dvt_ref.shape[3]
    for c in range(tm // vck):
        dvt_ref[0, c] = dv[c * vck:(c + 1) * vck, :].T.astype(BF16)

    iq = _rope_cols(z[:, dm.o_iq:dm.o_ik], c64, s64, dm.idx_dim // 2) * idx_scale
    for hh in range(dm.idx_heads):
        iq_ref[0, hh] = iq[:, hh * dm.idx_dim:(hh + 1) * dm.idx_dim].astype(BF16)
    last = z[:, dm.o_ik:dm.o_ik + LANES]
    ik = _rope_cols(last, c64, s64, dm.idx_dim // 2)[:, 0:dm.idx_dim]
    ik_ref[...] = ik
    ik16_ref[...] = ik.astype(BF16)
    iwt_ref[0] = last.T[dm.idx_dim:dm.idx_dim + 8, :] * idx_w_scale


def _even_project(x2d, sc, sh, g, w_in, qg, kvg, wqr, wabs, tabs, dm, bq, tq, tm, per_row, scales):
    m, d = x2d.shape
    nt = tq // tm
    vck = min(DSA_KEY_CHUNK, tm)
    nh, kvh, ih = dm.mla_heads, dm.dsa_kv_heads, dm.idx_heads
    row2 = lambda i: (i, 0)
    hm4 = lambda i: (i // nt, 0, i % nt, 0)
    if per_row:
        mod_spec = pl.BlockSpec((1, tm, d), lambda i: (0, i, 0))
    else:
        mod_spec = pl.BlockSpec((1, 1, d), lambda i: (i // nt, 0, 0))
    const = lambda shape: pl.BlockSpec(shape, lambda i: tuple(0 for _ in shape))
    tab_spec = pl.BlockSpec((tm, LANES), lambda i: (i % nt, 0))
    out_shapes = (
        jax.ShapeDtypeStruct((bq, nh, tq, dm.mla_row), BF16),
        jax.ShapeDtypeStruct((m, dm.mla_row), F32),
        jax.ShapeDtypeStruct((m, dm.mla_row), BF16),
        jax.ShapeDtypeStruct((m, 2 * LANES), F32),
        jax.ShapeDtypeStruct((bq, dm.dsa_heads, tq, dm.dsa_hd), BF16),
        jax.ShapeDtypeStruct((bq, kvh, tq, dm.dsa_hd), BF16),
        jax.ShapeDtypeStruct((bq, tq // vck, LANES, vck), BF16),
        jax.ShapeDtypeStruct((bq, ih, tq, dm.idx_dim), BF16),
        jax.ShapeDtypeStruct((m, dm.idx_dim), F32),
        jax.ShapeDtypeStruct((m, dm.idx_dim), BF16),
        jax.ShapeDtypeStruct((bq, 8, tq), F32),
    )
    out_specs = (
        pl.BlockSpec((1, nh, tm, dm.mla_row), hm4),
        pl.BlockSpec((tm, dm.mla_row), row2),
        pl.BlockSpec((tm, dm.mla_row), row2),
        pl.BlockSpec((tm, 2 * LANES), row2),
        pl.BlockSpec((1, dm.dsa_heads, tm, dm.dsa_hd), hm4),
        pl.BlockSpec((1, kvh, tm, dm.dsa_hd), hm4),
        pl.BlockSpec((1, tm // vck, LANES, vck), lambda i: (i // nt, i % nt, 0, 0)),
        pl.BlockSpec((1, ih, tm, dm.idx_dim), hm4),
        pl.BlockSpec((tm, dm.idx_dim), row2),
        pl.BlockSpec((tm, dm.idx_dim), row2),
        pl.BlockSpec((1, 8, tm), lambda i: (i // nt, 0, i % nt)),
    )
    kern = functools.partial(_even_kernel, dm=dm, mla_scale=scales[0], dsa_scale=scales[1],
                             idx_scale=scales[2], idx_w_scale=scales[3])
    return pl.pallas_call(
        kern,
        grid=(m // tm,),
        in_specs=[pl.BlockSpec((tm, d), row2), mod_spec, mod_spec, const((1, d)), const(w_in.shape),
                  const(qg.shape), const(kvg.shape), const(wqr.shape), const(wabs.shape),
                  tab_spec, tab_spec, tab_spec, tab_spec],
        out_specs=out_specs,
        out_shape=out_shapes,
        compiler_params=_cp(("parallel",), 56),
        name="even_project",
    )(x2d, sc, sh, g, w_in, qg, kvg, wqr, wabs, *tabs)


def _fold_lanes(x, op):
    out = x[:, 0:LANES]
    for c in range(1, x.shape[1] // LANES):
        out = op(out, x[:, c * LANES:(c + 1) * LANES])
    return out


def _mla_kernel(q_ref, kv_ref, wuv_ref, o_ref, s_ref, mp_ref, lp_ref, acc_ref, *, tq, tk, nh, r, vd):
    qi = pl.program_id(1)
    n = nh * tq
    q = q_ref[0].reshape(n, q_ref.shape[-1])
    nch = (qi * tq + tq + tk - 1) // tk
    reps = tk // LANES

    def keys(j):
        return kv_ref[0, pl.ds(pl.multiple_of(j * tk, tk), tk), :]

    mp_ref[...] = jnp.full(mp_ref.shape, NEG, F32)

    def pass_a(j, c):
        s = lax.dot_general(q, keys(j), NT, preferred_element_type=F32)
        s_ref[j] = s
        mp_ref[...] = jnp.maximum(mp_ref[...], _fold_lanes(s, jnp.maximum))
        return c

    lax.fori_loop(0, nch - 1, pass_a, 0)
    jl = nch - 1
    s = lax.dot_general(q, keys(jl), NT, preferred_element_type=F32)
    qpos = qi * tq + lax.broadcasted_iota(I32, s.shape, 0) % tq
    kpos = jl * tk + lax.broadcasted_iota(I32, s.shape, 1)
    s = jnp.where(kpos <= qpos, s, NEG)
    s_ref[jl] = s
    mp = jnp.maximum(mp_ref[...], _fold_lanes(s, jnp.maximum))
    mb = jnp.broadcast_to(jnp.max(mp, axis=1, keepdims=True), (n, LANES))
    mbt = jnp.concatenate([mb] * reps, axis=1)

    lp_ref[...] = jnp.zeros(lp_ref.shape, F32)
    acc_ref[...] = jnp.zeros(acc_ref.shape, F32)

    def pass_b(j, c):
        p = jnp.exp2(s_ref[j] - mbt)
        lp_ref[...] += _fold_lanes(p, jnp.add)
        acc_ref[...] += _dot(p.astype(BF16), keys(j)[:, 0:r])
        return c

    lax.fori_loop(0, nch, pass_b, 0)
    lat = (acc_ref[...] / jnp.sum(lp_ref[...], axis=1, keepdims=True)).astype(BF16)
    for hd in range(nh):
        o_ref[0, :, hd * vd:(hd + 1) * vd] = _dot(lat[hd * tq:(hd + 1) * tq], wuv_ref[hd]).astype(BF16)


def _mla_prompt(qcat, kv16, wuv, tq):
    b, nh, t, dq = qcat.shape
    r, vd = wuv.shape[1], wuv.shape[2]
    tk = _pick(t, (512, 256, 128))
    n = nh * tq
    kern = functools.partial(_mla_kernel, tq=tq, tk=tk, nh=nh, r=r, vd=vd)
    return pl.pallas_call(
        kern,
        grid=(b, t // tq),
        in_specs=[pl.BlockSpec((1, nh, tq, dq), lambda bi, qi: (bi, 0, qi, 0)),
                  pl.BlockSpec((1, t, dq), lambda bi, qi: (bi, 0, 0)),
                  pl.BlockSpec(wuv.shape, lambda bi, qi: (0, 0, 0))],
        out_specs=pl.BlockSpec((1, tq, nh * vd), lambda bi, qi: (bi, qi, 0)),
        out_shape=jax.ShapeDtypeStruct((b, t, nh * vd), BF16),
        scratch_shapes=[pltpu.VMEM((t // tk, n, tk), F32), pltpu.VMEM((n, LANES), F32),
                        pltpu.VMEM((n, LANES), F32), pltpu.VMEM((n, r), F32)],
        compiler_params=_cp(("parallel", "arbitrary")),
        name="mla_prompt",
    )(qcat, kv16, wuv)


def _sort_key(score):
    b = pltpu.bitcast(score, I32)
    return jnp.where(b < 0, (b ^ jnp.int32(0x7FFFFFFF)) + 1, b)


def _topk_select(read_keys, read_pos, nch, rows, width, nsel, pos_bits):
    def count(pred):
        def body(j, c):
            return c + jnp.sum(pred(j).astype(I32).reshape(rows // 8, 8, width), axis=0)
        c = lax.fori_loop(0, nch, body, jnp.zeros((8, width), I32))
        return jnp.sum(c, axis=0, keepdims=True)

    def bit_body(b, thr):
        cand = thr + lax.shift_left(jnp.int32(1), 31 - b)
        cnt = count(lambda j: read_keys(j) >= cand)
        return jnp.where(cnt >= nsel, cand, thr)

    thr = lax.fori_loop(0, 32, bit_body, jnp.full((1, width), INT_MIN, I32))

    def tie_search():
        need = nsel - count(lambda j: read_keys(j) > thr)

        def jbit_body(b, jc):
            cand = jc + lax.shift_left(jnp.int32(1), pos_bits - 1 - b)
            cnt = count(lambda j: (read_keys(j) == thr) & (read_pos(j) < cand))
            return jnp.where(cnt <= need, cand, jc)

        return lax.fori_loop(0, pos_bits, jbit_body, jnp.zeros((1, width), I32))

    excess = (count(lambda j: read_keys(j) >= thr) > nsel) & (thr > INT_MIN)
    any_excess = jnp.max(excess.astype(I32)) > 0
    jcut = lax.cond(any_excess, tie_search, lambda: jnp.full((1, width), 2 ** pos_bits, I32))
    return thr, jcut


def _selected(keys, pos, thr, jcut):
    return ((keys > thr) | ((keys == thr) & (pos < jcut))) & (keys != INT_MIN)


def _dsa_kernel(iq_ref, iwt_ref, ik_ref, dq_ref, dk_ref, dvt_ref, o_ref, keys_ref, bias_ref, s_ref,
                *, tq, ck, nsel, ih, gsz, kvh, hd, pos_bits):
    i = pl.program_id(1)
    nch = ((i + 1) * tq + ck - 1) // ck
    row = lax.broadcasted_iota(I32, (ck, tq), 0)
    qpos = i * tq + lax.broadcasted_iota(I32, (ck, tq), 1)

    def off(j):
        return pl.multiple_of(j * ck, ck)

    def score_chunk(j, carry):
        ikc = ik_ref[0, pl.ds(off(j), ck), :]
        acc = jnp.zeros((ck, tq), F32)
        for hh in range(ih):
            d = lax.dot_general(ikc, iq_ref[0, hh], NT, preferred_element_type=F32)
            acc = acc + iwt_ref[0, hh:hh + 1, :] * jnp.maximum(d, 0.0)
        valid = (j * ck + row) <= qpos
        keys_ref[pl.ds(off(j), ck), :] = jnp.where(valid, _sort_key(acc), INT_MIN)
        return carry

    lax.fori_loop(0, nch, score_chunk, 0)

    read_keys = lambda j: keys_ref[pl.ds(off(j), ck), :]
    read_pos = lambda j: j * ck + row
    thr, jcut = _topk_select(read_keys, read_pos, nch, ck, tq, nsel, pos_bits)

    def bias_chunk(j, carry):
        sel = _selected(read_keys(j), read_pos(j), thr, jcut)
        bias_ref[pl.ds(off(j), ck), :] = jnp.where(sel, 0.0, NEG)
        return carry

    lax.fori_loop(0, nch, bias_chunk, 0)

    qs = [dq_ref[0, g * gsz:(g + 1) * gsz].reshape(gsz * tq, hd) for g in range(kvh)]
    wq = gsz * tq

    def pass_a(j, ms):
        b = bias_ref[pl.ds(off(j), ck), :]
        bt = jnp.concatenate([b] * gsz, axis=1)
        out = []
        for g in range(kvh):
            s = lax.dot_general(dk_ref[0, g, pl.ds(off(j), ck), :], qs[g], NT, preferred_element_type=F32) + bt
            s_ref[g, pl.ds(off(j), ck), :] = s
            out.append(jnp.maximum(ms[g], jnp.max(s, axis=0, keepdims=True)))
        return tuple(out)

    ms = lax.fori_loop(0, nch, pass_a, tuple(jnp.full((1, wq), NEG, F32) for _ in range(kvh)))

    def pass_b(j, carry):
        out = []
        for g in range(kvh):
            l, acc = carry[g]
            p = jnp.exp2(s_ref[g, pl.ds(off(j), ck), :] - ms[g])
            l = l + jnp.sum(p, axis=0, keepdims=True)
            acc = acc + _dot(dvt_ref[0, j, g * hd:(g + 1) * hd, :], p.astype(BF16))
            out.append((l, acc))
        return tuple(out)

    res = lax.fori_loop(0, nch, pass_b,
                        tuple((jnp.zeros((1, wq), F32), jnp.zeros((hd, wq), F32)) for _ in range(kvh)))
    outs = []
    for g in range(kvh):
        ot = res[g][1] / res[g][0]
        for hh in range(gsz):
            outs.append(ot[:, hh * tq:(hh + 1) * tq])
    o_ref[0] = jnp.concatenate(outs, axis=0).T.astype(BF16)


def _dsa_prompt(iq, iwt, ik16, dq, dk, dvt, nsel, tq):
    b, ih, t, idim = iq.shape
    nhd, hd = dq.shape[1], dq.shape[3]
    kvh = dk.shape[1]
    gsz = nhd // kvh
    ck = dvt.shape[3]
    ik3 = ik16.reshape(b, t, idim)
    pos_bits = int(t).bit_length()
    kern = functools.partial(_dsa_kernel, tq=tq, ck=ck, nsel=nsel, ih=ih, gsz=gsz, kvh=kvh, hd=hd,
                             pos_bits=pos_bits)
    return pl.pallas_call(
        kern,
        grid=(b, t // tq),
        in_specs=[pl.BlockSpec((1, ih, tq, idim), lambda bi, i: (bi, 0, i, 0)),
                  pl.BlockSpec((1, 8, tq), lambda bi, i: (bi, 0, i)),
                  pl.BlockSpec((1, t, idim), lambda bi, i: (bi, 0, 0)),
                  pl.BlockSpec((1, nhd, tq, hd), lambda bi, i: (bi, 0, i, 0)),
                  pl.BlockSpec((1, kvh, t, hd), lambda bi, i: (bi, 0, 0, 0)),
                  pl.BlockSpec((1, t // ck, 2 * hd, ck), lambda bi, i: (bi, 0, 0, 0))],
        out_specs=pl.BlockSpec((1, tq, nhd * hd), lambda bi, i: (bi, i, 0)),
        out_shape=jax.ShapeDtypeStruct((b, t, nhd * hd), BF16),
        scratch_shapes=[pltpu.VMEM((t, tq), I32), pltpu.VMEM((t, tq), F32),
                        pltpu.VMEM((kvh, t, gsz * tq), F32)],
        compiler_params=_cp(("parallel", "arbitrary")),
        name="dsa_prompt",
    )(iq, iwt, ik3, dq, dk, dvt)


def _ffn_kernel(*refs, n_mix, final):
    x_ref = refs[0]
    mix_refs = refs[1:1 + 2 * n_mix]
    (g1_ref, nfg_ref, sc_ref, sh_ref, g2_ref, wg_ref, wu_ref, wd_ref, fg_ref,
     o_ref, x1_ref, hb_ref, acc_ref) = refs[1 + 2 * n_mix:]
    f = pl.program_id(1)

    @pl.when(f == 0)
    def _():
        mix = _dot(mix_refs[0][...], mix_refs[1][...])
        for k in range(1, n_mix):
            mix = mix + _dot(mix_refs[2 * k][...], mix_refs[2 * k + 1][...])
        x1 = x_ref[...] + g1_ref[0] * mix
        x1_ref[...] = x1
        hb_ref[...] = (_rms(x1, nfg_ref[...]) * (1.0 + sc_ref[0]) + sh_ref[0]).astype(BF16)
        acc_ref[...] = jnp.zeros(acc_ref.shape, F32)

    hb = hb_ref[...]
    act = _silu(_dot(hb, wg_ref[...])) * _dot(hb, wu_ref[...])
    acc_ref[...] += _dot(act.astype(BF16), wd_ref[...])

    @pl.when(f == pl.num_programs(1) - 1)
    def _():
        y = x1_ref[...] + g2_ref[0] * acc_ref[...]
        if final:
            y = _rms(y, fg_ref[...])
        o_ref[...] = y


def _mix_ffn(x2d, mixes, g1, nfg, sc2, sh2, g2, wg, wu, wd, fg, tq, tm, tf, per_row, final):
    m, d = x2d.shape
    dff = wg.shape[1]
    nt = tq // tm
    if per_row:
        mod_spec = pl.BlockSpec((1, tm, d), lambda i, f: (0, i, 0))
    else:
        mod_spec = pl.BlockSpec((1, 1, d), lambda i, f: (i // nt, 0, 0))
    row = lambda i, f: (i, 0)
    const2 = lambda shape: pl.BlockSpec(shape, lambda i, f: (0, 0))
    in_specs = [pl.BlockSpec((tm, d), row)]
    args = [x2d]
    for a, w in mixes:
        in_specs += [pl.BlockSpec((tm, a.shape[1]), row), const2(w.shape)]
        args += [a, w]
    in_specs += [mod_spec, const2((1, d)), mod_spec, mod_spec, mod_spec,
                 pl.BlockSpec((d, tf), lambda i, f: (0, f)), pl.BlockSpec((d, tf), lambda i, f: (0, f)),
                 pl.BlockSpec((tf, d), lambda i, f: (f, 0)), const2((1, d))]
    args += [g1, nfg, sc2, sh2, g2, wg, wu, wd, fg]
    kern = functools.partial(_ffn_kernel, n_mix=len(mixes), final=final)
    return pl.pallas_call(
        kern,
        grid=(m // tm, dff // tf),
        in_specs=in_specs,
        out_specs=pl.BlockSpec((tm, d), row),
        out_shape=jax.ShapeDtypeStruct((m, d), F32),
        scratch_shapes=[pltpu.VMEM((tm, d), F32), pltpu.VMEM((tm, d), BF16), pltpu.VMEM((tm, d), F32)],
        compiler_params=_cp(("parallel", "arbitrary"), 56),
        name="mix_ffn",
    )(*args)


def _normmod_mm_kernel(x_ref, sc_ref, sh_ref, g_ref, w_ref, o_ref):
    h = _rms(x_ref[...], g_ref[...]) * (1.0 + sc_ref[0]) + sh_ref[0]
    o_ref[...] = _dot(h.astype(BF16), w_ref[...])


def _normmod_mm(x2d, sc, sh, g, w, tq, tm, tn, per_row):
    m, d = x2d.shape
    n = w.shape[1]
    nt = tq // tm
    if per_row:
        mod_spec = pl.BlockSpec((1, tm, d), lambda i, j: (0, i, 0))
    else:
        mod_spec = pl.BlockSpec((1, 1, d), lambda i, j: (i // nt, 0, 0))
    return pl.pallas_call(
        _normmod_mm_kernel,
        grid=(m // tm, n // tn),
        in_specs=[pl.BlockSpec((tm, d), lambda i, j: (i, 0)), mod_spec, mod_spec,
                  pl.BlockSpec((1, d), lambda i, j: (0, 0)), pl.BlockSpec((d, tn), lambda i, j: (0, j))],
        out_specs=pl.BlockSpec((tm, tn), lambda i, j: (i, j)),
        out_shape=jax.ShapeDtypeStruct((m, n), F32),
        compiler_params=_cp(("parallel", "parallel")),
        name="odd_in_proj",
    )(x2d, sc, sh, g, w)


def _hgrn_consts(c):
    nlev = int(np.log2(c))
    t = np.arange(c)
    tril = (t[None, :] <= t[:, None]).astype(np.float32)
    mats = [tril]
    masks = [np.eye(c, dtype=np.float32)]
    for lev in range(1, nlev + 1):
        w = 2 ** (lev - 1)
        piv = (t // (2 * w)) * (2 * w) + w - 1
        mats.append(tril - tril[piv])
        same = (t[:, None] // (2 * w)) == (t[None, :] // (2 * w))
        right = ((t // w) % 2 == 1)[:, None]
        left = ((t // w) % 2 == 0)[None, :]
        masks.append((same & right & left).astype(np.float32))
    return np.concatenate(mats, axis=0), np.stack(masks), nlev


def _split3(x):
    a = x.astype(BF16)
    r = x - a.astype(F32)
    b = r.astype(BF16)
    c = (r - b.astype(F32)).astype(BF16)
    return a, b, c


def _hgrn_kernel(zq_ref, zf_ref, zi_ref, zg_ref, lb_ref, og_ref, cm_ref, bm_ref, o_ref, st_ref, s_ref,
                 *, c, nlev, q_scale, hpb, kk, vd):
    ci = pl.program_id(2)

    @pl.when(ci == 0)
    def _():
        s_ref[...] = jnp.zeros(s_ref.shape, F32)

    for hh in range(hpb):
        ks = slice(hh * kk, (hh + 1) * kk)
        vs = slice(hh * vd, (hh + 1) * vd)
        lb = lb_ref[hh]
        zf = zf_ref[:, ks]
        e = jnp.exp(-jnp.abs(zf))
        inv = 1.0 / (1.0 + e)
        sig_pos = jnp.where(zf >= 0, inv, e * inv)
        sig_neg = jnp.where(zf >= 0, e * inv, inv)
        logf = jnp.log(lb + (1.0 - lb) * sig_pos)
        k = (1.0 - lb) * sig_neg
        q = _silu(zq_ref[:, ks]) * q_scale
        v = zi_ref[:, vs].astype(BF16)

        a3 = jnp.concatenate(_split3(logf), axis=1)
        gd = _dot(cm_ref[...], a3)
        gd = gd[:, 0:kk] + gd[:, kk:2 * kk] + gd[:, 2 * kk:3 * kk]
        gcum = gd[0:c]
        glast = gcum[c - 1:c]

        amat = bm_ref[0] * lax.dot_general(q.astype(BF16), k.astype(BF16), NT, preferred_element_type=F32)
        for lev in range(1, nlev + 1):
            ed = jnp.exp(-jnp.abs(gd[lev * c:(lev + 1) * c]))
            amat = amat + bm_ref[lev] * lax.dot_general((q * ed).astype(BF16), (k * ed).astype(BF16), NT,
                                                        preferred_element_type=F32)
        s_prev = s_ref[hh]
        o = _dot(amat.astype(BF16), v) + _dot((q * jnp.exp(gcum)).astype(BF16), s_prev.astype(BF16))
        kdt = (k * jnp.exp(glast - gcum)).T.astype(BF16)
        decay_col = jnp.broadcast_to(jnp.exp(glast), (kk, kk)).T
        s_new = decay_col * s_prev + _dot(kdt, v)
        s_ref[hh] = s_new
        o_ref[:, vs] = (_rms(o, og_ref[...]) * _silu(zg_ref[:, vs])).astype(BF16)

    @pl.when(ci == pl.num_programs(2) - 1)
    def _():
        st_ref[...] = s_ref[...]


def _hgrn_prompt(z, lb, og, b, t, nh, c, q_scale):
    m = z.shape[0]
    kk = lb.shape[-1]
    vd = og.shape[-1]
    nc = t // c
    hpb = 2 if nh % 2 == 0 else 1
    ng = nh // hpb
    cm, bm, nlev = _hgrn_consts(c)
    cm = jnp.asarray(cm, BF16)
    bm = jnp.asarray(bm, F32)
    seg = lambda s, w: pl.BlockSpec((c, hpb * w), lambda bi, h, ci: (bi * nc + ci, s * ng + h))
    kern = functools.partial(_hgrn_kernel, c=c, nlev=nlev, q_scale=q_scale, hpb=hpb, kk=kk, vd=vd)
    return pl.pallas_call(
        kern,
        grid=(b, ng, nc),
        in_specs=[seg(0, kk), seg(1, kk), seg(2, vd), seg(3, vd),
                  pl.BlockSpec((hpb, 1, kk), lambda bi, h, ci: (h, 0, 0)),
                  pl.BlockSpec((1, vd), lambda bi, h, ci: (0, 0)),
                  pl.BlockSpec(cm.shape, lambda bi, h, ci: (0, 0)),
                  pl.BlockSpec(bm.shape, lambda bi, h, ci: (0, 0, 0))],
        out_specs=(pl.BlockSpec((c, hpb * vd), lambda bi, h, ci: (bi * nc + ci, h)),
                   pl.BlockSpec((None, hpb, kk, vd), lambda bi, h, ci: (bi, h, 0, 0))),
        out_shape=(jax.ShapeDtypeStruct((m, nh * vd), BF16),
                   jax.ShapeDtypeStruct((b, nh, kk, vd), F32)),
        scratch_shapes=[pltpu.VMEM((hpb, kk, vd), F32)],
        compiler_params=_cp(("parallel", "parallel", "arbitrary")),
        name="hgrn_prompt",
    )(z, z, z, z, lb, og, cm, bm)


def _hgrn_step_kernel(z_ref, lb_ref, og_ref, s_ref, o_ref, st_ref, *, nh, kk, vd, bt, q_scale):
    z = z_ref[...]
    hf = nh * kk
    for h in range(nh):
        lb = lb_ref[h]
        zf = z[:, hf + h * kk:hf + (h + 1) * kk]
        e = jnp.exp(-jnp.abs(zf))
        inv = 1.0 / (1.0 + e)
        f = lb + (1.0 - lb) * jnp.where(zf >= 0, inv, e * inv)
        k = (1.0 - lb) * jnp.where(zf >= 0, e * inv, inv)
        q = _silu(z[:, h * kk:(h + 1) * kk]) * q_scale
        v = z[:, 2 * hf + h * vd:2 * hf + (h + 1) * vd]
        gate = _silu(z[:, 2 * hf + nh * vd + h * vd:2 * hf + nh * vd + (h + 1) * vd])
        ft, kt, qt = f.T, k.T, q.T
        for j in range(bt):
            s_new = ft[:, j:j + 1] * s_ref[j, h] + kt[:, j:j + 1] * v[j:j + 1, :]
            st_ref[j, h] = s_new
            o = jnp.sum(qt[:, j:j + 1] * s_new, axis=0, keepdims=True)
            o_ref[j:j + 1, h * vd:(h + 1) * vd] = (_rms(o, og_ref[...]) * gate[j:j + 1, :]).astype(BF16)


def _hgrn_step(z, lb, og, state, nh, q_scale):
    bsz = z.shape[0]
    kk, vd = state.shape[2], state.shape[3]
    bt = 8
    kern = functools.partial(_hgrn_step_kernel, nh=nh, kk=kk, vd=vd, bt=bt, q_scale=q_scale)
    return pl.pallas_call(
        kern,
        grid=(bsz // bt,),
        in_specs=[pl.BlockSpec((bt, z.shape[1]), lambda i: (i, 0)),
                  pl.BlockSpec(lb.shape, lambda i: (0, 0, 0)),
                  pl.BlockSpec((1, vd), lambda i: (0, 0)),
                  pl.BlockSpec((bt, nh, kk, vd), lambda i: (i, 0, 0, 0))],
        out_specs=(pl.BlockSpec((bt, nh * vd), lambda i: (i, 0)),
                   pl.BlockSpec((bt, nh, kk, vd), lambda i: (i, 0, 0, 0))),
        out_shape=(jax.ShapeDtypeStruct((bsz, nh * vd), BF16),
                   jax.ShapeDtypeStruct(state.shape, F32)),
        compiler_params=_cp(("parallel",)),
        name="hgrn_step",
    )(z, lb, og, state)


def _mla_dec_kernel(pt_ref, q_ref, new_ref, wuv_ref, *rest, pp, r, vd, nh):
    pages = rest[:pp]
    o_ref, m_ref, l_ref, acc_ref = rest[pp:]
    g = pl.program_id(1)
    q = q_ref[...]

    @pl.when(g == 0)
    def _():
        new = new_ref[...]
        s = lax.dot_general(q, new, NT, preferred_element_type=F32)
        s = jnp.where(lax.broadcasted_iota(I32, s.shape, 1) == 0, s, NEG)
        m = jnp.max(s, axis=-1, keepdims=True)
        p = jnp.exp2(s - m)
        m_ref[...] = m
        l_ref[...] = jnp.sum(p, axis=-1, keepdims=True)
        acc_ref[...] = _dot(p.astype(BF16), new[:, 0:r])

    kvs = [pg[...].astype(BF16) for pg in pages]
    s = jnp.concatenate([_dot(q, kv) for kv in kvs], axis=1)
    m_prev = m_ref[...]
    m_new = jnp.maximum(m_prev, jnp.max(s, axis=-1, keepdims=True))
    alpha = jnp.exp2(m_prev - m_new)
    p = jnp.exp2(s - m_new).astype(BF16)
    l_ref[...] = alpha * l_ref[...] + jnp.sum(p.astype(F32), axis=-1, keepdims=True)
    acc = alpha * acc_ref[...]
    psz = kvs[0].shape[1]
    for i, kv in enumerate(kvs):
        acc = acc + lax.dot_general(p[:, i * psz:(i + 1) * psz], kv[0:r, :], NT, preferred_element_type=F32)
    acc_ref[...] = acc
    m_ref[...] = m_new

    @pl.when(g == pl.num_programs(1) - 1)
    def _():
        lat = (acc_ref[...] / l_ref[...]).astype(BF16)
        for hd in range(nh):
            o_ref[0:1, hd * vd:(hd + 1) * vd] = _dot(lat[hd:hd + 1], wuv_ref[hd])
        o_ref[1:8, :] = jnp.zeros((7, nh * vd), F32)


def _mla_decode(page_table, qdec, newrow, wuv, cache, pp):
    bsz, nh, dq = qdec.shape
    n_pages = page_table.shape[1]
    psz = cache.shape[2]
    r, vd = wuv.shape[1], wuv.shape[2]
    pt = page_table.reshape(-1)

    def page_map(k):
        return lambda b, g, ptr: (ptr[b * n_pages + g * pp + k], 0, 0)

    kern = functools.partial(_mla_dec_kernel, pp=pp, r=r, vd=vd, nh=nh)
    grid_spec = pltpu.PrefetchScalarGridSpec(
        num_scalar_prefetch=1,
        grid=(bsz, n_pages // pp),
        in_specs=[pl.BlockSpec((None, nh, dq), lambda b, g, ptr: (b, 0, 0)),
                  pl.BlockSpec((None, 8, dq), lambda b, g, ptr: (b, 0, 0)),
                  pl.BlockSpec(wuv.shape, lambda b, g, ptr: (0, 0, 0))]
                 + [pl.BlockSpec((None, dq, psz), page_map(k)) for k in range(pp)],
        out_specs=pl.BlockSpec((None, 8, nh * vd), lambda b, g, ptr: (b, 0, 0)),
        scratch_shapes=[pltpu.VMEM((nh, 1), F32), pltpu.VMEM((nh, 1), F32), pltpu.VMEM((nh, r), F32)],
    )
    out = pl.pallas_call(
        kern,
        grid_spec=grid_spec,
        out_shape=jax.ShapeDtypeStruct((bsz, 8, nh * vd), F32),
        compiler_params=_cp(("parallel", "arbitrary")),
        name="mla_decode",
    )(pt, qdec, newrow, wuv, *([cache] * pp))
    return out[:, 0, :]


def _idx_dec_kernel(pt_ref, iq_ref, iw_ref, iknew_ref, *rest, pp, n_pages, nsel, pos_bits):
    pages = rest[:pp]
    bias_ref, keys_ref = rest[pp:]
    g = pl.program_id(1)
    iq = iq_ref[...]
    iw = iw_ref[...]
    psz = pages[0].shape[1]

    def score(d):
        return jnp.sum(iw * jnp.maximum(d, 0.0), axis=0, keepdims=True)

    @pl.when(g == 0)
    def _():
        dnew = lax.dot_general(iq, iknew_ref[...], NT, preferred_element_type=F32)
        snew = _sort_key(score(dnew))
        slot = (lax.broadcasted_iota(I32, (8, psz), 0) == 0) & (lax.broadcasted_iota(I32, (8, psz), 1) == 0)
        keys_ref[n_pages:n_pages + 8, :] = jnp.where(slot, jnp.broadcast_to(snew, (8, psz)), INT_MIN)

    rows = jnp.concatenate([_sort_key(score(_dot(iq, pg[...].astype(BF16)))) for pg in pages], axis=0)
    keys_ref[pl.ds(pl.multiple_of(g * pp, pp), pp), :] = rows

    @pl.when(g == pl.num_programs(1) - 1)
    def _():
        rows = keys_ref.shape[0]
        pos = (lax.broadcasted_iota(I32, (rows, psz), 0) * psz + lax.broadcasted_iota(I32, (rows, psz), 1))
        keys = keys_ref[...]

        def count(pred):
            return jnp.sum(jnp.sum(pred.astype(I32), axis=0, keepdims=True), axis=1, keepdims=True)

        def bit_body(b, thr):
            cand = thr + lax.shift_left(jnp.int32(1), 31 - b)
            return jnp.where(count(keys >= cand) >= nsel, cand, thr)

        thr = lax.fori_loop(0, 32, bit_body, jnp.full((1, 1), INT_MIN, I32))
        need = nsel - count(keys > thr)

        def jbit_body(b, jc):
            cand = jc + lax.shift_left(jnp.int32(1), pos_bits - 1 - b)
            cnt = count((keys == thr) & (pos < cand))
            return jnp.where(cnt <= need, cand, jc)

        jcut = lax.fori_loop(0, pos_bits, jbit_body, jnp.zeros((1, 1), I32))
        bias_ref[...] = jnp.where(_selected(keys, pos, thr, jcut), 0.0, NEG)


def _idx_decode(page_table, iqd, iwd, iknew, cache, nsel, pp):
    assert pp % 8 == 0
    bsz = iqd.shape[0]
    idim = iqd.shape[2]
    n_pages = page_table.shape[1]
    psz = cache.shape[2]
    rows = n_pages + 8
    pos_bits = int(rows * psz).bit_length()
    pt = page_table.reshape(-1)

    def page_map(k):
        return lambda b, g, ptr: (ptr[b * n_pages + g * pp + k], 0, 0)

    kern = functools.partial(_idx_dec_kernel, pp=pp, n_pages=n_pages, nsel=nsel, pos_bits=pos_bits)
    grid_spec = pltpu.PrefetchScalarGridSpec(
        num_scalar_prefetch=1,
        grid=(bsz, n_pages // pp),
        in_specs=[pl.BlockSpec((None, 8, idim), lambda b, g, ptr: (b, 0, 0)),
                  pl.BlockSpec((None, 8, 1), lambda b, g, ptr: (b, 0, 0)),
                  pl.BlockSpec((None, psz, idim), lambda b, g, ptr: (b, 0, 0))]
                 + [pl.BlockSpec((None, idim, psz), page_map(k)) for k in range(pp)],
        out_specs=pl.BlockSpec((None, rows, psz), lambda b, g, ptr: (b, 0, 0)),
        scratch_shapes=[pltpu.VMEM((rows, psz), I32)],
    )
    return pl.pallas_call(
        kern,
        grid_spec=grid_spec,
        out_shape=jax.ShapeDtypeStruct((bsz, rows, psz), F32),
        compiler_params=_cp(("parallel", "arbitrary")),
        name="idx_decode",
    )(pt, iqd, iwd, iknew, *([cache] * pp))


def _dsa_dec_kernel(pt_ref, q_ref, new_ref, bias_ref, *rest, pp, n_pages, hd, gsz):
    pages = rest[:pp]
    o_ref, m_ref, l_ref, acc_ref = rest[pp:]
    g = pl.program_id(1)
    q = q_ref[...]
    kw = 2 * hd

    @pl.when(g == 0)
    def _():
        new = new_ref[...]
        s = lax.dot_general(q, new[:, 0:kw], NT, preferred_element_type=F32)
        s = s + bias_ref[n_pages:n_pages + 1, 0:8]
        m = jnp.max(s, axis=-1, keepdims=True)
        p = jnp.exp2(s - m)
        m_ref[...] = m
        l_ref[...] = jnp.sum(p, axis=-1, keepdims=True)
        acc_ref[...] = _dot(p.astype(BF16), new[:, kw:2 * kw])

    kvs = [pg[...].astype(BF16) for pg in pages]
    bias = bias_ref[pl.ds(pl.multiple_of(g * pp, pp), pp), :]
    s = jnp.concatenate(
        [_dot(q, kv[0:kw, :]) + bias[i:i + 1, :] for i, kv in enumerate(kvs)], axis=1)
    m_prev = m_ref[...]
    m_new = jnp.maximum(m_prev, jnp.max(s, axis=-1, keepdims=True))
    alpha = jnp.exp2(m_prev - m_new)
    p = jnp.exp2(s - m_new).astype(BF16)
    l_ref[...] = alpha * l_ref[...] + jnp.sum(p.astype(F32), axis=-1, keepdims=True)
    acc = alpha * acc_ref[...]
    psz = kvs[0].shape[1]
    for i, kv in enumerate(kvs):
        acc = acc + lax.dot_general(p[:, i * psz:(i + 1) * psz], kv[kw:2 * kw, :], NT,
                                    preferred_element_type=F32)
    acc_ref[...] = acc
    m_ref[...] = m_new

    @pl.when(g == pl.num_programs(1) - 1)
    def _():
        o = acc_ref[...] / l_ref[...]
        rowi = lax.broadcasted_iota(I32, o.shape, 0)
        o_ref[...] = jnp.where(rowi < gsz, o, pltpu.roll(o, hd, 1))


def _dsa_decode(page_table, qbd, newkv, bias, cache, pp, hd, gsz):
    bsz = qbd.shape[0]
    n_pages = page_table.shape[1]
    psz = cache.shape[2]
    pt = page_table.reshape(-1)

    def page_map(k):
        return lambda b, g, ptr: (ptr[b * n_pages + g * pp + k], 0, 0)

    kern = functools.partial(_dsa_dec_kernel, pp=pp, n_pages=n_pages, hd=hd, gsz=gsz)
    grid_spec = pltpu.PrefetchScalarGridSpec(
        num_scalar_prefetch=1,
        grid=(bsz, n_pages // pp),
        in_specs=[pl.BlockSpec((None, 8, 2 * hd), lambda b, g, ptr: (b, 0, 0)),
                  pl.BlockSpec((None, 8, 4 * hd), lambda b, g, ptr: (b, 0, 0)),
                  pl.BlockSpec((None, n_pages + 8, psz), lambda b, g, ptr: (b, 0, 0))]
                 + [pl.BlockSpec((None, 4 * hd, psz), page_map(k)) for k in range(pp)],
        out_specs=pl.BlockSpec((None, 8, 2 * hd), lambda b, g, ptr: (b, 0, 0)),
        scratch_shapes=[pltpu.VMEM((8, 1), F32), pltpu.VMEM((8, 1), F32), pltpu.VMEM((8, 2 * hd), F32)],
    )
    return pl.pallas_call(
        kern,
        grid_spec=grid_spec,
        out_shape=jax.ShapeDtypeStruct((bsz, 8, 2 * hd), F32),
        compiler_params=_cp(("parallel", "arbitrary")),
        name="dsa_decode",
    )(pt, qbd, newkv, bias, *([cache] * pp))


def _rope_tables(pos, half):
    inv_freq = ROPE_THETA ** (-jnp.arange(half, dtype=F32) / half)
    ang = pos.astype(F32)[:, None] * inv_freq[None, :]
    cos, sin = jnp.cos(ang), jnp.sin(ang)
    reps = LANES // (2 * half)
    return (jnp.tile(jnp.concatenate([cos, cos], axis=1), (1, reps)),
            jnp.tile(jnp.concatenate([-sin, sin], axis=1), (1, reps)))


def _pick(n, pref):
    for c in pref:
        if n % c == 0:
            return c
    return n


def kernel(x_prompt, x_sample, cache_mla, cache_dsa_kv, cache_idx, state_hgrn, page_table, c_prompt, c_sample,
           w_ada, b_ada, norm_mix_g, norm_ffn_g, w_in_even, mla_q_norm_g, mla_kv_norm_g, mla_w_q_nope,
           mla_w_q_rope, mla_w_uk, mla_w_uv, w_out_even, w_in_odd, hgrn_lower_bounds, hgrn_out_norm_g,
           w_out_odd, w_ffn_gate, w_ffn_up, w_ffn_down, final_norm_g):
    depth, d, _ = w_ada.shape
    bp, tp, _ = x_prompt.shape
    bs, ts, _ = x_sample.shape
    assert ts == 1
    q_lora, mla_heads, mla_nope = mla_w_q_nope.shape[1:]
    mla_rope = mla_w_q_rope.shape[3]
    kv_lora, _, mla_v = mla_w_uv.shape[1:]
    dsa_kv_heads, dsa_hd = cache_dsa_kv.shape[4:]
    idx_dim = cache_idx.shape[3]
    n_pages = page_table.shape[1]
    psz = cache_mla.shape[2]
    past_len = n_pages * psz
    sizes_tail = w_in_even.shape[2] - (q_lora + kv_lora + mla_rope + 2 * dsa_kv_heads * dsa_hd + idx_dim)
    dsa_heads = (w_out_even.shape[1] - mla_heads * mla_v) // dsa_hd
    idx_heads = (sizes_tail - dsa_heads * dsa_hd) // (idx_dim + 1)
    dm = EvenDims(q_lora, kv_lora, mla_rope, mla_heads, dsa_heads, dsa_kv_heads, dsa_hd, idx_heads, idx_dim)
    gsz = dsa_heads // dsa_kv_heads
    hg_f = hgrn_lower_bounds.shape[1]
    hg_heads = state_hgrn.shape[2]
    hg_k = state_hgrn.shape[3]
    hg_v = state_hgrn.shape[4]
    log2e = float(np.log2(np.e))
    scales = ((mla_nope + mla_rope) ** -0.5 * log2e, dsa_hd ** -0.5 * log2e, idx_dim ** -0.5, idx_heads ** -0.5)
    hg_q_scale = hg_k ** -0.5

    lb_all = jax.nn.softmax(hgrn_lower_bounds.astype(F32), axis=0)
    lb_all = jnp.cumsum(lb_all, axis=0) - lb_all[0]

    c_all = jnp.concatenate([c_prompt, c_sample], axis=0)
    mods = _adaln(c_all, w_ada, b_ada)

    def mods_for(l, sample):
        mm = mods[l, bp:] if sample else mods[l, :bp]
        parts = [mm[:, k * d:(k + 1) * d] for k in range(6)]
        return [p[None] if sample else p[:, None, :] for p in parts]

    bf = lambda a: a.astype(BF16)
    row1 = lambda a: a.reshape(1, -1).astype(F32)

    def even_weights(la):
        w = w_in_even[la]
        splits = np.cumsum([q_lora, kv_lora, mla_rope, dsa_heads * dsa_hd, dsa_kv_heads * dsa_hd,
                            dsa_kv_heads * dsa_hd, idx_heads * idx_dim, idx_dim])
        wq, wckv, wkr, wdq, wdk, wdv, wiq, wik, wiw = jnp.split(w, splits, axis=1)
        padc = lambda a, n: jnp.pad(a, ((0, 0), (0, n - a.shape[1])))
        w_cat = jnp.concatenate([wq, wckv, padc(wkr, LANES), wdq, wdk, wdv, wiq,
                                 padc(jnp.concatenate([wik, wiw], axis=1), LANES)], axis=1)
        assert w_cat.shape[1] == dm.n
        wabs = _absorb(mla_w_q_nope[la], mla_w_uk[la])
        wqr = mla_w_q_rope[la].reshape(q_lora, mla_heads * mla_rope)
        wuv = jnp.transpose(mla_w_uv[la], (1, 0, 2))
        return bf(w_cat), bf(wabs), bf(wqr), bf(wuv)

    def run(x, sample):
        b, t, _ = x.shape
        m = b * t
        x2d = x.reshape(m, d)
        if sample:
            bq, tq, per_row = 1, m, True
            tm = _pick(m, (128,))
            pos = jnp.full((m,), past_len, I32)
        else:
            bq, tq, per_row = b, t, False
            tm = _pick(t, (512, 256, 128))
            pos = jnp.arange(t)
        tabs = (*_rope_tables(pos, dsa_hd // 2), *_rope_tables(pos, mla_rope // 2))
        rows_mla, rows_dsa, rows_idx, states = [], [], [], []
        for l in range(depth):
            sh1, sc1, g1, sh2, sc2, g2 = mods_for(l, sample)
            nmg, nfg = row1(norm_mix_g[l]), row1(norm_ffn_g[l])
            final = l == depth - 1
            if l % 2 == 0:
                la = l // 2
                w_cat, wabs, wqr, wuv = even_weights(la)
                (qcat, mrow, kv16, dsarow, dq, dk, dvt, iq, ik, ik16, iwt) = _even_project(
                    x2d, sc1, sh1, nmg, w_cat, row1(mla_q_norm_g[la]), row1(mla_kv_norm_g[la]), wqr, wabs,
                    tabs, dm, bq, tq, tm, per_row, scales)
                if sample:
                    nsel = min(TOPK_MAX, (past_len + t) // 4)
                    pp = _pick(n_pages, (16, 8))
                    pad8 = lambda a: jnp.pad(a[:, None, :], ((0, 0), (0, 7), (0, 0)))
                    padp = lambda a: jnp.pad(a[:, None, :], ((0, 0), (0, psz - 1), (0, 0)))
                    qdec = jnp.transpose(qcat[0], (1, 0, 2))
                    page_t = lambda c: jnp.transpose(c, (0, 2, 1))
                    mla_o = _mla_decode(page_table, qdec, pad8(kv16), wuv, page_t(cache_mla[la]), pp)
                    iqd = jnp.pad(jnp.transpose(iq[0], (1, 0, 2)), ((0, 0), (0, 8 - idx_heads), (0, 0)))
                    iwd = jnp.transpose(iwt[0])[:, :, None]
                    iwd = jnp.where(jnp.arange(8)[None, :, None] < idx_heads, iwd, 0.0)
                    bias = _idx_decode(page_table, iqd, iwd, padp(ik16), page_t(cache_idx[la]), nsel, pp)
                    dqd = jnp.transpose(dq[0], (1, 0, 2))
                    zeros = jnp.zeros_like(dqd[:, :gsz])
                    qbd = jnp.concatenate(
                        [jnp.concatenate([dqd[:, :gsz], zeros], axis=2),
                         jnp.concatenate([zeros, dqd[:, gsz:]], axis=2)], axis=1)
                    cache_kv = jnp.transpose(cache_dsa_kv[la], (0, 2, 3, 4, 1)).reshape(-1, 4 * dsa_hd, psz)
                    dsa_o8 = _dsa_decode(page_table, qbd, pad8(bf(dsarow)), bias, cache_kv, pp, dsa_hd, gsz)
                    dsa_o = dsa_o8[:, :, :dsa_hd].reshape(b, dsa_heads * dsa_hd)
                    mla_o, dsa_o = bf(mla_o), bf(dsa_o)
                else:
                    nsel = min(TOPK_MAX, t // 4)
                    mla_o = _mla_prompt(qcat, kv16.reshape(b, t, -1), wuv, _pick(t, (256, 128))).reshape(m, -1)
                    dsa_o = _dsa_prompt(iq, iwt, ik16, dq, dk, dvt, nsel, LANES).reshape(m, -1)
                n_mla = mla_heads * mla_v
                mixes = [(mla_o, bf(w_out_even[la][:n_mla])), (dsa_o, bf(w_out_even[la][n_mla:]))]
                rows_mla.append(mrow.reshape(b, t, -1))
                rows_dsa.append(dsarow.reshape(b, t, 2, dsa_kv_heads, dsa_hd))
                rows_idx.append(ik.reshape(b, t, -1))
            else:
                lr = l // 2
                z = _normmod_mm(x2d, sc1, sh1, nmg, bf(w_in_odd[lr]), tq, tm, _pick(w_in_odd.shape[2], (2048,)),
                                per_row)
                lb = lb_all[l].reshape(hg_heads, 1, hg_k)
                og = row1(hgrn_out_norm_g[lr])
                if sample:
                    o_g, st = _hgrn_step(z, lb, og, state_hgrn[lr].astype(F32), hg_heads, hg_q_scale)
                else:
                    o_g, st = _hgrn_prompt(z, lb, og, b, t, hg_heads, _pick(t, (128,)), hg_q_scale)
                mixes = [(o_g, bf(w_out_odd[lr]))]
                states.append(st)
            dff = w_ffn_gate.shape[2]
            x2d = _mix_ffn(x2d, mixes, g1, nfg, sc2, sh2, g2, bf(w_ffn_gate[l]), bf(w_ffn_up[l]),
                           bf(w_ffn_down[l]), row1(final_norm_g), tq, tm, _pick(dff, (1408, 1024, 512, 256)),
                           per_row, final)
        return (x2d.reshape(b, t, d), jnp.stack(rows_mla), jnp.stack(rows_dsa), jnp.stack(rows_idx),
                jnp.stack(states))

    y_p, mla_p, dsa_p, idx_p, hg_p = run(x_prompt, False)
    y_s, mla_s, dsa_s, idx_s, hg_s = run(x_sample, True)
    return (y_p, y_s, mla_p, mla_s, dsa_p, dsa_s, idx_p, idx_s, hg_p, hg_s)
```

```python
import functools

import jax
import jax.numpy as jnp
import numpy as np
from jax import lax
from jax.experimental import pallas as pl
from jax.experimental.pallas import tpu as pltpu

F32 = jnp.float32
BF16 = jnp.bfloat16
I32 = jnp.int32

EPS = 1e-6
ROPE_THETA = 10000.0
TOPK_MAX = 256
NEG = -1e30
INT_MIN = -(2 ** 31)
LANES = 128
DSA_KEY_CHUNK = 512
MIB = 1024 * 1024

NT = (((1,), (1,)), ((), ()))
TN = (((0,), (0,)), ((), ()))


def _cp(sem, vmem_mib=48):
    return pltpu.CompilerParams(dimension_semantics=sem, vmem_limit_bytes=vmem_mib * MIB)


def _sigmoid(x):
    return 1.0 / (1.0 + jnp.exp(-x))


def _silu(x):
    return x * _sigmoid(x)


def _rms(x, g):
    return x * lax.rsqrt(jnp.mean(x * x, axis=-1, keepdims=True) + EPS) * g


def _dot(a, b):
    return jnp.dot(a, b, preferred_element_type=F32)


def _rope_cols(x, cos, sin_signed, half):
    outs = []
    lane = lax.broadcasted_iota(I32, (x.shape[0], LANES), 1)
    first = (lane % (2 * half)) < half
    for c in range(x.shape[1] // LANES):
        xc = x[:, c * LANES:(c + 1) * LANES]
        rot = jnp.where(first, pltpu.roll(xc, LANES - half, 1), pltpu.roll(xc, half, 1))
        outs.append(xc * cos + rot * sin_signed)
    return outs[0] if len(outs) == 1 else jnp.concatenate(outs, axis=1)


def _ada_kernel(c_ref, w_ref, b_ref, o_ref):
    a = _silu(c_ref[...]).astype(BF16)
    o_ref[...] = _dot(a, w_ref[...].astype(BF16)) + b_ref[...]


def _adaln(c_all, w_ada, b_ada):
    depth, d, n = w_ada.shape
    bc = c_all.shape[0]
    tn = n // 4
    return pl.pallas_call(
        _ada_kernel,
        grid=(depth, n // tn),
        in_specs=[pl.BlockSpec((bc, d), lambda l, j: (0, 0)),
                  pl.BlockSpec((None, d, tn), lambda l, j: (l, 0, j)),
                  pl.BlockSpec((None, 1, tn), lambda l, j: (l, 0, j))],
        out_specs=pl.BlockSpec((None, bc, tn), lambda l, j: (l, 0, j)),
        out_shape=jax.ShapeDtypeStruct((depth, bc, n), F32),
        compiler_params=_cp(("parallel", "parallel")),
        name="adaln",
    )(c_all, w_ada, b_ada.reshape(depth, 1, n))


def _absorb_kernel(a_ref, b_ref, o_ref):
    o_ref[...] = lax.dot_general(a_ref[...].astype(BF16), b_ref[...].astype(BF16), NT,
                                 preferred_element_type=F32)


def _absorb(w_q_nope, w_uk):
    a = jnp.transpose(w_q_nope, (1, 0, 2))
    b = jnp.transpose(w_uk, (1, 0, 2))
    h, rq, dn = a.shape
    rkv = b.shape[1]
    out = pl.pallas_call(
        _absorb_kernel,
        grid=(h,),
        in_specs=[pl.BlockSpec((None, rq, dn), lambda i: (i, 0, 0)),
                  pl.BlockSpec((None, rkv, dn), lambda i: (i, 0, 0))],
        out_specs=pl.BlockSpec((None, rq, rkv), lambda i: (i, 0, 0)),
        out_shape=jax.ShapeDtypeStruct((h, rq, rkv), F32),
        compiler_params=_cp(("parallel",)),
        name="mla_absorb",
    )(a, b)
    return jnp.transpose(out, (1, 0, 2)).reshape(rq, h * rkv)


class EvenDims:
    def __init__(self, q_lora, kv_lora, mla_rope, mla_heads, dsa_heads, dsa_kv_heads, dsa_hd, idx_heads, idx_dim):
        assert q_lora % LANES == 0 and kv_lora == LANES and mla_rope <= LANES
        assert dsa_hd == 64 and idx_dim == 64 and dsa_kv_heads == 2 and idx_heads <= 8
        self.q_lora, self.kv_lora, self.mla_rope, self.mla_heads = q_lora, kv_lora, mla_rope, mla_heads
        self.dsa_heads, self.dsa_kv_heads, self.dsa_hd = dsa_heads, dsa_kv_heads, dsa_hd
        self.idx_heads, self.idx_dim = idx_heads, idx_dim
        self.o_zq = 0
        self.o_ckv = q_lora
        self.o_kr = self.o_ckv + kv_lora
        self.o_dq = self.o_kr + LANES
        self.o_dk = self.o_dq + dsa_heads * dsa_hd
        self.o_dv = self.o_dk + dsa_kv_heads * dsa_hd
        self.o_iq = self.o_dv + dsa_kv_heads * dsa_hd
        self.o_ik = self.o_iq + idx_heads * idx_dim
        self.n = self.o_ik + LANES
        self.mla_row = kv_lora + mla_rope


def _even_kernel(x_ref, sc_ref, sh_ref, g_ref, w_ref, qg_ref, kvg_ref, wqr_ref, wabs_ref,
                 c64_ref, s64_ref, c32_ref, s32_ref,
                 qcat_ref, row_ref, kv16_ref, dsarow_ref, dq_ref, dk_ref, dvt_ref, iq_ref, ik_ref, ik16_ref,
                 iwt_ref, *, dm, mla_scale, dsa_scale, idx_scale, idx_w_scale):
    h = _rms(x_ref[...], g_ref[...]) * (1.0 + sc_ref[0]) + sh_ref[0]
    z = _dot(h.astype(BF16), w_ref[...])
    c64, s64, c32, s32 = c64_ref[...], s64_ref[...], c32_ref[...], s32_ref[...]
    tm = z.shape[0]

    cq = _rms(z[:, dm.o_zq:dm.o_zq + dm.q_lora], qg_ref[...]).astype(BF16)
    qlat = _dot(cq, wabs_ref[...]) * mla_scale
    qrope = _rope_cols(_dot(cq, wqr_ref[...]), c32, s32, dm.mla_rope // 2) * mla_scale
    for hd in range(dm.mla_heads):
        qcat_ref[0, hd, :, 0:dm.kv_lora] = qlat[:, hd * dm.kv_lora:(hd + 1) * dm.kv_lora].astype(BF16)
        qcat_ref[0, hd, :, dm.kv_lora:dm.mla_row] = qrope[:, hd * dm.mla_rope:(hd + 1) * dm.mla_rope].astype(BF16)

    ckv = _rms(z[:, dm.o_ckv:dm.o_ckv + dm.kv_lora], kvg_ref[...])
    kr = _rope_cols(z[:, dm.o_kr:dm.o_kr + LANES], c32, s32, dm.mla_rope // 2)[:, 0:dm.mla_rope]
    row_ref[:, 0:dm.kv_lora] = ckv
    row_ref[:, dm.kv_lora:dm.mla_row] = kr
    kv16_ref[:, 0:dm.kv_lora] = ckv.astype(BF16)
    kv16_ref[:, dm.kv_lora:dm.mla_row] = kr.astype(BF16)

    hd_ = dm.dsa_hd
    dq = _rope_cols(z[:, dm.o_dq:dm.o_dk], c64, s64, hd_ // 2) * dsa_scale
    for hh in range(dm.dsa_heads):
        dq_ref[0, hh] = dq[:, hh * hd_:(hh + 1) * hd_].astype(BF16)
    dk = _rope_cols(z[:, dm.o_dk:dm.o_dv], c64, s64, hd_ // 2)
    dv = z[:, dm.o_dv:dm.o_iq]
    dvt = dv.T
    if len(dsarow_ref.shape) == 3:
        dsarow_ref[0, 0:LANES, :] = dk.T
        dsarow_ref[0, LANES:2 * LANES, :] = dvt
    else:
        dsarow_ref[:, 0:LANES] = dk
        dsarow_ref[:, LANES:2 * LANES] = dv
    for gg in range(dm.dsa_kv_heads):
        dk_ref[0, gg] = dk[:, gg * hd_:(gg + 1) * hd_].astype(BF16)
    vck = dvt_ref.shape[3]
    for c in range(tm // vck):
        dvt_ref[0, c] = dvt[:, c * vck:(c + 1) * vck].astype(BF16)

    iq = _rope_cols(z[:, dm.o_iq:dm.o_ik], c64, s64, dm.idx_dim // 2) * idx_scale
    for hh in range(dm.idx_heads):
        iq_ref[0, hh] = iq[:, hh * dm.idx_dim:(hh + 1) * dm.idx_dim].astype(BF16)
    last = z[:, dm.o_ik:dm.o_ik + LANES]
    ik = _rope_cols(last, c64, s64, dm.idx_dim // 2)[:, 0:dm.idx_dim]
    ik_ref[...] = ik
    ik16_ref[...] = ik.astype(BF16)
    iwt_ref[0] = last.T[dm.idx_dim:dm.idx_dim + 8, :] * idx_w_scale


def _even_project(x2d, sc, sh, g, w_in, qg, kvg, wqr, wabs, tabs, dm, bq, tq, tm, per_row, scales):
    rows_t = not per_row
    m, d = x2d.shape
    nt = tq // tm
    vck = min(DSA_KEY_CHUNK, tm)
    nh, kvh, ih = dm.mla_heads, dm.dsa_kv_heads, dm.idx_heads
    row2 = lambda i: (i, 0)
    hm4 = lambda i: (i // nt, 0, i % nt, 0)
    if per_row:
        mod_spec = pl.BlockSpec((1, tm, d), lambda i: (0, i, 0))
    else:
        mod_spec = pl.BlockSpec((1, 1, d), lambda i: (i // nt, 0, 0))
    const = lambda shape: pl.BlockSpec(shape, lambda i: tuple(0 for _ in shape))
    tab_spec = pl.BlockSpec((tm, LANES), lambda i: (i % nt, 0))
    out_shapes = (
        jax.ShapeDtypeStruct((bq, nh, tq, dm.mla_row), BF16),
        jax.ShapeDtypeStruct((m, dm.mla_row), F32),
        jax.ShapeDtypeStruct((m, dm.mla_row), BF16),
        (jax.ShapeDtypeStruct((bq, 2 * LANES, tq), F32) if rows_t
         else jax.ShapeDtypeStruct((m, 2 * LANES), F32)),
        jax.ShapeDtypeStruct((bq, dm.dsa_heads, tq, dm.dsa_hd), BF16),
        jax.ShapeDtypeStruct((bq, kvh, tq, dm.dsa_hd), BF16),
        jax.ShapeDtypeStruct((bq, tq // vck, LANES, vck), BF16),
        jax.ShapeDtypeStruct((bq, ih, tq, dm.idx_dim), BF16),
        jax.ShapeDtypeStruct((m, dm.idx_dim), F32),
        jax.ShapeDtypeStruct((m, dm.idx_dim), BF16),
        jax.ShapeDtypeStruct((bq, 8, tq), F32),
    )
    out_specs = (
        pl.BlockSpec((1, nh, tm, dm.mla_row), hm4),
        pl.BlockSpec((tm, dm.mla_row), row2),
        pl.BlockSpec((tm, dm.mla_row), row2),
        (pl.BlockSpec((1, 2 * LANES, tm), lambda i: (i // nt, 0, i % nt)) if rows_t
         else pl.BlockSpec((tm, 2 * LANES), row2)),
        pl.BlockSpec((1, dm.dsa_heads, tm, dm.dsa_hd), hm4),
        pl.BlockSpec((1, kvh, tm, dm.dsa_hd), hm4),
        pl.BlockSpec((1, tm // vck, LANES, vck), lambda i: (i // nt, i % nt, 0, 0)),
        pl.BlockSpec((1, ih, tm, dm.idx_dim), hm4),
        pl.BlockSpec((tm, dm.idx_dim), row2),
        pl.BlockSpec((tm, dm.idx_dim), row2),
        pl.BlockSpec((1, 8, tm), lambda i: (i // nt, 0, i % nt)),
    )
    kern = functools.partial(_even_kernel, dm=dm, mla_scale=scales[0], dsa_scale=scales[1],
                             idx_scale=scales[2], idx_w_scale=scales[3])
    return pl.pallas_call(
        kern,
        grid=(m // tm,),
        in_specs=[pl.BlockSpec((tm, d), row2), mod_spec, mod_spec, const((1, d)), const(w_in.shape),
                  const(qg.shape), const(kvg.shape), const(wqr.shape), const(wabs.shape),
                  tab_spec, tab_spec, tab_spec, tab_spec],
        out_specs=out_specs,
        out_shape=out_shapes,
        compiler_params=_cp(("parallel",), 56),
        name="even_project",
    )(x2d, sc, sh, g, w_in, qg, kvg, wqr, wabs, *tabs)


def _fold_lanes(x, op):
    out = x[:, 0:LANES]
    for c in range(1, x.shape[1] // LANES):
        out = op(out, x[:, c * LANES:(c + 1) * LANES])
    return out


def _mla_kernel(q_ref, kv_ref, wuv_ref, o_ref, s_ref, mp_ref, lp_ref, acc_ref, *, tq, tk, nh, r, vd):
    qi = pl.program_id(1)
    n = nh * tq
    q = q_ref[0].reshape(n, q_ref.shape[-1])
    nch = (qi * tq + tq + tk - 1) // tk
    reps = tk // LANES

    def keys(j):
        return kv_ref[0, pl.ds(pl.multiple_of(j * tk, tk), tk), :]

    mp_ref[...] = jnp.full(mp_ref.shape, NEG, F32)

    def pass_a(j, c):
        s = lax.dot_general(q, keys(j), NT, preferred_element_type=F32)
        s_ref[j] = s
        mp_ref[...] = jnp.maximum(mp_ref[...], _fold_lanes(s, jnp.maximum))
        return c

    lax.fori_loop(0, nch - 1, pass_a, 0)
    jl = nch - 1
    s = lax.dot_general(q, keys(jl), NT, preferred_element_type=F32)
    qpos = qi * tq + lax.broadcasted_iota(I32, s.shape, 0) % tq
    kpos = jl * tk + lax.broadcasted_iota(I32, s.shape, 1)
    s = jnp.where(kpos <= qpos, s, NEG)
    s_ref[jl] = s
    mp = jnp.maximum(mp_ref[...], _fold_lanes(s, jnp.maximum))
    mb = jnp.broadcast_to(jnp.max(mp, axis=1, keepdims=True), (n, LANES))
    mbt = jnp.concatenate([mb] * reps, axis=1)

    lp_ref[...] = jnp.zeros(lp_ref.shape, F32)
    acc_ref[...] = jnp.zeros(acc_ref.shape, F32)

    def pass_b(j, c):
        p = jnp.exp2(s_ref[j] - mbt)
        lp_ref[...] += _fold_lanes(p, jnp.add)
        acc_ref[...] += _dot(p.astype(BF16), keys(j)[:, 0:r])
        return c

    lax.fori_loop(0, nch, pass_b, 0)
    lat = (acc_ref[...] / jnp.sum(lp_ref[...], axis=1, keepdims=True)).astype(BF16)
    for hd in range(nh):
        o_ref[0, :, hd * vd:(hd + 1) * vd] = _dot(lat[hd * tq:(hd + 1) * tq], wuv_ref[hd]).astype(BF16)


def _mla_prompt(qcat, kv16, wuv, tq):
    b, nh, t, dq = qcat.shape
    r, vd = wuv.shape[1], wuv.shape[2]
    tk = _pick(t, (512, 256, 128))
    n = nh * tq
    kern = functools.partial(_mla_kernel, tq=tq, tk=tk, nh=nh, r=r, vd=vd)
    return pl.pallas_call(
        kern,
        grid=(b, t // tq),
        in_specs=[pl.BlockSpec((1, nh, tq, dq), lambda bi, qi: (bi, 0, qi, 0)),
                  pl.BlockSpec((1, t, dq), lambda bi, qi: (bi, 0, 0)),
                  pl.BlockSpec(wuv.shape, lambda bi, qi: (0, 0, 0))],
        out_specs=pl.BlockSpec((1, tq, nh * vd), lambda bi, qi: (bi, qi, 0)),
        out_shape=jax.ShapeDtypeStruct((b, t, nh * vd), BF16),
        scratch_shapes=[pltpu.VMEM((t // tk, n, tk), F32), pltpu.VMEM((n, LANES), F32),
                        pltpu.VMEM((n, LANES), F32), pltpu.VMEM((n, r), F32)],
        compiler_params=_cp(("parallel", "arbitrary")),
        name="mla_prompt",
    )(qcat, kv16, wuv)


def _sort_key(score):
    b = pltpu.bitcast(score, I32)
    return jnp.where(b < 0, (b ^ jnp.int32(0x7FFFFFFF)) + 1, b)


def _topk_select(read_keys, read_pos, nch, rows, width, nsel, pos_bits):
    def count(pred):
        def body(j, c):
            return c + jnp.sum(pred(j).astype(I32).reshape(rows // 8, 8, width), axis=0)
        c = lax.fori_loop(0, nch, body, jnp.zeros((8, width), I32))
        return jnp.sum(c, axis=0, keepdims=True)

    def bit_body(b, thr):
        cand = thr + lax.shift_left(jnp.int32(1), 31 - b)
        cnt = count(lambda j: read_keys(j) >= cand)
        return jnp.where(cnt >= nsel, cand, thr)

    thr = lax.fori_loop(0, 32, bit_body, jnp.full((1, width), INT_MIN, I32))

    def tie_search():
        need = nsel - count(lambda j: read_keys(j) > thr)

        def jbit_body(b, jc):
            cand = jc + lax.shift_left(jnp.int32(1), pos_bits - 1 - b)
            cnt = count(lambda j: (read_keys(j) == thr) & (read_pos(j) < cand))
            return jnp.where(cnt <= need, cand, jc)

        return lax.fori_loop(0, pos_bits, jbit_body, jnp.zeros((1, width), I32))

    excess = (count(lambda j: read_keys(j) >= thr) > nsel) & (thr > INT_MIN)
    any_excess = jnp.max(excess.astype(I32)) > 0
    jcut = lax.cond(any_excess, tie_search, lambda: jnp.full((1, width), 2 ** pos_bits, I32))
    return thr, jcut


def _selected(keys, pos, thr, jcut):
    return ((keys > thr) | ((keys == thr) & (pos < jcut))) & (keys != INT_MIN)


def _dsa_kernel(iq_ref, iwt_ref, ik_ref, dq_ref, dk_ref, dvt_ref, o_ref, keys_ref, bias_ref, s_ref,
                *, tq, ck, nsel, ih, gsz, kvh, hd, pos_bits):
    i = pl.program_id(1)
    nch = ((i + 1) * tq + ck - 1) // ck
    row = lax.broadcasted_iota(I32, (ck, tq), 0)
    qpos = i * tq + lax.broadcasted_iota(I32, (ck, tq), 1)

    def off(j):
        return pl.multiple_of(j * ck, ck)

    def score_chunk(j, carry):
        ikc = ik_ref[0, pl.ds(off(j), ck), :]
        acc = jnp.zeros((ck, tq), F32)
        for hh in range(ih):
            d = lax.dot_general(ikc, iq_ref[0, hh], NT, preferred_element_type=F32)
            acc = acc + iwt_ref[0, hh:hh + 1, :] * jnp.maximum(d, 0.0)
        valid = (j * ck + row) <= qpos
        keys_ref[pl.ds(off(j), ck), :] = jnp.where(valid, _sort_key(acc), INT_MIN)
        return carry

    lax.fori_loop(0, nch, score_chunk, 0)

    read_keys = lambda j: keys_ref[pl.ds(off(j), ck), :]
    read_pos = lambda j: j * ck + row
    thr, jcut = _topk_select(read_keys, read_pos, nch, ck, tq, nsel, pos_bits)

    def bias_chunk(j, carry):
        sel = _selected(read_keys(j), read_pos(j), thr, jcut)
        bias_ref[pl.ds(off(j), ck), :] = jnp.where(sel, 0.0, NEG)
        return carry

    lax.fori_loop(0, nch, bias_chunk, 0)

    qs = [dq_ref[0, g * gsz:(g + 1) * gsz].reshape(gsz * tq, hd) for g in range(kvh)]
    wq = gsz * tq

    def pass_a(j, ms):
        b = bias_ref[pl.ds(off(j), ck), :]
        bt = jnp.concatenate([b] * gsz, axis=1)
        out = []
        for g in range(kvh):
            s = lax.dot_general(dk_ref[0, g, pl.ds(off(j), ck), :], qs[g], NT, preferred_element_type=F32) + bt
            s_ref[g, pl.ds(off(j), ck), :] = s
            out.append(jnp.maximum(ms[g], jnp.max(s, axis=0, keepdims=True)))
        return tuple(out)

    ms = lax.fori_loop(0, nch, pass_a, tuple(jnp.full((1, wq), NEG, F32) for _ in range(kvh)))

    def pass_b(j, carry):
        out = []
        for g in range(kvh):
            l, acc = carry[g]
            p = jnp.exp2(s_ref[g, pl.ds(off(j), ck), :] - ms[g])
            l = l + jnp.sum(p, axis=0, keepdims=True)
            acc = acc + _dot(dvt_ref[0, j, g * hd:(g + 1) * hd, :], p.astype(BF16))
            out.append((l, acc))
        return tuple(out)

    res = lax.fori_loop(0, nch, pass_b,
                        tuple((jnp.zeros((1, wq), F32), jnp.zeros((hd, wq), F32)) for _ in range(kvh)))
    outs = []
    for g in range(kvh):
        ot = res[g][1] / res[g][0]
        for hh in range(gsz):
            outs.append(ot[:, hh * tq:(hh + 1) * tq])
    o_ref[0] = jnp.concatenate(outs, axis=0).T.astype(BF16)


def _dsa_prompt(iq, iwt, ik16, dq, dk, dvt, nsel, tq):
    b, ih, t, idim = iq.shape
    nhd, hd = dq.shape[1], dq.shape[3]
    kvh = dk.shape[1]
    gsz = nhd // kvh
    ck = dvt.shape[3]
    ik3 = ik16.reshape(b, t, idim)
    pos_bits = int(t).bit_length()
    kern = functools.partial(_dsa_kernel, tq=tq, ck=ck, nsel=nsel, ih=ih, gsz=gsz, kvh=kvh, hd=hd,
                             pos_bits=pos_bits)
    return pl.pallas_call(
        kern,
        grid=(b, t // tq),
        in_specs=[pl.BlockSpec((1, ih, tq, idim), lambda bi, i: (bi, 0, i, 0)),
                  pl.BlockSpec((1, 8, tq), lambda bi, i: (bi, 0, i)),
                  pl.BlockSpec((1, t, idim), lambda bi, i: (bi, 0, 0)),
                  pl.BlockSpec((1, nhd, tq, hd), lambda bi, i: (bi, 0, i, 0)),
                  pl.BlockSpec((1, kvh, t, hd), lambda bi, i: (bi, 0, 0, 0)),
                  pl.BlockSpec((1, t // ck, 2 * hd, ck), lambda bi, i: (bi, 0, 0, 0))],
        out_specs=pl.BlockSpec((1, tq, nhd * hd), lambda bi, i: (bi, i, 0)),
        out_shape=jax.ShapeDtypeStruct((b, t, nhd * hd), BF16),
        scratch_shapes=[pltpu.VMEM((t, tq), I32), pltpu.VMEM((t, tq), F32),
                        pltpu.VMEM((kvh, t, gsz * tq), F32)],
        compiler_params=_cp(("parallel", "arbitrary")),
        name="dsa_prompt",
    )(iq, iwt, ik3, dq, dk, dvt)


def _ffn_kernel(*refs, n_mix, final):
    x_ref = refs[0]
    mix_refs = refs[1:1 + 2 * n_mix]
    (g1_ref, nfg_ref, sc_ref, sh_ref, g2_ref, wg_ref, wu_ref, wd_ref, fg_ref,
     o_ref, x1_ref, hb_ref, acc_ref) = refs[1 + 2 * n_mix:]
    f = pl.program_id(1)

    @pl.when(f == 0)
    def _():
        mix = _dot(mix_refs[0][...], mix_refs[1][...])
        for k in range(1, n_mix):
            mix = mix + _dot(mix_refs[2 * k][...], mix_refs[2 * k + 1][...])
        x1 = x_ref[...] + g1_ref[0] * mix
        x1_ref[...] = x1
        hb_ref[...] = (_rms(x1, nfg_ref[...]) * (1.0 + sc_ref[0]) + sh_ref[0]).astype(BF16)
        acc_ref[...] = jnp.zeros(acc_ref.shape, F32)

    hb = hb_ref[...]
    act = _silu(_dot(hb, wg_ref[...])) * _dot(hb, wu_ref[...])
    acc_ref[...] += _dot(act.astype(BF16), wd_ref[...])

    @pl.when(f == pl.num_programs(1) - 1)
    def _():
        y = x1_ref[...] + g2_ref[0] * acc_ref[...]
        if final:
            y = _rms(y, fg_ref[...])
        o_ref[...] = y


def _mix_ffn(x2d, mixes, g1, nfg, sc2, sh2, g2, wg, wu, wd, fg, tq, tm, tf, per_row, final):
    m, d = x2d.shape
    dff = wg.shape[1]
    nt = tq // tm
    if per_row:
        mod_spec = pl.BlockSpec((1, tm, d), lambda i, f: (0, i, 0))
    else:
        mod_spec = pl.BlockSpec((1, 1, d), lambda i, f: (i // nt, 0, 0))
    row = lambda i, f: (i, 0)
    const2 = lambda shape: pl.BlockSpec(shape, lambda i, f: (0, 0))
    in_specs = [pl.BlockSpec((tm, d), row)]
    args = [x2d]
    for a, w in mixes:
        in_specs += [pl.BlockSpec((tm, a.shape[1]), row), const2(w.shape)]
        args += [a, w]
    in_specs += [mod_spec, const2((1, d)), mod_spec, mod_spec, mod_spec,
                 pl.BlockSpec((d, tf), lambda i, f: (0, f)), pl.BlockSpec((d, tf), lambda i, f: (0, f)),
                 pl.BlockSpec((tf, d), lambda i, f: (f, 0)), const2((1, d))]
    args += [g1, nfg, sc2, sh2, g2, wg, wu, wd, fg]
    kern = functools.partial(_ffn_kernel, n_mix=len(mixes), final=final)
    return pl.pallas_call(
        kern,
        grid=(m // tm, dff // tf),
        in_specs=in_specs,
        out_specs=pl.BlockSpec((tm, d), row),
        out_shape=jax.ShapeDtypeStruct((m, d), F32),
        scratch_shapes=[pltpu.VMEM((tm, d), F32), pltpu.VMEM((tm, d), BF16), pltpu.VMEM((tm, d), F32)],
        compiler_params=_cp(("parallel", "arbitrary"), 56),
        name="mix_ffn",
    )(*args)


def _normmod_mm_kernel(x_ref, sc_ref, sh_ref, g_ref, w_ref, o_ref):
    h = _rms(x_ref[...], g_ref[...]) * (1.0 + sc_ref[0]) + sh_ref[0]
    o_ref[...] = _dot(h.astype(BF16), w_ref[...])


def _normmod_mm(x2d, sc, sh, g, w, tq, tm, tn, per_row):
    m, d = x2d.shape
    n = w.shape[1]
    nt = tq // tm
    if per_row:
        mod_spec = pl.BlockSpec((1, tm, d), lambda i, j: (0, i, 0))
    else:
        mod_spec = pl.BlockSpec((1, 1, d), lambda i, j: (i // nt, 0, 0))
    return pl.pallas_call(
        _normmod_mm_kernel,
        grid=(m // tm, n // tn),
        in_specs=[pl.BlockSpec((tm, d), lambda i, j: (i, 0)), mod_spec, mod_spec,
                  pl.BlockSpec((1, d), lambda i, j: (0, 0)), pl.BlockSpec((d, tn), lambda i, j: (0, j))],
        out_specs=pl.BlockSpec((tm, tn), lambda i, j: (i, j)),
        out_shape=jax.ShapeDtypeStruct((m, n), F32),
        compiler_params=_cp(("parallel", "parallel")),
        name="odd_in_proj",
    )(x2d, sc, sh, g, w)


def _hgrn_consts(c):
    nlev = int(np.log2(c))
    t = np.arange(c)
    tril = (t[None, :] <= t[:, None]).astype(np.float32)
    mats = [tril]
    masks = [np.eye(c, dtype=np.float32)]
    for lev in range(1, nlev + 1):
        w = 2 ** (lev - 1)
        piv = (t // (2 * w)) * (2 * w) + w - 1
        mats.append(tril - tril[piv])
        same = (t[:, None] // (2 * w)) == (t[None, :] // (2 * w))
        right = ((t // w) % 2 == 1)[:, None]
        left = ((t // w) % 2 == 0)[None, :]
        masks.append((same & right & left).astype(np.float32))
    return np.concatenate(mats, axis=0), np.stack(masks), nlev


def _split3(x):
    a = x.astype(BF16)
    r = x - a.astype(F32)
    b = r.astype(BF16)
    c = (r - b.astype(F32)).astype(BF16)
    return a, b, c


def _hgrn_kernel(zq_ref, zf_ref, zi_ref, zg_ref, lb_ref, og_ref, cm_ref, bm_ref, o_ref, st_ref, s_ref,
                 *, c, nlev, q_scale, hpb, kk, vd):
    ci = pl.program_id(2)

    @pl.when(ci == 0)
    def _():
        s_ref[...] = jnp.zeros(s_ref.shape, F32)

    for hh in range(hpb):
        ks = slice(hh * kk, (hh + 1) * kk)
        vs = slice(hh * vd, (hh + 1) * vd)
        lb = lb_ref[hh]
        zf = zf_ref[:, ks]
        e = jnp.exp(-jnp.abs(zf))
        inv = 1.0 / (1.0 + e)
        sig_pos = jnp.where(zf >= 0, inv, e * inv)
        sig_neg = jnp.where(zf >= 0, e * inv, inv)
        logf = jnp.log(lb + (1.0 - lb) * sig_pos)
        k = (1.0 - lb) * sig_neg
        q = _silu(zq_ref[:, ks]) * q_scale
        v = zi_ref[:, vs].astype(BF16)

        a3 = jnp.concatenate(_split3(logf), axis=1)
        gd = _dot(cm_ref[...], a3)
        gd = gd[:, 0:kk] + gd[:, kk:2 * kk] + gd[:, 2 * kk:3 * kk]
        gcum = gd[0:c]
        glast = gcum[c - 1:c]

        amat = bm_ref[0] * lax.dot_general(q.astype(BF16), k.astype(BF16), NT, preferred_element_type=F32)
        for lev in range(1, nlev + 1):
            ed = jnp.exp(-jnp.abs(gd[lev * c:(lev + 1) * c]))
            amat = amat + bm_ref[lev] * lax.dot_general((q * ed).astype(BF16), (k * ed).astype(BF16), NT,
                                                        preferred_element_type=F32)
        s_prev = s_ref[hh]
        o = _dot(amat.astype(BF16), v) + _dot((q * jnp.exp(gcum)).astype(BF16), s_prev.astype(BF16))
        kdt = (k * jnp.exp(glast - gcum)).T.astype(BF16)
        decay_col = jnp.broadcast_to(jnp.exp(glast), (kk, kk)).T
        s_new = decay_col * s_prev + _dot(kdt, v)
        s_ref[hh] = s_new
        o_ref[:, vs] = (_rms(o, og_ref[...]) * _silu(zg_ref[:, vs])).astype(BF16)

    @pl.when(ci == pl.num_programs(2) - 1)
    def _():
        st_ref[...] = s_ref[...]


def _hgrn_prompt(z, lb, og, b, t, nh, c, q_scale):
    m = z.shape[0]
    kk = lb.shape[-1]
    vd = og.shape[-1]
    nc = t // c
    hpb = _pick(nh, (4, 2, 1))
    ng = nh // hpb
    cm, bm, nlev = _hgrn_consts(c)
    cm = jnp.asarray(cm, BF16)
    bm = jnp.asarray(bm, F32)
    seg = lambda s, w: pl.BlockSpec((c, hpb * w), lambda bi, h, ci: (bi * nc + ci, s * ng + h))
    kern = functools.partial(_hgrn_kernel, c=c, nlev=nlev, q_scale=q_scale, hpb=hpb, kk=kk, vd=vd)
    return pl.pallas_call(
        kern,
        grid=(b, ng, nc),
        in_specs=[seg(0, kk), seg(1, kk), seg(2, vd), seg(3, vd),
                  pl.BlockSpec((hpb, 1, kk), lambda bi, h, ci: (h, 0, 0)),
                  pl.BlockSpec((1, vd), lambda bi, h, ci: (0, 0)),
                  pl.BlockSpec(cm.shape, lambda bi, h, ci: (0, 0)),
                  pl.BlockSpec(bm.shape, lambda bi, h, ci: (0, 0, 0))],
        out_specs=(pl.BlockSpec((c, hpb * vd), lambda bi, h, ci: (bi * nc + ci, h)),
                   pl.BlockSpec((None, hpb, kk, vd), lambda bi, h, ci: (bi, h, 0, 0))),
        out_shape=(jax.ShapeDtypeStruct((m, nh * vd), BF16),
                   jax.ShapeDtypeStruct((b, nh, kk, vd), F32)),
        scratch_shapes=[pltpu.VMEM((hpb, kk, vd), F32)],
        compiler_params=_cp(("parallel", "parallel", "arbitrary")),
        name="hgrn_prompt",
    )(z, z, z, z, lb, og, cm, bm)


def _hgrn_step_kernel(z_ref, lb_ref, og_ref, s_ref, o_ref, st_ref, *, nh, kk, vd, bt, q_scale):
    z = z_ref[...]
    hf = nh * kk
    for h in range(nh):
        lb = lb_ref[h]
        zf = z[:, hf + h * kk:hf + (h + 1) * kk]
        e = jnp.exp(-jnp.abs(zf))
        inv = 1.0 / (1.0 + e)
        f = lb + (1.0 - lb) * jnp.where(zf >= 0, inv, e * inv)
        k = (1.0 - lb) * jnp.where(zf >= 0, e * inv, inv)
        q = _silu(z[:, h * kk:(h + 1) * kk]) * q_scale
        v = z[:, 2 * hf + h * vd:2 * hf + (h + 1) * vd]
        gate = _silu(z[:, 2 * hf + nh * vd + h * vd:2 * hf + nh * vd + (h + 1) * vd])
        ft, kt, qt = f.T, k.T, q.T
        for j in range(bt):
            s_new = ft[:, j:j + 1] * s_ref[j, h] + kt[:, j:j + 1] * v[j:j + 1, :]
            st_ref[j, h] = s_new
            o = jnp.sum(qt[:, j:j + 1] * s_new, axis=0, keepdims=True)
            o_ref[j:j + 1, h * vd:(h + 1) * vd] = (_rms(o, og_ref[...]) * gate[j:j + 1, :]).astype(BF16)


def _hgrn_step(z, lb, og, state, nh, q_scale):
    bsz = z.shape[0]
    kk, vd = state.shape[2], state.shape[3]
    bt = 8
    kern = functools.partial(_hgrn_step_kernel, nh=nh, kk=kk, vd=vd, bt=bt, q_scale=q_scale)
    return pl.pallas_call(
        kern,
        grid=(bsz // bt,),
        in_specs=[pl.BlockSpec((bt, z.shape[1]), lambda i: (i, 0)),
                  pl.BlockSpec(lb.shape, lambda i: (0, 0, 0)),
                  pl.BlockSpec((1, vd), lambda i: (0, 0)),
                  pl.BlockSpec((bt, nh, kk, vd), lambda i: (i, 0, 0, 0))],
        out_specs=(pl.BlockSpec((bt, nh * vd), lambda i: (i, 0)),
                   pl.BlockSpec((bt, nh, kk, vd), lambda i: (i, 0, 0, 0))),
        out_shape=(jax.ShapeDtypeStruct((bsz, nh * vd), BF16),
                   jax.ShapeDtypeStruct(state.shape, F32)),
        compiler_params=_cp(("parallel",)),
        name="hgrn_step",
    )(z, lb, og, state)


def _stage_pages(pages):
    return jnp.concatenate([pg[...].astype(BF16) for pg in pages], axis=1)


def _paged_softmax_pv(s_ref, pg_ref, s_new, new_v, v_rows):
    ng, nh, w = s_ref.shape
    mp = s_ref[0]
    for gi in range(1, ng):
        mp = jnp.maximum(mp, s_ref[gi])
    m = jnp.maximum(jnp.max(_fold_lanes(mp, jnp.maximum), axis=1, keepdims=True),
                    jnp.max(s_new, axis=1, keepdims=True))
    mbt = jnp.concatenate([jnp.broadcast_to(m, (nh, LANES))] * (w // LANES), axis=1)
    lp = jnp.zeros((nh, LANES), F32)
    acc = None
    for gi in range(ng):
        p = jnp.exp2(s_ref[gi] - mbt)
        lp = lp + _fold_lanes(p, jnp.add)
        pv = lax.dot_general(p.astype(BF16), pg_ref[gi, v_rows, :], NT, preferred_element_type=F32)
        acc = pv if acc is None else acc + pv
    p_new = jnp.exp2(s_new - m)
    l = jnp.sum(lp, axis=1, keepdims=True) + jnp.sum(p_new, axis=1, keepdims=True)
    return (acc + _dot(p_new.astype(BF16), new_v)) / l


def _mla_dec2_kernel(pt_ref, q_ref, new_ref, wuv_ref, *rest, pp, r, vd, nh):
    pages = rest[:pp]
    o_ref, pg_ref, s_ref = rest[pp:]
    g = pl.program_id(1)
    q = q_ref[...]
    slab = _stage_pages(pages)
    pg_ref[g] = slab
    s_ref[g] = _dot(q, slab)

    @pl.when(g == pl.num_programs(1) - 1)
    def _():
        new = new_ref[...]
        s_new = lax.dot_general(q, new, NT, preferred_element_type=F32)
        s_new = jnp.where(lax.broadcasted_iota(I32, s_new.shape, 1) == 0, s_new, NEG)
        lat = _paged_softmax_pv(s_ref, pg_ref, s_new, new[:, 0:r], slice(0, r)).astype(BF16)
        for hd in range(nh):
            o_ref[0:1, hd * vd:(hd + 1) * vd] = _dot(lat[hd:hd + 1], wuv_ref[hd])
        o_ref[1:8, :] = jnp.zeros((7, nh * vd), F32)


def _mla_decode2(page_table, qdec, newrow, wuv, cache, pp):
    bsz, nh, dq = qdec.shape
    n_pages = page_table.shape[1]
    psz = cache.shape[2]
    r, vd = wuv.shape[1], wuv.shape[2]
    ng = n_pages // pp
    pt = page_table.reshape(-1)

    def page_map(k):
        return lambda b, g, ptr: (ptr[b * n_pages + g * pp + k], 0, 0)

    kern = functools.partial(_mla_dec2_kernel, pp=pp, r=r, vd=vd, nh=nh)
    grid_spec = pltpu.PrefetchScalarGridSpec(
        num_scalar_prefetch=1,
        grid=(bsz, n_pages // pp),
        in_specs=[pl.BlockSpec((None, nh, dq), lambda b, g, ptr: (b, 0, 0)),
                  pl.BlockSpec((None, 8, dq), lambda b, g, ptr: (b, 0, 0)),
                  pl.BlockSpec(wuv.shape, lambda b, g, ptr: (0, 0, 0))]
                 + [pl.BlockSpec((None, dq, psz), page_map(k)) for k in range(pp)],
        out_specs=pl.BlockSpec((None, 8, nh * vd), lambda b, g, ptr: (b, 0, 0)),
        scratch_shapes=[pltpu.VMEM((ng, dq, pp * psz), BF16), pltpu.VMEM((ng, nh, pp * psz), F32)],
    )
    out = pl.pallas_call(
        kern,
        grid_spec=grid_spec,
        out_shape=jax.ShapeDtypeStruct((bsz, 8, nh * vd), F32),
        compiler_params=_cp(("parallel", "arbitrary")),
        name="mla_decode",
    )(pt, qdec, newrow, wuv, *([cache] * pp))
    return out[:, 0, :]


def _idx_score_kernel(pt_ref, iq_ref, iw_ref, *rest, pp):
    pages = rest[:pp]
    keys_ref = rest[pp]
    iq = iq_ref[...]
    iw = iw_ref[...]
    d = _dot(iq, _stage_pages(pages))
    keys_ref[...] = _sort_key(jnp.sum(iw * jnp.maximum(d, 0.0), axis=0, keepdims=True))


def _idx_select_kernel(keys_ref, iq_ref, iw_ref, iknew_ref, bias_ref, bnew_ref, *, eb, nsel, pos_bits):
    ng, w = keys_ref.shape[1], keys_ref.shape[2]
    pos = lax.broadcasted_iota(I32, (ng, w), 0) * w + lax.broadcasted_iota(I32, (ng, w), 1)
    knew = []
    for e in range(eb):
        d = lax.dot_general(iq_ref[e], iknew_ref[e], NT, preferred_element_type=F32)
        snew = _sort_key(jnp.sum(iw_ref[e] * jnp.maximum(d, 0.0), axis=0, keepdims=True))
        knew.append(snew[0:1, 0:1])

    def count(pred):
        return jnp.sum(jnp.sum(pred.astype(I32), axis=0, keepdims=True), axis=1, keepdims=True)

    def bit_body(b, thrs):
        out = []
        for e in range(eb):
            cand = thrs[e] + lax.shift_left(jnp.int32(1), 31 - b)
            cnt = count(keys_ref[e] >= cand) + (knew[e] >= cand).astype(I32)
            out.append(jnp.where(cnt >= nsel, cand, thrs[e]))
        return tuple(out)

    thrs = lax.fori_loop(0, 32, bit_body, tuple(jnp.full((1, 1), INT_MIN, I32) for _ in range(eb)))
    needs = tuple(nsel - count(keys_ref[e] > thrs[e]) - (knew[e] > thrs[e]).astype(I32) for e in range(eb))

    def jbit_body(b, jcs):
        out = []
        for e in range(eb):
            cand = jcs[e] + lax.shift_left(jnp.int32(1), pos_bits - 1 - b)
            cnt = count((keys_ref[e] == thrs[e]) & (pos < cand))
            out.append(jnp.where(cnt <= needs[e], cand, jcs[e]))
        return tuple(out)

    jcs = lax.fori_loop(0, pos_bits, jbit_body, tuple(jnp.zeros((1, 1), I32) for _ in range(eb)))
    for e in range(eb):
        bias_ref[e] = jnp.where(_selected(keys_ref[e], pos, thrs[e], jcs[e]), 0.0, NEG)
        ties_taken = count((keys_ref[e] == thrs[e]) & (pos < jcs[e]))
        new_sel = (knew[e] > thrs[e]) | ((knew[e] == thrs[e]) & (ties_taken < needs[e]))
        slot0 = (lax.broadcasted_iota(I32, (8, LANES), 0) == 0) & (lax.broadcasted_iota(I32, (8, LANES), 1) == 0)
        bnew_ref[e] = jnp.where(slot0 & jnp.broadcast_to(new_sel, (8, LANES)), 0.0, NEG)


def _idx_decode2(page_table, iqd, iwd, iknew, cache, nsel, pp):
    bsz, _, idim = iqd.shape
    n_pages = page_table.shape[1]
    psz = cache.shape[2]
    ng, w = n_pages // pp, pp * psz
    pos_bits = int(n_pages * psz).bit_length()
    pt = page_table.reshape(-1)

    def page_map(k):
        return lambda b, g, ptr: (ptr[b * n_pages + g * pp + k], 0, 0)

    grid_spec = pltpu.PrefetchScalarGridSpec(
        num_scalar_prefetch=1,
        grid=(bsz, n_pages // pp),
        in_specs=[pl.BlockSpec((None, 8, idim), lambda b, g, ptr: (b, 0, 0)),
                  pl.BlockSpec((None, 8, 1), lambda b, g, ptr: (b, 0, 0))]
                 + [pl.BlockSpec((None, idim, psz), page_map(k)) for k in range(pp)],
        out_specs=pl.BlockSpec((None, None, 1, w), lambda b, g, ptr: (b, g, 0, 0)),
    )
    keys = pl.pallas_call(
        functools.partial(_idx_score_kernel, pp=pp),
        grid_spec=grid_spec,
        out_shape=jax.ShapeDtypeStruct((bsz, ng, 1, w), I32),
        compiler_params=_cp(("parallel", "parallel")),
        name="idx_score",
    )(pt, iqd, iwd, *([cache] * pp))

    eb = _pick(bsz, (8, 4, 2, 1))
    blk = lambda *shape: pl.BlockSpec((eb,) + shape, lambda i: (i,) + tuple(0 for _ in shape))
    return pl.pallas_call(
        functools.partial(_idx_select_kernel, eb=eb, nsel=nsel, pos_bits=pos_bits),
        grid=(bsz // eb,),
        in_specs=[blk(ng, w), blk(8, idim), blk(8, 1), blk(8, idim)],
        out_specs=(blk(ng, w), blk(8, LANES)),
        out_shape=(jax.ShapeDtypeStruct((bsz, ng, w), F32), jax.ShapeDtypeStruct((bsz, 8, LANES), F32)),
        compiler_params=_cp(("parallel",)),
        name="idx_select",
    )(keys.reshape(bsz, ng, w), iqd, iwd, iknew)


def _dsa_dec2_kernel(pt_ref, q_ref, new_ref, bias_ref, bnew_ref, *rest, pp, hd, gsz):
    pages = rest[:pp]
    o_ref, pg_ref, s_ref = rest[pp:]
    g = pl.program_id(1)
    q = q_ref[...]
    kw = 2 * hd
    slab = _stage_pages(pages)
    pg_ref[g] = slab
    s_ref[g] = _dot(q, slab[0:kw, :]) + bias_ref[g]

    @pl.when(g == pl.num_programs(1) - 1)
    def _():
        new = new_ref[...]
        s_new = lax.dot_general(q, new[:, 0:kw], NT, preferred_element_type=F32)
        s_new = s_new + bnew_ref[0:1, 0:8]
        o = _paged_softmax_pv(s_ref, pg_ref, s_new, new[:, kw:2 * kw], slice(kw, 2 * kw))
        rowi = lax.broadcasted_iota(I32, o.shape, 0)
        o_ref[...] = jnp.where(rowi < gsz, o, pltpu.roll(o, hd, 1))


def _dsa_decode2(page_table, qbd, newkv, bias, bias_new, cache, pp, hd, gsz):
    bsz = qbd.shape[0]
    n_pages = page_table.shape[1]
    psz = cache.shape[2]
    ng, w = n_pages // pp, pp * psz
    pt = page_table.reshape(-1)

    def page_map(k):
        return lambda b, g, ptr: (ptr[b * n_pages + g * pp + k], 0, 0)

    kern = functools.partial(_dsa_dec2_kernel, pp=pp, hd=hd, gsz=gsz)
    grid_spec = pltpu.PrefetchScalarGridSpec(
        num_scalar_prefetch=1,
        grid=(bsz, ng),
        in_specs=[pl.BlockSpec((None, 8, 2 * hd), lambda b, g, ptr: (b, 0, 0)),
                  pl.BlockSpec((None, 8, 4 * hd), lambda b, g, ptr: (b, 0, 0)),
                  pl.BlockSpec((None, ng, 1, w), lambda b, g, ptr: (b, 0, 0, 0)),
                  pl.BlockSpec((None, 8, LANES), lambda b, g, ptr: (b, 0, 0))]
                 + [pl.BlockSpec((None, 4 * hd, psz), page_map(k)) for k in range(pp)],
        out_specs=pl.BlockSpec((None, 8, 2 * hd), lambda b, g, ptr: (b, 0, 0)),
        scratch_shapes=[pltpu.VMEM((ng, 4 * hd, w), BF16), pltpu.VMEM((ng, 8, w), F32)],
    )
    return pl.pallas_call(
        kern,
        grid_spec=grid_spec,
        out_shape=jax.ShapeDtypeStruct((bsz, 8, 2 * hd), F32),
        compiler_params=_cp(("parallel", "arbitrary")),
        name="dsa_decode",
    )(pt, qbd, newkv, bias.reshape(bsz, ng, 1, w), bias_new, *([cache] * pp))


def _rope_tables(pos, half):
    inv_freq = ROPE_THETA ** (-jnp.arange(half, dtype=F32) / half)
    ang = pos.astype(F32)[:, None] * inv_freq[None, :]
    cos, sin = jnp.cos(ang), jnp.sin(ang)
    reps = LANES // (2 * half)
    return (jnp.tile(jnp.concatenate([cos, cos], axis=1), (1, reps)),
            jnp.tile(jnp.concatenate([-sin, sin], axis=1), (1, reps)))


def _pick(n, pref):
    for c in pref:
        if n % c == 0:
            return c
    return n


def kernel(x_prompt, x_sample, cache_mla, cache_dsa_kv, cache_idx, state_hgrn, page_table, c_prompt, c_sample,
           w_ada, b_ada, norm_mix_g, norm_ffn_g, w_in_even, mla_q_norm_g, mla_kv_norm_g, mla_w_q_nope,
           mla_w_q_rope, mla_w_uk, mla_w_uv, w_out_even, w_in_odd, hgrn_lower_bounds, hgrn_out_norm_g,
           w_out_odd, w_ffn_gate, w_ffn_up, w_ffn_down, final_norm_g):
    depth, d, _ = w_ada.shape
    bp, tp, _ = x_prompt.shape
    bs, ts, _ = x_sample.shape
    assert ts == 1
    q_lora, mla_heads, mla_nope = mla_w_q_nope.shape[1:]
    mla_rope = mla_w_q_rope.shape[3]
    kv_lora, _, mla_v = mla_w_uv.shape[1:]
    dsa_kv_heads, dsa_hd = cache_dsa_kv.shape[4:]
    idx_dim = cache_idx.shape[3]
    n_pages = page_table.shape[1]
    psz = cache_mla.shape[2]
    past_len = n_pages * psz
    sizes_tail = w_in_even.shape[2] - (q_lora + kv_lora + mla_rope + 2 * dsa_kv_heads * dsa_hd + idx_dim)
    dsa_heads = (w_out_even.shape[1] - mla_heads * mla_v) // dsa_hd
    idx_heads = (sizes_tail - dsa_heads * dsa_hd) // (idx_dim + 1)
    dm = EvenDims(q_lora, kv_lora, mla_rope, mla_heads, dsa_heads, dsa_kv_heads, dsa_hd, idx_heads, idx_dim)
    gsz = dsa_heads // dsa_kv_heads
    hg_f = hgrn_lower_bounds.shape[1]
    hg_heads = state_hgrn.shape[2]
    hg_k = state_hgrn.shape[3]
    hg_v = state_hgrn.shape[4]
    log2e = float(np.log2(np.e))
    scales = ((mla_nope + mla_rope) ** -0.5 * log2e, dsa_hd ** -0.5 * log2e, idx_dim ** -0.5, idx_heads ** -0.5)
    hg_q_scale = hg_k ** -0.5

    lb_all = jax.nn.softmax(hgrn_lower_bounds.astype(F32), axis=0)
    lb_all = jnp.cumsum(lb_all, axis=0) - lb_all[0]

    c_all = jnp.concatenate([c_prompt, c_sample], axis=0)
    mods = _adaln(c_all, w_ada, b_ada)

    def mods_for(l, sample):
        mm = mods[l, bp:] if sample else mods[l, :bp]
        parts = [mm[:, k * d:(k + 1) * d] for k in range(6)]
        return [p[None] if sample else p[:, None, :] for p in parts]

    bf = lambda a: a.astype(BF16)
    row1 = lambda a: a.reshape(1, -1).astype(F32)

    def even_weights(la):
        w = w_in_even[la]
        splits = np.cumsum([q_lora, kv_lora, mla_rope, dsa_heads * dsa_hd, dsa_kv_heads * dsa_hd,
                            dsa_kv_heads * dsa_hd, idx_heads * idx_dim, idx_dim])
        wq, wckv, wkr, wdq, wdk, wdv, wiq, wik, wiw = jnp.split(w, splits, axis=1)
        padc = lambda a, n: jnp.pad(a, ((0, 0), (0, n - a.shape[1])))
        w_cat = jnp.concatenate([wq, wckv, padc(wkr, LANES), wdq, wdk, wdv, wiq,
                                 padc(jnp.concatenate([wik, wiw], axis=1), LANES)], axis=1)
        assert w_cat.shape[1] == dm.n
        wabs = _absorb(mla_w_q_nope[la], mla_w_uk[la])
        wqr = mla_w_q_rope[la].reshape(q_lora, mla_heads * mla_rope)
        wuv = jnp.transpose(mla_w_uv[la], (1, 0, 2))
        return bf(w_cat), bf(wabs), bf(wqr), bf(wuv)

    def run(x, sample):
        b, t, _ = x.shape
        m = b * t
        x2d = x.reshape(m, d)
        if sample:
            bq, tq, per_row = 1, m, True
            tm = _pick(m, (128,))
            pos = jnp.full((m,), past_len, I32)
        else:
            bq, tq, per_row = b, t, False
            tm = _pick(t, (512, 256, 128))
            pos = jnp.arange(t)
        tabs = (*_rope_tables(pos, dsa_hd // 2), *_rope_tables(pos, mla_rope // 2))
        rows_mla, rows_dsa, rows_idx, states = [], [], [], []
        for l in range(depth):
            sh1, sc1, g1, sh2, sc2, g2 = mods_for(l, sample)
            nmg, nfg = row1(norm_mix_g[l]), row1(norm_ffn_g[l])
            final = l == depth - 1
            if l % 2 == 0:
                la = l // 2
                w_cat, wabs, wqr, wuv = even_weights(la)
                (qcat, mrow, kv16, dsarow, dq, dk, dvt, iq, ik, ik16, iwt) = _even_project(
                    x2d, sc1, sh1, nmg, w_cat, row1(mla_q_norm_g[la]), row1(mla_kv_norm_g[la]), wqr, wabs,
                    tabs, dm, bq, tq, tm, per_row, scales)
                if sample:
                    nsel = min(TOPK_MAX, (past_len + t) // 4)
                    pp = _pick(n_pages, (16, 8))
                    pad8 = lambda a: jnp.pad(a[:, None, :], ((0, 0), (0, 7), (0, 0)))
                    qdec = jnp.transpose(qcat[0], (1, 0, 2))
                    page_t = lambda c: jnp.transpose(c, (0, 2, 1))
                    mla_o = _mla_decode2(page_table, qdec, pad8(kv16), wuv, page_t(cache_mla[la]), pp)
                    iqd = jnp.pad(jnp.transpose(iq[0], (1, 0, 2)), ((0, 0), (0, 8 - idx_heads), (0, 0)))
                    iwd = jnp.transpose(iwt[0])[:, :, None]
                    iwd = jnp.where(jnp.arange(8)[None, :, None] < idx_heads, iwd, 0.0)
                    bias, bias_new = _idx_decode2(page_table, iqd, iwd, pad8(ik16), page_t(cache_idx[la]), nsel, pp)
                    dqd = jnp.transpose(dq[0], (1, 0, 2))
                    zeros = jnp.zeros_like(dqd[:, :gsz])
                    qbd = jnp.concatenate(
                        [jnp.concatenate([dqd[:, :gsz], zeros], axis=2),
                         jnp.concatenate([zeros, dqd[:, gsz:]], axis=2)], axis=1)
                    cache_kv = jnp.transpose(cache_dsa_kv[la], (0, 2, 3, 4, 1)).reshape(-1, 4 * dsa_hd, psz)
                    dsa_o8 = _dsa_decode2(page_table, qbd, pad8(bf(dsarow)), bias, bias_new, cache_kv, pp, dsa_hd,
                                          gsz)
                    dsa_o = dsa_o8[:, :, :dsa_hd].reshape(b, dsa_heads * dsa_hd)
                    mla_o, dsa_o = bf(mla_o), bf(dsa_o)
                else:
                    nsel = min(TOPK_MAX, t // 4)
                    mla_o = _mla_prompt(qcat, kv16.reshape(b, t, -1), wuv, _pick(t, (256, 128))).reshape(m, -1)
                    dsa_o = _dsa_prompt(iq, iwt, ik16, dq, dk, dvt, nsel, LANES).reshape(m, -1)
                n_mla = mla_heads * mla_v
                mixes = [(mla_o, bf(w_out_even[la][:n_mla])), (dsa_o, bf(w_out_even[la][n_mla:]))]
                rows_mla.append(mrow.reshape(b, t, -1))
                if sample:
                    rows_dsa.append(dsarow.reshape(b, t, 2, dsa_kv_heads, dsa_hd))
                else:
                    rows_dsa.append(jnp.transpose(dsarow.reshape(b, 2, dsa_kv_heads, dsa_hd, t), (0, 4, 1, 2, 3)))
                rows_idx.append(ik.reshape(b, t, -1))
            else:
                lr = l // 2
                z = _normmod_mm(x2d, sc1, sh1, nmg, bf(w_in_odd[lr]), tq, tm, _pick(w_in_odd.shape[2], (2048,)),
                                per_row)
                lb = lb_all[l].reshape(hg_heads, 1, hg_k)
                og = row1(hgrn_out_norm_g[lr])
                if sample:
                    o_g, st = _hgrn_step(z, lb, og, state_hgrn[lr].astype(F32), hg_heads, hg_q_scale)
                else:
                    o_g, st = _hgrn_prompt(z, lb, og, b, t, hg_heads, _pick(t, (128,)), hg_q_scale)
                mixes = [(o_g, bf(w_out_odd[lr]))]
                states.append(st)
            dff = w_ffn_gate.shape[2]
            x2d = _mix_ffn(x2d, mixes, g1, nfg, sc2, sh2, g2, bf(w_ffn_gate[l]), bf(w_ffn_up[l]),
                           bf(w_ffn_down[l]), row1(final_norm_g), tq, tm, _pick(dff, (1408, 1024, 512, 256)),
                           per_row, final)
        return (x2d.reshape(b, t, d), jnp.stack(rows_mla), jnp.stack(rows_dsa), jnp.stack(rows_idx),
                jnp.stack(states))

    y_p, mla_p, dsa_p, idx_p, hg_p = run(x_prompt, False)
    y_s, mla_s, dsa_s, idx_s, hg_s = run(x_sample, True)
    return (y_p, y_s, mla_p, mla_s, dsa_p, dsa_s, idx_p, idx_s, hg_p, hg_s)
```

```python
import functools

import jax
import jax.numpy as jnp
import numpy as np
from jax import lax
from jax.experimental import pallas as pl
from jax.experimental.pallas import tpu as pltpu

F32 = jnp.float32
BF16 = jnp.bfloat16
I32 = jnp.int32

EPS = 1e-6
ROPE_THETA = 10000.0
TOPK_MAX = 256
NEG = -1e30
INT_MIN = -(2 ** 31)
LANES = 128
DSA_KEY_CHUNK = 512
MIB = 1024 * 1024

NT = (((1,), (1,)), ((), ()))
TN = (((0,), (0,)), ((), ()))


def _cp(sem, vmem_mib=48):
    return pltpu.CompilerParams(dimension_semantics=sem, vmem_limit_bytes=vmem_mib * MIB)


def _sigmoid(x):
    return 1.0 / (1.0 + jnp.exp(-x))


def _silu(x):
    return x * _sigmoid(x)


def _rms(x, g):
    return x * lax.rsqrt(jnp.mean(x * x, axis=-1, keepdims=True) + EPS) * g


def _dot(a, b):
    return jnp.dot(a, b, preferred_element_type=F32)


def _rope_cols(x, cos, sin_signed, half):
    outs = []
    lane = lax.broadcasted_iota(I32, (x.shape[0], LANES), 1)
    first = (lane % (2 * half)) < half
    for c in range(x.shape[1] // LANES):
        xc = x[:, c * LANES:(c + 1) * LANES]
        rot = jnp.where(first, pltpu.roll(xc, LANES - half, 1), pltpu.roll(xc, half, 1))
        outs.append(xc * cos + rot * sin_signed)
    return outs[0] if len(outs) == 1 else jnp.concatenate(outs, axis=1)


def _ada_kernel(c_ref, w_ref, b_ref, o_ref):
    a = _silu(c_ref[...]).astype(BF16)
    o_ref[...] = _dot(a, w_ref[...].astype(BF16)) + b_ref[...]


def _adaln(c_all, w_ada, b_ada):
    depth, d, n = w_ada.shape
    bc = c_all.shape[0]
    tn = n // 4
    return pl.pallas_call(
        _ada_kernel,
        grid=(depth, n // tn),
        in_specs=[pl.BlockSpec((bc, d), lambda l, j: (0, 0)),
                  pl.BlockSpec((None, d, tn), lambda l, j: (l, 0, j)),
                  pl.BlockSpec((None, 1, tn), lambda l, j: (l, 0, j))],
        out_specs=pl.BlockSpec((None, bc, tn), lambda l, j: (l, 0, j)),
        out_shape=jax.ShapeDtypeStruct((depth, bc, n), F32),
        compiler_params=_cp(("parallel", "parallel")),
        name="adaln",
    )(c_all, w_ada, b_ada.reshape(depth, 1, n))


def _absorb_kernel(a_ref, b_ref, o_ref):
    o_ref[...] = lax.dot_general(a_ref[...].astype(BF16), b_ref[...].astype(BF16), NT,
                                 preferred_element_type=F32)


def _absorb(w_q_nope, w_uk):
    a = jnp.transpose(w_q_nope, (1, 0, 2))
    b = jnp.transpose(w_uk, (1, 0, 2))
    h, rq, dn = a.shape
    rkv = b.shape[1]
    out = pl.pallas_call(
        _absorb_kernel,
        grid=(h,),
        in_specs=[pl.BlockSpec((None, rq, dn), lambda i: (i, 0, 0)),
                  pl.BlockSpec((None, rkv, dn), lambda i: (i, 0, 0))],
        out_specs=pl.BlockSpec((None, rq, rkv), lambda i: (i, 0, 0)),
        out_shape=jax.ShapeDtypeStruct((h, rq, rkv), F32),
        compiler_params=_cp(("parallel",)),
        name="mla_absorb",
    )(a, b)
    return jnp.transpose(out, (1, 0, 2)).reshape(rq, h * rkv)


class EvenDims:
    def __init__(self, q_lora, kv_lora, mla_rope, mla_heads, dsa_heads, dsa_kv_heads, dsa_hd, idx_heads, idx_dim):
        assert q_lora % LANES == 0 and kv_lora == LANES and mla_rope <= LANES
        assert dsa_hd == 64 and idx_dim == 64 and dsa_kv_heads == 2 and idx_heads <= 8
        self.q_lora, self.kv_lora, self.mla_rope, self.mla_heads = q_lora, kv_lora, mla_rope, mla_heads
        self.dsa_heads, self.dsa_kv_heads, self.dsa_hd = dsa_heads, dsa_kv_heads, dsa_hd
        self.idx_heads, self.idx_dim = idx_heads, idx_dim
        self.o_zq = 0
        self.o_ckv = q_lora
        self.o_kr = self.o_ckv + kv_lora
        self.o_dq = self.o_kr + LANES
        self.o_dk = self.o_dq + dsa_heads * dsa_hd
        self.o_dv = self.o_dk + dsa_kv_heads * dsa_hd
        self.o_iq = self.o_dv + dsa_kv_heads * dsa_hd
        self.o_ik = self.o_iq + idx_heads * idx_dim
        self.n = self.o_ik + LANES
        self.mla_row = kv_lora + mla_rope


def _even_kernel(x_ref, sc_ref, sh_ref, g_ref, w_ref, qg_ref, kvg_ref, wqr_ref, wabs_ref,
                 c64_ref, s64_ref, c32_ref, s32_ref,
                 qcat_ref, row_ref, kv16_ref, dsarow_ref, dq_ref, dk_ref, dvt_ref, iq_ref, ik_ref, ik16_ref,
                 iwt_ref, *, dm, mla_scale, dsa_scale, idx_scale, idx_w_scale):
    h = _rms(x_ref[...], g_ref[...]) * (1.0 + sc_ref[0]) + sh_ref[0]
    z = _dot(h.astype(BF16), w_ref[...])
    c64, s64, c32, s32 = c64_ref[...], s64_ref[...], c32_ref[...], s32_ref[...]
    tm = z.shape[0]

    cq = _rms(z[:, dm.o_zq:dm.o_zq + dm.q_lora], qg_ref[...]).astype(BF16)
    qlat = _dot(cq, wabs_ref[...]) * mla_scale
    qrope = _rope_cols(_dot(cq, wqr_ref[...]), c32, s32, dm.mla_rope // 2) * mla_scale
    for hd in range(dm.mla_heads):
        qcat_ref[0, hd, :, 0:dm.kv_lora] = qlat[:, hd * dm.kv_lora:(hd + 1) * dm.kv_lora].astype(BF16)
        qcat_ref[0, hd, :, dm.kv_lora:dm.mla_row] = qrope[:, hd * dm.mla_rope:(hd + 1) * dm.mla_rope].astype(BF16)

    ckv = _rms(z[:, dm.o_ckv:dm.o_ckv + dm.kv_lora], kvg_ref[...])
    kr = _rope_cols(z[:, dm.o_kr:dm.o_kr + LANES], c32, s32, dm.mla_rope // 2)[:, 0:dm.mla_rope]
    row_ref[:, 0:dm.kv_lora] = ckv
    row_ref[:, dm.kv_lora:dm.mla_row] = kr
    kv16_ref[:, 0:dm.kv_lora] = ckv.astype(BF16)
    kv16_ref[:, dm.kv_lora:dm.mla_row] = kr.astype(BF16)

    hd_ = dm.dsa_hd
    dq = _rope_cols(z[:, dm.o_dq:dm.o_dk], c64, s64, hd_ // 2) * dsa_scale
    for hh in range(dm.dsa_heads):
        dq_ref[0, hh] = dq[:, hh * hd_:(hh + 1) * hd_].astype(BF16)
    dk = _rope_cols(z[:, dm.o_dk:dm.o_dv], c64, s64, hd_ // 2)
    dv = z[:, dm.o_dv:dm.o_iq]
    dvt = dv.T
    if len(dsarow_ref.shape) == 3:
        dsarow_ref[0, 0:LANES, :] = dk.T
        dsarow_ref[0, LANES:2 * LANES, :] = dvt
    else:
        dsarow_ref[:, 0:LANES] = dk
        dsarow_ref[:, LANES:2 * LANES] = dv
    for gg in range(dm.dsa_kv_heads):
        dk_ref[0, gg] = dk[:, gg * hd_:(gg + 1) * hd_].astype(BF16)
    vck = dvt_ref.shape[3]
    for c in range(tm // vck):
        dvt_ref[0, c] = dvt[:, c * vck:(c + 1) * vck].astype(BF16)

    iq = _rope_cols(z[:, dm.o_iq:dm.o_ik], c64, s64, dm.idx_dim // 2) * idx_scale
    for hh in range(dm.idx_heads):
        iq_ref[0, hh] = iq[:, hh * dm.idx_dim:(hh + 1) * dm.idx_dim].astype(BF16)
    last = z[:, dm.o_ik:dm.o_ik + LANES]
    ik = _rope_cols(last, c64, s64, dm.idx_dim // 2)[:, 0:dm.idx_dim]
    ik_ref[...] = ik
    ik16_ref[...] = ik.astype(BF16)
    iwt_ref[0] = last.T[dm.idx_dim:dm.idx_dim + 8, :] * idx_w_scale


def _even_project(x2d, sc, sh, g, w_in, qg, kvg, wqr, wabs, tabs, dm, bq, tq, tm, per_row, scales):
    rows_t = not per_row
    m, d = x2d.shape
    nt = tq // tm
    vck = min(DSA_KEY_CHUNK, tm)
    nh, kvh, ih = dm.mla_heads, dm.dsa_kv_heads, dm.idx_heads
    row2 = lambda i: (i, 0)
    hm4 = lambda i: (i // nt, 0, i % nt, 0)
    if per_row:
        mod_spec = pl.BlockSpec((1, tm, d), lambda i: (0, i, 0))
    else:
        mod_spec = pl.BlockSpec((1, 1, d), lambda i: (i // nt, 0, 0))
    const = lambda shape: pl.BlockSpec(shape, lambda i: tuple(0 for _ in shape))
    tab_spec = pl.BlockSpec((tm, LANES), lambda i: (i % nt, 0))
    out_shapes = (
        jax.ShapeDtypeStruct((bq, nh, tq, dm.mla_row), BF16),
        jax.ShapeDtypeStruct((m, dm.mla_row), F32),
        jax.ShapeDtypeStruct((m, dm.mla_row), BF16),
        (jax.ShapeDtypeStruct((bq, 2 * LANES, tq), F32) if rows_t
         else jax.ShapeDtypeStruct((m, 2 * LANES), F32)),
        jax.ShapeDtypeStruct((bq, dm.dsa_heads, tq, dm.dsa_hd), BF16),
        jax.ShapeDtypeStruct((bq, kvh, tq, dm.dsa_hd), BF16),
        jax.ShapeDtypeStruct((bq, tq // vck, LANES, vck), BF16),
        jax.ShapeDtypeStruct((bq, ih, tq, dm.idx_dim), BF16),
        jax.ShapeDtypeStruct((m, dm.idx_dim), F32),
        jax.ShapeDtypeStruct((m, dm.idx_dim), BF16),
        jax.ShapeDtypeStruct((bq, 8, tq), F32),
    )
    out_specs = (
        pl.BlockSpec((1, nh, tm, dm.mla_row), hm4),
        pl.BlockSpec((tm, dm.mla_row), row2),
        pl.BlockSpec((tm, dm.mla_row), row2),
        (pl.BlockSpec((1, 2 * LANES, tm), lambda i: (i // nt, 0, i % nt)) if rows_t
         else pl.BlockSpec((tm, 2 * LANES), row2)),
        pl.BlockSpec((1, dm.dsa_heads, tm, dm.dsa_hd), hm4),
        pl.BlockSpec((1, kvh, tm, dm.dsa_hd), hm4),
        pl.BlockSpec((1, tm // vck, LANES, vck), lambda i: (i // nt, i % nt, 0, 0)),
        pl.BlockSpec((1, ih, tm, dm.idx_dim), hm4),
        pl.BlockSpec((tm, dm.idx_dim), row2),
        pl.BlockSpec((tm, dm.idx_dim), row2),
        pl.BlockSpec((1, 8, tm), lambda i: (i // nt, 0, i % nt)),
    )
    kern = functools.partial(_even_kernel, dm=dm, mla_scale=scales[0], dsa_scale=scales[1],
                             idx_scale=scales[2], idx_w_scale=scales[3])
    return pl.pallas_call(
        kern,
        grid=(m // tm,),
        in_specs=[pl.BlockSpec((tm, d), row2), mod_spec, mod_spec, const((1, d)), const(w_in.shape),
                  const(qg.shape), const(kvg.shape), const(wqr.shape), const(wabs.shape),
                  tab_spec, tab_spec, tab_spec, tab_spec],
        out_specs=out_specs,
        out_shape=out_shapes,
        compiler_params=_cp(("parallel",), 56),
        name="even_project",
    )(x2d, sc, sh, g, w_in, qg, kvg, wqr, wabs, *tabs)


def _fold_lanes(x, op):
    out = x[:, 0:LANES]
    for c in range(1, x.shape[1] // LANES):
        out = op(out, x[:, c * LANES:(c + 1) * LANES])
    return out


def _mla_kernel(q_ref, kv_ref, wuv_ref, o_ref, s_ref, mp_ref, lp_ref, acc_ref, *, tq, tk, nh, r, vd):
    qi = pl.program_id(1)
    n = nh * tq
    q = q_ref[0].reshape(n, q_ref.shape[-1])
    nch = (qi * tq + tq + tk - 1) // tk
    reps = tk // LANES

    def keys(j):
        return kv_ref[0, pl.ds(pl.multiple_of(j * tk, tk), tk), :]

    mp_ref[...] = jnp.full(mp_ref.shape, NEG, F32)

    def pass_a(j, c):
        s = lax.dot_general(q, keys(j), NT, preferred_element_type=F32)
        s_ref[j] = s
        mp_ref[...] = jnp.maximum(mp_ref[...], _fold_lanes(s, jnp.maximum))
        return c

    lax.fori_loop(0, nch - 1, pass_a, 0)
    jl = nch - 1
    s = lax.dot_general(q, keys(jl), NT, preferred_element_type=F32)
    qpos = qi * tq + lax.broadcasted_iota(I32, s.shape, 0) % tq
    kpos = jl * tk + lax.broadcasted_iota(I32, s.shape, 1)
    s = jnp.where(kpos <= qpos, s, NEG)
    s_ref[jl] = s
    mp = jnp.maximum(mp_ref[...], _fold_lanes(s, jnp.maximum))
    mb = jnp.broadcast_to(jnp.max(mp, axis=1, keepdims=True), (n, LANES))
    mbt = jnp.concatenate([mb] * reps, axis=1)

    lp_ref[...] = jnp.zeros(lp_ref.shape, F32)
    acc_ref[...] = jnp.zeros(acc_ref.shape, F32)

    def pass_b(j, c):
        p = jnp.exp2(s_ref[j] - mbt)
        lp_ref[...] += _fold_lanes(p, jnp.add)
        acc_ref[...] += _dot(p.astype(BF16), keys(j)[:, 0:r])
        return c

    lax.fori_loop(0, nch, pass_b, 0)
    lat = (acc_ref[...] / jnp.sum(lp_ref[...], axis=1, keepdims=True)).astype(BF16)
    for hd in range(nh):
        o_ref[0, :, hd * vd:(hd + 1) * vd] = _dot(lat[hd * tq:(hd + 1) * tq], wuv_ref[hd]).astype(BF16)


def _mla_prompt(qcat, kv16, wuv, tq):
    b, nh, t, dq = qcat.shape
    r, vd = wuv.shape[1], wuv.shape[2]
    tk = _pick(t, (512, 256, 128))
    n = nh * tq
    kern = functools.partial(_mla_kernel, tq=tq, tk=tk, nh=nh, r=r, vd=vd)
    return pl.pallas_call(
        kern,
        grid=(b, t // tq),
        in_specs=[pl.BlockSpec((1, nh, tq, dq), lambda bi, qi: (bi, 0, qi, 0)),
                  pl.BlockSpec((1, t, dq), lambda bi, qi: (bi, 0, 0)),
                  pl.BlockSpec(wuv.shape, lambda bi, qi: (0, 0, 0))],
        out_specs=pl.BlockSpec((1, tq, nh * vd), lambda bi, qi: (bi, qi, 0)),
        out_shape=jax.ShapeDtypeStruct((b, t, nh * vd), BF16),
        scratch_shapes=[pltpu.VMEM((t // tk, n, tk), F32), pltpu.VMEM((n, LANES), F32),
                        pltpu.VMEM((n, LANES), F32), pltpu.VMEM((n, r), F32)],
        compiler_params=_cp(("parallel", "arbitrary")),
        name="mla_prompt",
    )(qcat, kv16, wuv)


def _sort_key(score):
    b = pltpu.bitcast(score, I32)
    return jnp.where(b < 0, (b ^ jnp.int32(0x7FFFFFFF)) + 1, b)


def _topk_select(read_keys, read_pos, nch, rows, width, nsel, pos_bits):
    def count(pred):
        def body(j, c):
            return c + jnp.sum(pred(j).astype(I32).reshape(rows // 8, 8, width), axis=0)
        c = lax.fori_loop(0, nch, body, jnp.zeros((8, width), I32))
        return jnp.sum(c, axis=0, keepdims=True)

    def bit_body(b, thr):
        cand = thr + lax.shift_left(jnp.int32(1), 31 - b)
        cnt = count(lambda j: read_keys(j) >= cand)
        return jnp.where(cnt >= nsel, cand, thr)

    thr = lax.fori_loop(0, 32, bit_body, jnp.full((1, width), INT_MIN, I32))

    def tie_search():
        need = nsel - count(lambda j: read_keys(j) > thr)

        def jbit_body(b, jc):
            cand = jc + lax.shift_left(jnp.int32(1), pos_bits - 1 - b)
            cnt = count(lambda j: (read_keys(j) == thr) & (read_pos(j) < cand))
            return jnp.where(cnt <= need, cand, jc)

        return lax.fori_loop(0, pos_bits, jbit_body, jnp.zeros((1, width), I32))

    excess = (count(lambda j: read_keys(j) >= thr) > nsel) & (thr > INT_MIN)
    any_excess = jnp.max(excess.astype(I32)) > 0
    jcut = lax.cond(any_excess, tie_search, lambda: jnp.full((1, width), 2 ** pos_bits, I32))
    return thr, jcut


def _selected(keys, pos, thr, jcut):
    return ((keys > thr) | ((keys == thr) & (pos < jcut))) & (keys != INT_MIN)


def _dsa_kernel(iq_ref, iwt_ref, ik_ref, dq_ref, dk_ref, dvt_ref, o_ref, keys_ref, bias_ref, s_ref,
                *, tq, ck, nsel, ih, gsz, kvh, hd, pos_bits):
    i = pl.program_id(1)
    nch = ((i + 1) * tq + ck - 1) // ck
    row = lax.broadcasted_iota(I32, (ck, tq), 0)
    qpos = i * tq + lax.broadcasted_iota(I32, (ck, tq), 1)

    def off(j):
        return pl.multiple_of(j * ck, ck)

    def score_chunk(j, carry):
        ikc = ik_ref[0, pl.ds(off(j), ck), :]
        acc = jnp.zeros((ck, tq), F32)
        for hh in range(ih):
            d = lax.dot_general(ikc, iq_ref[0, hh], NT, preferred_element_type=F32)
            acc = acc + iwt_ref[0, hh:hh + 1, :] * jnp.maximum(d, 0.0)
        valid = (j * ck + row) <= qpos
        keys_ref[pl.ds(off(j), ck), :] = jnp.where(valid, _sort_key(acc), INT_MIN)
        return carry

    lax.fori_loop(0, nch, score_chunk, 0)

    read_keys = lambda j: keys_ref[pl.ds(off(j), ck), :]
    read_pos = lambda j: j * ck + row
    thr, jcut = _topk_select(read_keys, read_pos, nch, ck, tq, nsel, pos_bits)

    def bias_chunk(j, carry):
        sel = _selected(read_keys(j), read_pos(j), thr, jcut)
        bias_ref[pl.ds(off(j), ck), :] = jnp.where(sel, 0.0, NEG)
        return carry

    lax.fori_loop(0, nch, bias_chunk, 0)

    qs = [dq_ref[0, g * gsz:(g + 1) * gsz].reshape(gsz * tq, hd) for g in range(kvh)]
    wq = gsz * tq

    def pass_a(j, ms):
        b = bias_ref[pl.ds(off(j), ck), :]
        bt = jnp.concatenate([b] * gsz, axis=1)
        out = []
        for g in range(kvh):
            s = lax.dot_general(dk_ref[0, g, pl.ds(off(j), ck), :], qs[g], NT, preferred_element_type=F32) + bt
            s_ref[g, pl.ds(off(j), ck), :] = s
            out.append(jnp.maximum(ms[g], jnp.max(s, axis=0, keepdims=True)))
        return tuple(out)

    ms = lax.fori_loop(0, nch, pass_a, tuple(jnp.full((1, wq), NEG, F32) for _ in range(kvh)))

    def pass_b(j, carry):
        out = []
        for g in range(kvh):
            l, acc = carry[g]
            p = jnp.exp2(s_ref[g, pl.ds(off(j), ck), :] - ms[g])
            l = l + jnp.sum(p, axis=0, keepdims=True)
            acc = acc + _dot(dvt_ref[0, j, g * hd:(g + 1) * hd, :], p.astype(BF16))
            out.append((l, acc))
        return tuple(out)

    res = lax.fori_loop(0, nch, pass_b,
                        tuple((jnp.zeros((1, wq), F32), jnp.zeros((hd, wq), F32)) for _ in range(kvh)))
    outs = []
    for g in range(kvh):
        ot = res[g][1] / res[g][0]
        for hh in range(gsz):
            outs.append(ot[:, hh * tq:(hh + 1) * tq])
    o_ref[0] = jnp.concatenate(outs, axis=0).T.astype(BF16)


def _dsa_prompt(iq, iwt, ik16, dq, dk, dvt, nsel, tq):
    b, ih, t, idim = iq.shape
    nhd, hd = dq.shape[1], dq.shape[3]
    kvh = dk.shape[1]
    gsz = nhd // kvh
    ck = dvt.shape[3]
    ik3 = ik16.reshape(b, t, idim)
    pos_bits = int(t).bit_length()
    kern = functools.partial(_dsa_kernel, tq=tq, ck=ck, nsel=nsel, ih=ih, gsz=gsz, kvh=kvh, hd=hd,
                             pos_bits=pos_bits)
    return pl.pallas_call(
        kern,
        grid=(b, t // tq),
        in_specs=[pl.BlockSpec((1, ih, tq, idim), lambda bi, i: (bi, 0, i, 0)),
                  pl.BlockSpec((1, 8, tq), lambda bi, i: (bi, 0, i)),
                  pl.BlockSpec((1, t, idim), lambda bi, i: (bi, 0, 0)),
                  pl.BlockSpec((1, nhd, tq, hd), lambda bi, i: (bi, 0, i, 0)),
                  pl.BlockSpec((1, kvh, t, hd), lambda bi, i: (bi, 0, 0, 0)),
                  pl.BlockSpec((1, t // ck, 2 * hd, ck), lambda bi, i: (bi, 0, 0, 0))],
        out_specs=pl.BlockSpec((1, tq, nhd * hd), lambda bi, i: (bi, i, 0)),
        out_shape=jax.ShapeDtypeStruct((b, t, nhd * hd), BF16),
        scratch_shapes=[pltpu.VMEM((t, tq), I32), pltpu.VMEM((t, tq), F32),
                        pltpu.VMEM((kvh, t, gsz * tq), F32)],
        compiler_params=_cp(("parallel", "arbitrary")),
        name="dsa_prompt",
    )(iq, iwt, ik3, dq, dk, dvt)


def _ffn_kernel(*refs, n_mix, final, splits):
    x_ref = refs[0]
    mix_refs = refs[1:1 + 2 * n_mix]
    (g1_ref, nfg_ref, sc_ref, sh_ref, g2_ref, wg_ref, wu_ref, wd_ref, fg_ref, o_ref) = refs[1 + 2 * n_mix:]
    mix = _dot(mix_refs[0][...], mix_refs[1][...])
    for k in range(1, n_mix):
        mix = mix + _dot(mix_refs[2 * k][...], mix_refs[2 * k + 1][...])
    x1 = x_ref[...] + g1_ref[0] * mix
    hb = (_rms(x1, nfg_ref[...]) * (1.0 + sc_ref[0]) + sh_ref[0]).astype(BF16)
    y = None
    for lo, hi in splits:
        act = _silu(_dot(hb, wg_ref[:, lo:hi])) * _dot(hb, wu_ref[:, lo:hi])
        part = _dot(act.astype(BF16), wd_ref[lo:hi, :])
        y = part if y is None else y + part
    y = x1 + g2_ref[0] * y
    if final:
        y = _rms(y, fg_ref[...])
    o_ref[...] = y


MXU_TILE = 256


def _mix_ffn(x2d, mixes, g1, nfg, sc2, sh2, g2, wg, wu, wd, fg, tq, tm, per_row, final):
    m, d = x2d.shape
    dff = wg.shape[1]
    nt = tq // tm
    half = (dff // MXU_TILE + 1) // 2 * MXU_TILE
    splits = ((0, half), (half, dff)) if 0 < half < dff else ((0, dff),)
    if per_row:
        mod_spec = pl.BlockSpec((1, tm, d), lambda i: (0, i, 0))
    else:
        mod_spec = pl.BlockSpec((1, 1, d), lambda i: (i // nt, 0, 0))
    row = lambda i: (i, 0)
    const2 = lambda shape: pl.BlockSpec(shape, lambda i: (0, 0), pipeline_mode=pl.Buffered(1))
    in_specs = [pl.BlockSpec((tm, d), row)]
    args = [x2d]
    for a, w in mixes:
        in_specs += [pl.BlockSpec((tm, a.shape[1]), row), const2(w.shape)]
        args += [a, w]
    in_specs += [mod_spec, const2((1, d)), mod_spec, mod_spec, mod_spec,
                 const2(wg.shape), const2(wu.shape), const2(wd.shape), const2((1, d))]
    args += [g1, nfg, sc2, sh2, g2, wg, wu, wd, fg]
    kern = functools.partial(_ffn_kernel, n_mix=len(mixes), final=final, splits=splits)
    return pl.pallas_call(
        kern,
        grid=(m // tm,),
        in_specs=in_specs,
        out_specs=pl.BlockSpec((tm, d), row),
        out_shape=jax.ShapeDtypeStruct((m, d), F32),
        compiler_params=_cp(("parallel",), 60),
        name="mix_ffn",
    )(*args)


def _normmod_mm_kernel(x_ref, sc_ref, sh_ref, g_ref, w_ref, o_ref):
    h = _rms(x_ref[...], g_ref[...]) * (1.0 + sc_ref[0]) + sh_ref[0]
    o_ref[...] = _dot(h.astype(BF16), w_ref[...])


def _normmod_mm(x2d, sc, sh, g, w, tq, tm, tn, per_row):
    m, d = x2d.shape
    n = w.shape[1]
    nt = tq // tm
    if per_row:
        mod_spec = pl.BlockSpec((1, tm, d), lambda i, j: (0, i, 0))
    else:
        mod_spec = pl.BlockSpec((1, 1, d), lambda i, j: (i // nt, 0, 0))
    return pl.pallas_call(
        _normmod_mm_kernel,
        grid=(m // tm, n // tn),
        in_specs=[pl.BlockSpec((tm, d), lambda i, j: (i, 0)), mod_spec, mod_spec,
                  pl.BlockSpec((1, d), lambda i, j: (0, 0)), pl.BlockSpec((d, tn), lambda i, j: (0, j))],
        out_specs=pl.BlockSpec((tm, tn), lambda i, j: (i, j)),
        out_shape=jax.ShapeDtypeStruct((m, n), F32),
        compiler_params=_cp(("parallel", "parallel")),
        name="odd_in_proj",
    )(x2d, sc, sh, g, w)


def _hgrn_consts(c):
    nlev = int(np.log2(c))
    t = np.arange(c)
    tril = (t[None, :] <= t[:, None]).astype(np.float32)
    mats = [tril]
    masks = [np.eye(c, dtype=np.float32)]
    for lev in range(1, nlev + 1):
        w = 2 ** (lev - 1)
        piv = (t // (2 * w)) * (2 * w) + w - 1
        mats.append(tril - tril[piv])
        same = (t[:, None] // (2 * w)) == (t[None, :] // (2 * w))
        right = ((t // w) % 2 == 1)[:, None]
        left = ((t // w) % 2 == 0)[None, :]
        masks.append((same & right & left).astype(np.float32))
    return np.concatenate(mats, axis=0), np.stack(masks), nlev


def _split2(x):
    a = x.astype(BF16)
    b = (x - a.astype(F32)).astype(BF16)
    return a, b


def _hgrn_kernel(zq_ref, zf_ref, zi_ref, zg_ref, lb_ref, og_ref, cm_ref, bm_ref, o_ref, st_ref, s_ref,
                 *, c, nlev, q_scale, hpb, kk, vd):
    ci = pl.program_id(2)

    @pl.when(ci == 0)
    def _():
        s_ref[...] = jnp.zeros(s_ref.shape, F32)

    for hh in range(hpb):
        ks = slice(hh * kk, (hh + 1) * kk)
        vs = slice(hh * vd, (hh + 1) * vd)
        lb = lb_ref[hh]
        zf = zf_ref[:, ks]
        e = jnp.exp(-jnp.abs(zf))
        inv = 1.0 / (1.0 + e)
        sig_pos = jnp.where(zf >= 0, inv, e * inv)
        sig_neg = jnp.where(zf >= 0, e * inv, inv)
        logf = jnp.log(lb + (1.0 - lb) * sig_pos)
        k = (1.0 - lb) * sig_neg
        q = _silu(zq_ref[:, ks]) * q_scale
        v = zi_ref[:, vs].astype(BF16)

        a2 = jnp.concatenate(_split2(logf), axis=1)
        gd = _dot(cm_ref[...], a2)
        gd = gd[:, 0:kk] + gd[:, kk:2 * kk]
        gcum = gd[0:c]
        glast = gcum[c - 1:c]

        amat = bm_ref[0] * lax.dot_general(q.astype(BF16), k.astype(BF16), NT, preferred_element_type=F32)
        for lev in range(1, nlev + 1):
            ed = jnp.exp(-jnp.abs(gd[lev * c:(lev + 1) * c]))
            amat = amat + bm_ref[lev] * lax.dot_general((q * ed).astype(BF16), (k * ed).astype(BF16), NT,
                                                        preferred_element_type=F32)
        s_prev = s_ref[hh]
        o = _dot(amat.astype(BF16), v) + _dot((q * jnp.exp(gcum)).astype(BF16), s_prev.astype(BF16))
        kdt = (k * jnp.exp(glast - gcum)).T.astype(BF16)
        decay_col = jnp.broadcast_to(jnp.exp(glast), (kk, kk)).T
        s_new = decay_col * s_prev + _dot(kdt, v)
        s_ref[hh] = s_new
        o_ref[:, vs] = (_rms(o, og_ref[...]) * _silu(zg_ref[:, vs])).astype(BF16)

    @pl.when(ci == pl.num_programs(2) - 1)
    def _():
        st_ref[...] = s_ref[...]


def _hgrn_prompt(z, lb, og, b, t, nh, c, q_scale):
    m = z.shape[0]
    kk = lb.shape[-1]
    vd = og.shape[-1]
    nc = t // c
    hpb = _pick(nh, (4, 2, 1))
    ng = nh // hpb
    cm, bm, nlev = _hgrn_consts(c)
    cm = jnp.asarray(cm, BF16)
    bm = jnp.asarray(bm, F32)
    seg = lambda s, w: pl.BlockSpec((c, hpb * w), lambda bi, h, ci: (bi * nc + ci, s * ng + h))
    kern = functools.partial(_hgrn_kernel, c=c, nlev=nlev, q_scale=q_scale, hpb=hpb, kk=kk, vd=vd)
    return pl.pallas_call(
        kern,
        grid=(b, ng, nc),
        in_specs=[seg(0, kk), seg(1, kk), seg(2, vd), seg(3, vd),
                  pl.BlockSpec((hpb, 1, kk), lambda bi, h, ci: (h, 0, 0)),
                  pl.BlockSpec((1, vd), lambda bi, h, ci: (0, 0)),
                  pl.BlockSpec(cm.shape, lambda bi, h, ci: (0, 0)),
                  pl.BlockSpec(bm.shape, lambda bi, h, ci: (0, 0, 0))],
        out_specs=(pl.BlockSpec((c, hpb * vd), lambda bi, h, ci: (bi * nc + ci, h)),
                   pl.BlockSpec((None, hpb, kk, vd), lambda bi, h, ci: (bi, h, 0, 0))),
        out_shape=(jax.ShapeDtypeStruct((m, nh * vd), BF16),
                   jax.ShapeDtypeStruct((b, nh, kk, vd), F32)),
        scratch_shapes=[pltpu.VMEM((hpb, kk, vd), F32)],
        compiler_params=_cp(("parallel", "parallel", "arbitrary")),
        name="hgrn_prompt",
    )(z, z, z, z, lb, og, cm, bm)


def _hgrn_step_kernel(z_ref, lb_ref, og_ref, s_ref, o_ref, st_ref, *, nh, kk, vd, bt, q_scale):
    z = z_ref[...]
    hf = nh * kk
    for h in range(nh):
        lb = lb_ref[h]
        zf = z[:, hf + h * kk:hf + (h + 1) * kk]
        e = jnp.exp(-jnp.abs(zf))
        inv = 1.0 / (1.0 + e)
        f = lb + (1.0 - lb) * jnp.where(zf >= 0, inv, e * inv)
        k = (1.0 - lb) * jnp.where(zf >= 0, e * inv, inv)
        q = _silu(z[:, h * kk:(h + 1) * kk]) * q_scale
        v = z[:, 2 * hf + h * vd:2 * hf + (h + 1) * vd]
        gate = _silu(z[:, 2 * hf + nh * vd + h * vd:2 * hf + nh * vd + (h + 1) * vd])
        ft, kt, qt = f.T, k.T, q.T
        for j in range(bt):
            s_new = ft[:, j:j + 1] * s_ref[j, h] + kt[:, j:j + 1] * v[j:j + 1, :]
            st_ref[j, h] = s_new
            o = jnp.sum(qt[:, j:j + 1] * s_new, axis=0, keepdims=True)
            o_ref[j:j + 1, h * vd:(h + 1) * vd] = (_rms(o, og_ref[...]) * gate[j:j + 1, :]).astype(BF16)


def _hgrn_step(z, lb, og, state, nh, q_scale):
    bsz = z.shape[0]
    kk, vd = state.shape[2], state.shape[3]
    bt = 8
    kern = functools.partial(_hgrn_step_kernel, nh=nh, kk=kk, vd=vd, bt=bt, q_scale=q_scale)
    return pl.pallas_call(
        kern,
        grid=(bsz // bt,),
        in_specs=[pl.BlockSpec((bt, z.shape[1]), lambda i: (i, 0)),
                  pl.BlockSpec(lb.shape, lambda i: (0, 0, 0)),
                  pl.BlockSpec((1, vd), lambda i: (0, 0)),
                  pl.BlockSpec((bt, nh, kk, vd), lambda i: (i, 0, 0, 0))],
        out_specs=(pl.BlockSpec((bt, nh * vd), lambda i: (i, 0)),
                   pl.BlockSpec((bt, nh, kk, vd), lambda i: (i, 0, 0, 0))),
        out_shape=(jax.ShapeDtypeStruct((bsz, nh * vd), BF16),
                   jax.ShapeDtypeStruct(state.shape, F32)),
        compiler_params=_cp(("parallel",)),
        name="hgrn_step",
    )(z, lb, og, state)


def _stage_pages(pages):
    return jnp.concatenate([pg[...].astype(BF16) for pg in pages], axis=1)


def _paged_softmax_pv(s_ref, pg_ref, s_new, new_v, v_rows):
    ng, nh, w = s_ref.shape
    mp = s_ref[0]
    for gi in range(1, ng):
        mp = jnp.maximum(mp, s_ref[gi])
    m = jnp.maximum(jnp.max(_fold_lanes(mp, jnp.maximum), axis=1, keepdims=True),
                    jnp.max(s_new, axis=1, keepdims=True))
    mbt = jnp.concatenate([jnp.broadcast_to(m, (nh, LANES))] * (w // LANES), axis=1)
    lp = jnp.zeros((nh, LANES), F32)
    acc = None
    for gi in range(ng):
        p = jnp.exp2(s_ref[gi] - mbt)
        lp = lp + _fold_lanes(p, jnp.add)
        pv = lax.dot_general(p.astype(BF16), pg_ref[gi, v_rows, :], NT, preferred_element_type=F32)
        acc = pv if acc is None else acc + pv
    p_new = jnp.exp2(s_new - m)
    l = jnp.sum(lp, axis=1, keepdims=True) + jnp.sum(p_new, axis=1, keepdims=True)
    return (acc + _dot(p_new.astype(BF16), new_v)) / l


def _mla_dec2_kernel(pt_ref, q_ref, new_ref, wuv_ref, *rest, pp, r, vd, nh):
    pages = rest[:pp]
    o_ref, pg_ref, s_ref = rest[pp:]
    g = pl.program_id(1)
    q = q_ref[...]
    slab = _stage_pages(pages)
    pg_ref[g] = slab
    s_ref[g] = _dot(q, slab)

    @pl.when(g == pl.num_programs(1) - 1)
    def _():
        new = new_ref[...]
        s_new = lax.dot_general(q, new, NT, preferred_element_type=F32)
        s_new = jnp.where(lax.broadcasted_iota(I32, s_new.shape, 1) == 0, s_new, NEG)
        lat = _paged_softmax_pv(s_ref, pg_ref, s_new, new[:, 0:r], slice(0, r)).astype(BF16)
        for hd in range(nh):
            o_ref[0:1, hd * vd:(hd + 1) * vd] = _dot(lat[hd:hd + 1], wuv_ref[hd])
        o_ref[1:8, :] = jnp.zeros((7, nh * vd), F32)


def _mla_decode2(page_table, qdec, newrow, wuv, cache, pp):
    bsz, nh, dq = qdec.shape
    n_pages = page_table.shape[1]
    psz = cache.shape[2]
    r, vd = wuv.shape[1], wuv.shape[2]
    ng = n_pages // pp
    pt = page_table.reshape(-1)

    def page_map(k):
        return lambda b, g, ptr: (ptr[b * n_pages + g * pp + k], 0, 0)

    kern = functools.partial(_mla_dec2_kernel, pp=pp, r=r, vd=vd, nh=nh)
    grid_spec = pltpu.PrefetchScalarGridSpec(
        num_scalar_prefetch=1,
        grid=(bsz, n_pages // pp),
        in_specs=[pl.BlockSpec((None, nh, dq), lambda b, g, ptr: (b, 0, 0)),
                  pl.BlockSpec((None, 8, dq), lambda b, g, ptr: (b, 0, 0)),
                  pl.BlockSpec(wuv.shape, lambda b, g, ptr: (0, 0, 0))]
                 + [pl.BlockSpec((None, dq, psz), page_map(k)) for k in range(pp)],
        out_specs=pl.BlockSpec((None, 8, nh * vd), lambda b, g, ptr: (b, 0, 0)),
        scratch_shapes=[pltpu.VMEM((ng, dq, pp * psz), BF16), pltpu.VMEM((ng, nh, pp * psz), F32)],
    )
    out = pl.pallas_call(
        kern,
        grid_spec=grid_spec,
        out_shape=jax.ShapeDtypeStruct((bsz, 8, nh * vd), F32),
        compiler_params=_cp(("parallel", "arbitrary")),
        name="mla_decode",
    )(pt, qdec, newrow, wuv, *([cache] * pp))
    return out[:, 0, :]


def _idx_score_kernel(pt_ref, iq_ref, iw_ref, *rest, pp):
    pages = rest[:pp]
    keys_ref = rest[pp]
    iq = iq_ref[...]
    iw = iw_ref[...]
    d = _dot(iq, _stage_pages(pages))
    keys_ref[...] = _sort_key(jnp.sum(iw * jnp.maximum(d, 0.0), axis=0, keepdims=True))


def _idx_select_kernel(keys_ref, iq_ref, iw_ref, iknew_ref, bias_ref, bnew_ref, *, eb, nsel, pos_bits):
    ng, w = keys_ref.shape[1], keys_ref.shape[2]
    pos = lax.broadcasted_iota(I32, (ng, w), 0) * w + lax.broadcasted_iota(I32, (ng, w), 1)
    knew = []
    for e in range(eb):
        d = lax.dot_general(iq_ref[e], iknew_ref[e], NT, preferred_element_type=F32)
        snew = _sort_key(jnp.sum(iw_ref[e] * jnp.maximum(d, 0.0), axis=0, keepdims=True))
        knew.append(snew[0:1, 0:1])

    def count(pred):
        return jnp.sum(jnp.sum(pred.astype(I32), axis=0, keepdims=True), axis=1, keepdims=True)

    def bit_body(b, thrs):
        out = []
        for e in range(eb):
            cand = thrs[e] + lax.shift_left(jnp.int32(1), 31 - b)
            cnt = count(keys_ref[e] >= cand) + (knew[e] >= cand).astype(I32)
            out.append(jnp.where(cnt >= nsel, cand, thrs[e]))
        return tuple(out)

    thrs = lax.fori_loop(0, 32, bit_body, tuple(jnp.full((1, 1), INT_MIN, I32) for _ in range(eb)))
    needs = tuple(nsel - count(keys_ref[e] > thrs[e]) - (knew[e] > thrs[e]).astype(I32) for e in range(eb))

    def jbit_body(b, jcs):
        out = []
        for e in range(eb):
            cand = jcs[e] + lax.shift_left(jnp.int32(1), pos_bits - 1 - b)
            cnt = count((keys_ref[e] == thrs[e]) & (pos < cand))
            out.append(jnp.where(cnt <= needs[e], cand, jcs[e]))
        return tuple(out)

    jcs = lax.fori_loop(0, pos_bits, jbit_body, tuple(jnp.zeros((1, 1), I32) for _ in range(eb)))
    for e in range(eb):
        bias_ref[e] = jnp.where(_selected(keys_ref[e], pos, thrs[e], jcs[e]), 0.0, NEG)
        ties_taken = count((keys_ref[e] == thrs[e]) & (pos < jcs[e]))
        new_sel = (knew[e] > thrs[e]) | ((knew[e] == thrs[e]) & (ties_taken < needs[e]))
        slot0 = (lax.broadcasted_iota(I32, (8, LANES), 0) == 0) & (lax.broadcasted_iota(I32, (8, LANES), 1) == 0)
        bnew_ref[e] = jnp.where(slot0 & jnp.broadcast_to(new_sel, (8, LANES)), 0.0, NEG)


def _idx_decode2(page_table, iqd, iwd, iknew, cache, nsel, pp):
    bsz, _, idim = iqd.shape
    n_pages = page_table.shape[1]
    psz = cache.shape[2]
    ng, w = n_pages // pp, pp * psz
    pos_bits = int(n_pages * psz).bit_length()
    pt = page_table.reshape(-1)

    def page_map(k):
        return lambda b, g, ptr: (ptr[b * n_pages + g * pp + k], 0, 0)

    grid_spec = pltpu.PrefetchScalarGridSpec(
        num_scalar_prefetch=1,
        grid=(bsz, n_pages // pp),
        in_specs=[pl.BlockSpec((None, 8, idim), lambda b, g, ptr: (b, 0, 0)),
                  pl.BlockSpec((None, 8, 1), lambda b, g, ptr: (b, 0, 0))]
                 + [pl.BlockSpec((None, idim, psz), page_map(k)) for k in range(pp)],
        out_specs=pl.BlockSpec((None, None, 1, w), lambda b, g, ptr: (b, g, 0, 0)),
    )
    keys = pl.pallas_call(
        functools.partial(_idx_score_kernel, pp=pp),
        grid_spec=grid_spec,
        out_shape=jax.ShapeDtypeStruct((bsz, ng, 1, w), I32),
        compiler_params=_cp(("parallel", "parallel")),
        name="idx_score",
    )(pt, iqd, iwd, *([cache] * pp))

    eb = _pick(bsz, (8, 4, 2, 1))
    blk = lambda *shape: pl.BlockSpec((eb,) + shape, lambda i: (i,) + tuple(0 for _ in shape))
    return pl.pallas_call(
        functools.partial(_idx_select_kernel, eb=eb, nsel=nsel, pos_bits=pos_bits),
        grid=(bsz // eb,),
        in_specs=[blk(ng, w), blk(8, idim), blk(8, 1), blk(8, idim)],
        out_specs=(blk(ng, w), blk(8, LANES)),
        out_shape=(jax.ShapeDtypeStruct((bsz, ng, w), F32), jax.ShapeDtypeStruct((bsz, 8, LANES), F32)),
        compiler_params=_cp(("parallel",)),
        name="idx_select",
    )(keys.reshape(bsz, ng, w), iqd, iwd, iknew)


def _dsa_dec2_kernel(pt_ref, q_ref, new_ref, bias_ref, bnew_ref, *rest, pp, hd, gsz):
    pages = rest[:pp]
    o_ref, pg_ref, s_ref = rest[pp:]
    g = pl.program_id(1)
    q = q_ref[...]
    kw = 2 * hd
    slab = _stage_pages(pages)
    pg_ref[g] = slab
    s_ref[g] = _dot(q, slab[0:kw, :]) + bias_ref[g]

    @pl.when(g == pl.num_programs(1) - 1)
    def _():
        new = new_ref[...]
        s_new = lax.dot_general(q, new[:, 0:kw], NT, preferred_element_type=F32)
        s_new = s_new + bnew_ref[0:1, 0:8]
        o = _paged_softmax_pv(s_ref, pg_ref, s_new, new[:, kw:2 * kw], slice(kw, 2 * kw))
        rowi = lax.broadcasted_iota(I32, o.shape, 0)
        o_ref[...] = jnp.where(rowi < gsz, o, pltpu.roll(o, hd, 1))


def _dsa_decode2(page_table, qbd, newkv, bias, bias_new, cache, pp, hd, gsz):
    bsz = qbd.shape[0]
    n_pages = page_table.shape[1]
    psz = cache.shape[2]
    ng, w = n_pages // pp, pp * psz
    pt = page_table.reshape(-1)

    def page_map(k):
        return lambda b, g, ptr: (ptr[b * n_pages + g * pp + k], 0, 0)

    kern = functools.partial(_dsa_dec2_kernel, pp=pp, hd=hd, gsz=gsz)
    grid_spec = pltpu.PrefetchScalarGridSpec(
        num_scalar_prefetch=1,
        grid=(bsz, ng),
        in_specs=[pl.BlockSpec((None, 8, 2 * hd), lambda b, g, ptr: (b, 0, 0)),
                  pl.BlockSpec((None, 8, 4 * hd), lambda b, g, ptr: (b, 0, 0)),
                  pl.BlockSpec((None, ng, 1, w), lambda b, g, ptr: (b, 0, 0, 0)),
                  pl.BlockSpec((None, 8, LANES), lambda b, g, ptr: (b, 0, 0))]
                 + [pl.BlockSpec((None, 4 * hd, psz), page_map(k)) for k in range(pp)],
        out_specs=pl.BlockSpec((None, 8, 2 * hd), lambda b, g, ptr: (b, 0, 0)),
        scratch_shapes=[pltpu.VMEM((ng, 4 * hd, w), BF16), pltpu.VMEM((ng, 8, w), F32)],
    )
    return pl.pallas_call(
        kern,
        grid_spec=grid_spec,
        out_shape=jax.ShapeDtypeStruct((bsz, 8, 2 * hd), F32),
        compiler_params=_cp(("parallel", "arbitrary")),
        name="dsa_decode",
    )(pt, qbd, newkv, bias.reshape(bsz, ng, 1, w), bias_new, *([cache] * pp))


def _rope_tables(pos, half):
    inv_freq = ROPE_THETA ** (-jnp.arange(half, dtype=F32) / half)
    ang = pos.astype(F32)[:, None] * inv_freq[None, :]
    cos, sin = jnp.cos(ang), jnp.sin(ang)
    reps = LANES // (2 * half)
    return (jnp.tile(jnp.concatenate([cos, cos], axis=1), (1, reps)),
            jnp.tile(jnp.concatenate([-sin, sin], axis=1), (1, reps)))


def _pick(n, pref):
    for c in pref:
        if n % c == 0:
            return c
    return n


def kernel(x_prompt, x_sample, cache_mla, cache_dsa_kv, cache_idx, state_hgrn, page_table, c_prompt, c_sample,
           w_ada, b_ada, norm_mix_g, norm_ffn_g, w_in_even, mla_q_norm_g, mla_kv_norm_g, mla_w_q_nope,
           mla_w_q_rope, mla_w_uk, mla_w_uv, w_out_even, w_in_odd, hgrn_lower_bounds, hgrn_out_norm_g,
           w_out_odd, w_ffn_gate, w_ffn_up, w_ffn_down, final_norm_g):
    depth, d, _ = w_ada.shape
    bp, tp, _ = x_prompt.shape
    bs, ts, _ = x_sample.shape
    assert ts == 1
    q_lora, mla_heads, mla_nope = mla_w_q_nope.shape[1:]
    mla_rope = mla_w_q_rope.shape[3]
    kv_lora, _, mla_v = mla_w_uv.shape[1:]
    dsa_kv_heads, dsa_hd = cache_dsa_kv.shape[4:]
    idx_dim = cache_idx.shape[3]
    n_pages = page_table.shape[1]
    psz = cache_mla.shape[2]
    past_len = n_pages * psz
    sizes_tail = w_in_even.shape[2] - (q_lora + kv_lora + mla_rope + 2 * dsa_kv_heads * dsa_hd + idx_dim)
    dsa_heads = (w_out_even.shape[1] - mla_heads * mla_v) // dsa_hd
    idx_heads = (sizes_tail - dsa_heads * dsa_hd) // (idx_dim + 1)
    dm = EvenDims(q_lora, kv_lora, mla_rope, mla_heads, dsa_heads, dsa_kv_heads, dsa_hd, idx_heads, idx_dim)
    gsz = dsa_heads // dsa_kv_heads
    hg_f = hgrn_lower_bounds.shape[1]
    hg_heads = state_hgrn.shape[2]
    hg_k = state_hgrn.shape[3]
    hg_v = state_hgrn.shape[4]
    log2e = float(np.log2(np.e))
    scales = ((mla_nope + mla_rope) ** -0.5 * log2e, dsa_hd ** -0.5 * log2e, idx_dim ** -0.5, idx_heads ** -0.5)
    hg_q_scale = hg_k ** -0.5

    lb_all = jax.nn.softmax(hgrn_lower_bounds.astype(F32), axis=0)
    lb_all = jnp.cumsum(lb_all, axis=0) - lb_all[0]

    c_all = jnp.concatenate([c_prompt, c_sample], axis=0)
    mods = _adaln(c_all, w_ada, b_ada)

    def mods_for(l, sample):
        mm = mods[l, bp:] if sample else mods[l, :bp]
        parts = [mm[:, k * d:(k + 1) * d] for k in range(6)]
        return [p[None] if sample else p[:, None, :] for p in parts]

    bf = lambda a: a.astype(BF16)
    row1 = lambda a: a.reshape(1, -1).astype(F32)

    def even_weights(la):
        w = w_in_even[la]
        splits = np.cumsum([q_lora, kv_lora, mla_rope, dsa_heads * dsa_hd, dsa_kv_heads * dsa_hd,
                            dsa_kv_heads * dsa_hd, idx_heads * idx_dim, idx_dim])
        wq, wckv, wkr, wdq, wdk, wdv, wiq, wik, wiw = jnp.split(w, splits, axis=1)
        padc = lambda a, n: jnp.pad(a, ((0, 0), (0, n - a.shape[1])))
        w_cat = jnp.concatenate([wq, wckv, padc(wkr, LANES), wdq, wdk, wdv, wiq,
                                 padc(jnp.concatenate([wik, wiw], axis=1), LANES)], axis=1)
        assert w_cat.shape[1] == dm.n
        wabs = _absorb(mla_w_q_nope[la], mla_w_uk[la])
        wqr = mla_w_q_rope[la].reshape(q_lora, mla_heads * mla_rope)
        wuv = jnp.transpose(mla_w_uv[la], (1, 0, 2))
        return bf(w_cat), bf(wabs), bf(wqr), bf(wuv)

    def run(x, sample):
        b, t, _ = x.shape
        m = b * t
        x2d = x.reshape(m, d)
        if sample:
            bq, tq, per_row = 1, m, True
            tm = _pick(m, (128,))
            pos = jnp.full((m,), past_len, I32)
        else:
            bq, tq, per_row = b, t, False
            tm = _pick(t, (512, 256, 128))
            pos = jnp.arange(t)
        tabs = (*_rope_tables(pos, dsa_hd // 2), *_rope_tables(pos, mla_rope // 2))
        rows_mla, rows_dsa, rows_idx, states = [], [], [], []
        for l in range(depth):
            sh1, sc1, g1, sh2, sc2, g2 = mods_for(l, sample)
            nmg, nfg = row1(norm_mix_g[l]), row1(norm_ffn_g[l])
            final = l == depth - 1
            if l % 2 == 0:
                la = l // 2
                w_cat, wabs, wqr, wuv = even_weights(la)
                (qcat, mrow, kv16, dsarow, dq, dk, dvt, iq, ik, ik16, iwt) = _even_project(
                    x2d, sc1, sh1, nmg, w_cat, row1(mla_q_norm_g[la]), row1(mla_kv_norm_g[la]), wqr, wabs,
                    tabs, dm, bq, tq, tm, per_row, scales)
                if sample:
                    nsel = min(TOPK_MAX, (past_len + t) // 4)
                    pp = _pick(n_pages, (16, 8))
                    pad8 = lambda a: jnp.pad(a[:, None, :], ((0, 0), (0, 7), (0, 0)))
                    qdec = jnp.transpose(qcat[0], (1, 0, 2))
                    page_t = lambda c: jnp.transpose(c, (0, 2, 1))
                    mla_o = _mla_decode2(page_table, qdec, pad8(kv16), wuv, page_t(cache_mla[la]), pp)
                    iqd = jnp.pad(jnp.transpose(iq[0], (1, 0, 2)), ((0, 0), (0, 8 - idx_heads), (0, 0)))
                    iwd = jnp.transpose(iwt[0])[:, :, None]
                    iwd = jnp.where(jnp.arange(8)[None, :, None] < idx_heads, iwd, 0.0)
                    bias, bias_new = _idx_decode2(page_table, iqd, iwd, pad8(ik16), page_t(cache_idx[la]), nsel, pp)
                    dqd = jnp.transpose(dq[0], (1, 0, 2))
                    zeros = jnp.zeros_like(dqd[:, :gsz])
                    qbd = jnp.concatenate(
                        [jnp.concatenate([dqd[:, :gsz], zeros], axis=2),
                         jnp.concatenate([zeros, dqd[:, gsz:]], axis=2)], axis=1)
                    cache_kv = jnp.transpose(cache_dsa_kv[la], (0, 2, 3, 4, 1)).reshape(-1, 4 * dsa_hd, psz)
                    dsa_o8 = _dsa_decode2(page_table, qbd, pad8(bf(dsarow)), bias, bias_new, cache_kv, pp, dsa_hd,
                                          gsz)
                    dsa_o = dsa_o8[:, :, :dsa_hd].reshape(b, dsa_heads * dsa_hd)
                    mla_o, dsa_o = bf(mla_o), bf(dsa_o)
                else:
                    nsel = min(TOPK_MAX, t // 4)
                    mla_o = _mla_prompt(qcat, kv16.reshape(b, t, -1), wuv, _pick(t, (256, 128))).reshape(m, -1)
                    dsa_o = _dsa_prompt(iq, iwt, ik16, dq, dk, dvt, nsel, _pick(t, (256, 128))).reshape(m, -1)
                n_mla = mla_heads * mla_v
                mixes = [(mla_o, bf(w_out_even[la][:n_mla])), (dsa_o, bf(w_out_even[la][n_mla:]))]
                rows_mla.append(mrow.reshape(b, t, -1))
                if sample:
                    rows_dsa.append(dsarow.reshape(b, t, 2, dsa_kv_heads, dsa_hd))
                else:
                    rows_dsa.append(jnp.transpose(dsarow.reshape(b, 2, dsa_kv_heads, dsa_hd, t), (0, 4, 1, 2, 3)))
                rows_idx.append(ik.reshape(b, t, -1))
            else:
                lr = l // 2
                z = _normmod_mm(x2d, sc1, sh1, nmg, bf(w_in_odd[lr]), tq, tm, _pick(w_in_odd.shape[2], (2048,)),
                                per_row)
                lb = lb_all[l].reshape(hg_heads, 1, hg_k)
                og = row1(hgrn_out_norm_g[lr])
                if sample:
                    o_g, st = _hgrn_step(z, lb, og, state_hgrn[lr].astype(F32), hg_heads, hg_q_scale)
                else:
                    o_g, st = _hgrn_prompt(z, lb, og, b, t, hg_heads, _pick(t, (256, 128)), hg_q_scale)
                mixes = [(o_g, bf(w_out_odd[lr]))]
                states.append(st)
            x2d = _mix_ffn(x2d, mixes, g1, nfg, sc2, sh2, g2, bf(w_ffn_gate[l]), bf(w_ffn_up[l]),
                           bf(w_ffn_down[l]), row1(final_norm_g), tq, tm, per_row, final)
        return (x2d.reshape(b, t, d), jnp.stack(rows_mla), jnp.stack(rows_dsa), jnp.stack(rows_idx),
                jnp.stack(states))

    y_p, mla_p, dsa_p, idx_p, hg_p = run(x_prompt, False)
    y_s, mla_s, dsa_s, idx_s, hg_s = run(x_sample, True)
    return (y_p, y_s, mla_p, mla_s, dsa_p, dsa_s, idx_p, idx_s, hg_p, hg_s)
```

```python
import functools

import jax
import jax.numpy as jnp
import numpy as np
from jax import lax
from jax.experimental import pallas as pl
from jax.experimental.pallas import tpu as pltpu

F32 = jnp.float32
BF16 = jnp.bfloat16
I32 = jnp.int32

EPS = 1e-6
ROPE_THETA = 10000.0
TOPK_MAX = 256
NEG = -1e30
INT_MIN = -(2 ** 31)
LANES = 128
DSA_KEY_CHUNK = 512
MIB = 1024 * 1024

NT = (((1,), (1,)), ((), ()))
TN = (((0,), (0,)), ((), ()))


def _cp(sem, vmem_mib=48):
    return pltpu.CompilerParams(dimension_semantics=sem, vmem_limit_bytes=vmem_mib * MIB)


def _sigmoid(x):
    return 1.0 / (1.0 + jnp.exp(-x))


def _silu(x):
    return x * _sigmoid(x)


def _rms(x, g):
    return x * lax.rsqrt(jnp.mean(x * x, axis=-1, keepdims=True) + EPS) * g


def _dot(a, b):
    return jnp.dot(a, b, preferred_element_type=F32)


def _rope_cols(x, cos, sin_signed, half):
    outs = []
    lane = lax.broadcasted_iota(I32, (x.shape[0], LANES), 1)
    first = (lane % (2 * half)) < half
    for c in range(x.shape[1] // LANES):
        xc = x[:, c * LANES:(c + 1) * LANES]
        rot = jnp.where(first, pltpu.roll(xc, LANES - half, 1), pltpu.roll(xc, half, 1))
        outs.append(xc * cos + rot * sin_signed)
    return outs[0] if len(outs) == 1 else jnp.concatenate(outs, axis=1)


def _ada_kernel(c_ref, w_ref, b_ref, o_ref):
    a = _silu(c_ref[...]).astype(BF16)
    o_ref[...] = _dot(a, w_ref[...].astype(BF16)) + b_ref[...]


def _adaln(c_all, w_ada, b_ada):
    depth, d, n = w_ada.shape
    bc = c_all.shape[0]
    tn = n // 4
    return pl.pallas_call(
        _ada_kernel,
        grid=(depth, n // tn),
        in_specs=[pl.BlockSpec((bc, d), lambda l, j: (0, 0)),
                  pl.BlockSpec((None, d, tn), lambda l, j: (l, 0, j)),
                  pl.BlockSpec((None, 1, tn), lambda l, j: (l, 0, j))],
        out_specs=pl.BlockSpec((None, bc, tn), lambda l, j: (l, 0, j)),
        out_shape=jax.ShapeDtypeStruct((depth, bc, n), F32),
        compiler_params=_cp(("parallel", "parallel")),
        name="adaln",
    )(c_all, w_ada, b_ada.reshape(depth, 1, n))


def _absorb_kernel(a_ref, b_ref, o_ref):
    o_ref[...] = lax.dot_general(a_ref[...].astype(BF16), b_ref[...].astype(BF16), NT,
                                 preferred_element_type=F32)


def _absorb(w_q_nope, w_uk):
    a = jnp.transpose(w_q_nope, (1, 0, 2))
    b = jnp.transpose(w_uk, (1, 0, 2))
    h, rq, dn = a.shape
    rkv = b.shape[1]
    out = pl.pallas_call(
        _absorb_kernel,
        grid=(h,),
        in_specs=[pl.BlockSpec((None, rq, dn), lambda i: (i, 0, 0)),
                  pl.BlockSpec((None, rkv, dn), lambda i: (i, 0, 0))],
        out_specs=pl.BlockSpec((None, rq, rkv), lambda i: (i, 0, 0)),
        out_shape=jax.ShapeDtypeStruct((h, rq, rkv), F32),
        compiler_params=_cp(("parallel",)),
        name="mla_absorb",
    )(a, b)
    return jnp.transpose(out, (1, 0, 2)).reshape(rq, h * rkv)


class EvenDims:
    def __init__(self, q_lora, kv_lora, mla_rope, mla_heads, dsa_heads, dsa_kv_heads, dsa_hd, idx_heads, idx_dim):
        assert q_lora % LANES == 0 and kv_lora == LANES and mla_rope <= LANES
        assert dsa_hd == 64 and idx_dim == 64 and dsa_kv_heads == 2 and idx_heads <= 8
        self.q_lora, self.kv_lora, self.mla_rope, self.mla_heads = q_lora, kv_lora, mla_rope, mla_heads
        self.dsa_heads, self.dsa_kv_heads, self.dsa_hd = dsa_heads, dsa_kv_heads, dsa_hd
        self.idx_heads, self.idx_dim = idx_heads, idx_dim
        self.o_zq = 0
        self.o_ckv = q_lora
        self.o_kr = self.o_ckv + kv_lora
        self.o_dq = self.o_kr + LANES
        self.o_dk = self.o_dq + dsa_heads * dsa_hd
        self.o_dv = self.o_dk + dsa_kv_heads * dsa_hd
        self.o_iq = self.o_dv + dsa_kv_heads * dsa_hd
        self.o_ik = self.o_iq + idx_heads * idx_dim
        self.n = self.o_ik + LANES
        self.mla_row = kv_lora + mla_rope


def _even_kernel(x_ref, sc_ref, sh_ref, g_ref, w_ref, qg_ref, kvg_ref, wqr_ref, wabs_ref,
                 c64_ref, s64_ref, c32_ref, s32_ref,
                 qcat_ref, row_ref, kv16_ref, dsarow_ref, dq_ref, dk_ref, dvt_ref, iq_ref, ik_ref, ik16_ref,
                 iwt_ref, *, dm, mla_scale, dsa_scale, idx_scale, idx_w_scale):
    h = _rms(x_ref[...], g_ref[...]) * (1.0 + sc_ref[0]) + sh_ref[0]
    z = _dot(h.astype(BF16), w_ref[...])
    c64, s64, c32, s32 = c64_ref[...], s64_ref[...], c32_ref[...], s32_ref[...]
    tm = z.shape[0]

    cq = _rms(z[:, dm.o_zq:dm.o_zq + dm.q_lora], qg_ref[...]).astype(BF16)
    qlat = _dot(cq, wabs_ref[...]) * mla_scale
    qrope = _rope_cols(_dot(cq, wqr_ref[...]), c32, s32, dm.mla_rope // 2) * mla_scale
    for hd in range(dm.mla_heads):
        qcat_ref[0, hd, :, 0:dm.kv_lora] = qlat[:, hd * dm.kv_lora:(hd + 1) * dm.kv_lora].astype(BF16)
        qcat_ref[0, hd, :, dm.kv_lora:dm.mla_row] = qrope[:, hd * dm.mla_rope:(hd + 1) * dm.mla_rope].astype(BF16)

    ckv = _rms(z[:, dm.o_ckv:dm.o_ckv + dm.kv_lora], kvg_ref[...])
    kr = _rope_cols(z[:, dm.o_kr:dm.o_kr + LANES], c32, s32, dm.mla_rope // 2)[:, 0:dm.mla_rope]
    row_ref[:, 0:dm.kv_lora] = ckv
    row_ref[:, dm.kv_lora:dm.mla_row] = kr
    kv16_ref[:, 0:dm.kv_lora] = ckv.astype(BF16)
    kv16_ref[:, dm.kv_lora:dm.mla_row] = kr.astype(BF16)

    hd_ = dm.dsa_hd
    dq = _rope_cols(z[:, dm.o_dq:dm.o_dk], c64, s64, hd_ // 2) * dsa_scale
    for hh in range(dm.dsa_heads):
        dq_ref[0, hh] = dq[:, hh * hd_:(hh + 1) * hd_].astype(BF16)
    dk = _rope_cols(z[:, dm.o_dk:dm.o_dv], c64, s64, hd_ // 2)
    dv = z[:, dm.o_dv:dm.o_iq]
    dvt = dv.T
    if len(dsarow_ref.shape) == 3:
        dsarow_ref[0, 0:LANES, :] = dk.T
        dsarow_ref[0, LANES:2 * LANES, :] = dvt
    else:
        dsarow_ref[:, 0:LANES] = dk
        dsarow_ref[:, LANES:2 * LANES] = dv
    for gg in range(dm.dsa_kv_heads):
        dk_ref[0, gg] = dk[:, gg * hd_:(gg + 1) * hd_].astype(BF16)
    vck = dvt_ref.shape[3]
    for c in range(tm // vck):
        dvt_ref[0, c] = dvt[:, c * vck:(c + 1) * vck].astype(BF16)

    iq = _rope_cols(z[:, dm.o_iq:dm.o_ik], c64, s64, dm.idx_dim // 2) * idx_scale
    for hh in range(dm.idx_heads):
        iq_ref[0, hh] = iq[:, hh * dm.idx_dim:(hh + 1) * dm.idx_dim].astype(BF16)
    last = z[:, dm.o_ik:dm.o_ik + LANES]
    ik = _rope_cols(last, c64, s64, dm.idx_dim // 2)[:, 0:dm.idx_dim]
    ik_ref[...] = ik
    ik16_ref[...] = ik.astype(BF16)
    iwt_ref[0] = last.T[dm.idx_dim:dm.idx_dim + 8, :] * idx_w_scale


def _even_project(x2d, sc, sh, g, w_in, qg, kvg, wqr, wabs, tabs, dm, bq, tq, tm, per_row, scales):
    rows_t = not per_row
    m, d = x2d.shape
    nt = tq // tm
    vck = min(DSA_KEY_CHUNK, tm)
    nh, kvh, ih = dm.mla_heads, dm.dsa_kv_heads, dm.idx_heads
    row2 = lambda i: (i, 0)
    hm4 = lambda i: (i // nt, 0, i % nt, 0)
    if per_row:
        mod_spec = pl.BlockSpec((1, tm, d), lambda i: (0, i, 0))
    else:
        mod_spec = pl.BlockSpec((1, 1, d), lambda i: (i // nt, 0, 0))
    const = lambda shape: pl.BlockSpec(shape, lambda i: tuple(0 for _ in shape))
    tab_spec = pl.BlockSpec((tm, LANES), lambda i: (i % nt, 0))
    out_shapes = (
        jax.ShapeDtypeStruct((bq, nh, tq, dm.mla_row), BF16),
        jax.ShapeDtypeStruct((m, dm.mla_row), F32),
        jax.ShapeDtypeStruct((m, dm.mla_row), BF16),
        (jax.ShapeDtypeStruct((bq, 2 * LANES, tq), F32) if rows_t
         else jax.ShapeDtypeStruct((m, 2 * LANES), F32)),
        jax.ShapeDtypeStruct((bq, dm.dsa_heads, tq, dm.dsa_hd), BF16),
        jax.ShapeDtypeStruct((bq, kvh, tq, dm.dsa_hd), BF16),
        jax.ShapeDtypeStruct((bq, tq // vck, LANES, vck), BF16),
        jax.ShapeDtypeStruct((bq, ih, tq, dm.idx_dim), BF16),
        jax.ShapeDtypeStruct((m, dm.idx_dim), F32),
        jax.ShapeDtypeStruct((m, dm.idx_dim), BF16),
        jax.ShapeDtypeStruct((bq, 8, tq), F32),
    )
    out_specs = (
        pl.BlockSpec((1, nh, tm, dm.mla_row), hm4),
        pl.BlockSpec((tm, dm.mla_row), row2),
        pl.BlockSpec((tm, dm.mla_row), row2),
        (pl.BlockSpec((1, 2 * LANES, tm), lambda i: (i // nt, 0, i % nt)) if rows_t
         else pl.BlockSpec((tm, 2 * LANES), row2)),
        pl.BlockSpec((1, dm.dsa_heads, tm, dm.dsa_hd), hm4),
        pl.BlockSpec((1, kvh, tm, dm.dsa_hd), hm4),
        pl.BlockSpec((1, tm // vck, LANES, vck), lambda i: (i // nt, i % nt, 0, 0)),
        pl.BlockSpec((1, ih, tm, dm.idx_dim), hm4),
        pl.BlockSpec((tm, dm.idx_dim), row2),
        pl.BlockSpec((tm, dm.idx_dim), row2),
        pl.BlockSpec((1, 8, tm), lambda i: (i // nt, 0, i % nt)),
    )
    kern = functools.partial(_even_kernel, dm=dm, mla_scale=scales[0], dsa_scale=scales[1],
                             idx_scale=scales[2], idx_w_scale=scales[3])
    return pl.pallas_call(
        kern,
        grid=(m // tm,),
        in_specs=[pl.BlockSpec((tm, d), row2), mod_spec, mod_spec, const((1, d)), const(w_in.shape),
                  const(qg.shape), const(kvg.shape), const(wqr.shape), const(wabs.shape),
                  tab_spec, tab_spec, tab_spec, tab_spec],
        out_specs=out_specs,
        out_shape=out_shapes,
        compiler_params=_cp(("parallel",), 56),
        name="even_project",
    )(x2d, sc, sh, g, w_in, qg, kvg, wqr, wabs, *tabs)


def _fold_lanes(x, op):
    out = x[:, 0:LANES]
    for c in range(1, x.shape[1] // LANES):
        out = op(out, x[:, c * LANES:(c + 1) * LANES])
    return out


def _mla_kernel(q_ref, kv_ref, wuv_ref, o_ref, s_ref, mp_ref, lp_ref, acc_ref, *, tq, tk, nh, r, vd):
    qi = pl.program_id(1)
    n = nh * tq
    q = q_ref[0].reshape(n, q_ref.shape[-1])
    nch = (qi * tq + tq + tk - 1) // tk
    reps = tk // LANES

    def keys(j):
        return kv_ref[0, pl.ds(pl.multiple_of(j * tk, tk), tk), :]

    mp_ref[...] = jnp.full(mp_ref.shape, NEG, F32)

    def pass_a(j, c):
        s = lax.dot_general(q, keys(j), NT, preferred_element_type=F32)
        s_ref[j] = s
        mp_ref[...] = jnp.maximum(mp_ref[...], _fold_lanes(s, jnp.maximum))
        return c

    lax.fori_loop(0, nch - 1, pass_a, 0)
    jl = nch - 1
    s = lax.dot_general(q, keys(jl), NT, preferred_element_type=F32)
    qpos = qi * tq + lax.broadcasted_iota(I32, s.shape, 0) % tq
    kpos = jl * tk + lax.broadcasted_iota(I32, s.shape, 1)
    s = jnp.where(kpos <= qpos, s, NEG)
    s_ref[jl] = s
    mp = jnp.maximum(mp_ref[...], _fold_lanes(s, jnp.maximum))
    mb = jnp.broadcast_to(jnp.max(mp, axis=1, keepdims=True), (n, LANES))
    mbt = jnp.concatenate([mb] * reps, axis=1)

    lp_ref[...] = jnp.zeros(lp_ref.shape, F32)
    acc_ref[...] = jnp.zeros(acc_ref.shape, F32)

    def pass_b(j, c):
        p = jnp.exp2(s_ref[j] - mbt)
        lp_ref[...] += _fold_lanes(p, jnp.add)
        acc_ref[...] += _dot(p.astype(BF16), keys(j)[:, 0:r])
        return c

    lax.fori_loop(0, nch, pass_b, 0)
    lat = (acc_ref[...] / jnp.sum(lp_ref[...], axis=1, keepdims=True)).astype(BF16)
    for hd in range(nh):
        o_ref[0, :, hd * vd:(hd + 1) * vd] = _dot(lat[hd * tq:(hd + 1) * tq], wuv_ref[hd]).astype(BF16)


def _mla_prompt(qcat, kv16, wuv, tq):
    b, nh, t, dq = qcat.shape
    r, vd = wuv.shape[1], wuv.shape[2]
    tk = _pick(t, (512, 256, 128))
    n = nh * tq
    kern = functools.partial(_mla_kernel, tq=tq, tk=tk, nh=nh, r=r, vd=vd)
    return pl.pallas_call(
        kern,
        grid=(b, t // tq),
        in_specs=[pl.BlockSpec((1, nh, tq, dq), lambda bi, qi: (bi, 0, qi, 0)),
                  pl.BlockSpec((1, t, dq), lambda bi, qi: (bi, 0, 0)),
                  pl.BlockSpec(wuv.shape, lambda bi, qi: (0, 0, 0))],
        out_specs=pl.BlockSpec((1, tq, nh * vd), lambda bi, qi: (bi, qi, 0)),
        out_shape=jax.ShapeDtypeStruct((b, t, nh * vd), BF16),
        scratch_shapes=[pltpu.VMEM((t // tk, n, tk), F32), pltpu.VMEM((n, LANES), F32),
                        pltpu.VMEM((n, LANES), F32), pltpu.VMEM((n, r), F32)],
        compiler_params=_cp(("parallel", "arbitrary")),
        name="mla_prompt",
    )(qcat, kv16, wuv)


def _sort_key(score):
    b = pltpu.bitcast(score, I32)
    return jnp.where(b < 0, (b ^ jnp.int32(0x7FFFFFFF)) + 1, b)


def _topk_select(read_keys, read_pos, nch, rows, width, nsel, pos_bits):
    def count(pred):
        def body(j, c):
            return c + jnp.sum(pred(j).astype(I32).reshape(rows // 8, 8, width), axis=0)
        c = lax.fori_loop(0, nch, body, jnp.zeros((8, width), I32))
        return jnp.sum(c, axis=0, keepdims=True)

    def bit_body(b, carry):
        thr, c_thr = carry
        cand = thr + lax.shift_left(jnp.int32(1), 31 - b)
        cnt = count(lambda j: read_keys(j) >= cand)
        ok = cnt >= nsel
        return jnp.where(ok, cand, thr), jnp.where(ok, cnt, c_thr)

    thr, c_thr = lax.fori_loop(0, 32, bit_body, (jnp.full((1, width), INT_MIN, I32),
                                                 jnp.zeros((1, width), I32)))

    def tie_search():
        need = nsel - count(lambda j: read_keys(j) > thr)

        def jbit_body(b, jc):
            cand = jc + lax.shift_left(jnp.int32(1), pos_bits - 1 - b)
            cnt = count(lambda j: (read_keys(j) == thr) & (read_pos(j) < cand))
            return jnp.where(cnt <= need, cand, jc)

        return lax.fori_loop(0, pos_bits, jbit_body, jnp.zeros((1, width), I32))

    excess = (c_thr > nsel) & (thr > INT_MIN)
    any_excess = jnp.max(excess.astype(I32)) > 0
    jcut = lax.cond(any_excess, tie_search, lambda: jnp.full((1, width), 2 ** pos_bits, I32))
    return thr, jcut


def _selected(keys, pos, thr, jcut):
    return ((keys > thr) | ((keys == thr) & (pos < jcut))) & (keys != INT_MIN)


def _dsa_kernel(iq_ref, iwt_ref, ik_ref, dq_ref, dk_ref, dvt_ref, o_ref, keys_ref, bias_ref, s_ref,
                *, tq, ck, nsel, ih, gsz, kvh, hd, pos_bits):
    i = pl.program_id(1)
    nch = ((i + 1) * tq + ck - 1) // ck
    row = lax.broadcasted_iota(I32, (ck, tq), 0)
    qpos = i * tq + lax.broadcasted_iota(I32, (ck, tq), 1)

    def off(j):
        return pl.multiple_of(j * ck, ck)

    def score_chunk(j, carry):
        ikc = ik_ref[0, pl.ds(off(j), ck), :]
        acc = jnp.zeros((ck, tq), F32)
        for hh in range(ih):
            d = lax.dot_general(ikc, iq_ref[0, hh], NT, preferred_element_type=F32)
            acc = acc + iwt_ref[0, hh:hh + 1, :] * jnp.maximum(d, 0.0)
        valid = (j * ck + row) <= qpos
        keys_ref[pl.ds(off(j), ck), :] = jnp.where(valid, _sort_key(acc), INT_MIN)
        return carry

    lax.fori_loop(0, nch, score_chunk, 0)

    read_keys = lambda j: keys_ref[pl.ds(off(j), ck), :]
    read_pos = lambda j: j * ck + row
    thr, jcut = _topk_select(read_keys, read_pos, nch, ck, tq, nsel, pos_bits)

    def bias_chunk(j, carry):
        sel = _selected(read_keys(j), read_pos(j), thr, jcut)
        bias_ref[pl.ds(off(j), ck), :] = jnp.where(sel, 0.0, NEG)
        return carry

    lax.fori_loop(0, nch, bias_chunk, 0)

    qs = [dq_ref[0, g * gsz:(g + 1) * gsz].reshape(gsz * tq, hd) for g in range(kvh)]
    wq = gsz * tq

    def pass_a(j, ms):
        b = bias_ref[pl.ds(off(j), ck), :]
        bt = jnp.concatenate([b] * gsz, axis=1)
        out = []
        for g in range(kvh):
            s = lax.dot_general(dk_ref[0, g, pl.ds(off(j), ck), :], qs[g], NT, preferred_element_type=F32) + bt
            s_ref[g, pl.ds(off(j), ck), :] = s
            out.append(jnp.maximum(ms[g], jnp.max(s, axis=0, keepdims=True)))
        return tuple(out)

    ms = lax.fori_loop(0, nch, pass_a, tuple(jnp.full((1, wq), NEG, F32) for _ in range(kvh)))

    def pass_b(j, carry):
        out = []
        for g in range(kvh):
            l, acc = carry[g]
            p = jnp.exp2(s_ref[g, pl.ds(off(j), ck), :] - ms[g])
            l = l + jnp.sum(p, axis=0, keepdims=True)
            acc = acc + _dot(dvt_ref[0, j, g * hd:(g + 1) * hd, :], p.astype(BF16))
            out.append((l, acc))
        return tuple(out)

    res = lax.fori_loop(0, nch, pass_b,
                        tuple((jnp.zeros((1, wq), F32), jnp.zeros((hd, wq), F32)) for _ in range(kvh)))
    outs = []
    for g in range(kvh):
        ot = res[g][1] / res[g][0]
        for hh in range(gsz):
            outs.append(ot[:, hh * tq:(hh + 1) * tq])
    o_ref[0] = jnp.concatenate(outs, axis=0).T.astype(BF16)


def _dsa_prompt(iq, iwt, ik16, dq, dk, dvt, nsel, tq):
    b, ih, t, idim = iq.shape
    nhd, hd = dq.shape[1], dq.shape[3]
    kvh = dk.shape[1]
    gsz = nhd // kvh
    ck = dvt.shape[3]
    ik3 = ik16.reshape(b, t, idim)
    pos_bits = int(t).bit_length()
    kern = functools.partial(_dsa_kernel, tq=tq, ck=ck, nsel=nsel, ih=ih, gsz=gsz, kvh=kvh, hd=hd,
                             pos_bits=pos_bits)
    return pl.pallas_call(
        kern,
        grid=(b, t // tq),
        in_specs=[pl.BlockSpec((1, ih, tq, idim), lambda bi, i: (bi, 0, i, 0)),
                  pl.BlockSpec((1, 8, tq), lambda bi, i: (bi, 0, i)),
                  pl.BlockSpec((1, t, idim), lambda bi, i: (bi, 0, 0)),
                  pl.BlockSpec((1, nhd, tq, hd), lambda bi, i: (bi, 0, i, 0)),
                  pl.BlockSpec((1, kvh, t, hd), lambda bi, i: (bi, 0, 0, 0)),
                  pl.BlockSpec((1, t // ck, 2 * hd, ck), lambda bi, i: (bi, 0, 0, 0))],
        out_specs=pl.BlockSpec((1, tq, nhd * hd), lambda bi, i: (bi, i, 0)),
        out_shape=jax.ShapeDtypeStruct((b, t, nhd * hd), BF16),
        scratch_shapes=[pltpu.VMEM((t, tq), I32), pltpu.VMEM((t, tq), F32),
                        pltpu.VMEM((kvh, t, gsz * tq), F32)],
        compiler_params=_cp(("parallel", "arbitrary")),
        name="dsa_prompt",
    )(iq, iwt, ik3, dq, dk, dvt)


def _ffn_kernel(*refs, n_mix, final, splits):
    x_ref = refs[0]
    mix_refs = refs[1:1 + 2 * n_mix]
    (g1_ref, nfg_ref, sc_ref, sh_ref, g2_ref, wg_ref, wu_ref, wd_ref, fg_ref, o_ref) = refs[1 + 2 * n_mix:]
    mix = _dot(mix_refs[0][...], mix_refs[1][...])
    for k in range(1, n_mix):
        mix = mix + _dot(mix_refs[2 * k][...], mix_refs[2 * k + 1][...])
    x1 = x_ref[...] + g1_ref[0] * mix
    hb = (_rms(x1, nfg_ref[...]) * (1.0 + sc_ref[0]) + sh_ref[0]).astype(BF16)
    y = None
    for lo, hi in splits:
        act = _silu(_dot(hb, wg_ref[:, lo:hi])) * _dot(hb, wu_ref[:, lo:hi])
        part = _dot(act.astype(BF16), wd_ref[lo:hi, :])
        y = part if y is None else y + part
    y = x1 + g2_ref[0] * y
    if final:
        y = _rms(y, fg_ref[...])
    o_ref[...] = y


MXU_TILE = 256


def _mix_ffn(x2d, mixes, g1, nfg, sc2, sh2, g2, wg, wu, wd, fg, tq, tm, per_row, final):
    m, d = x2d.shape
    dff = wg.shape[1]
    nt = tq // tm
    half = (dff // MXU_TILE + 1) // 2 * MXU_TILE
    splits = ((0, half), (half, dff)) if 0 < half < dff else ((0, dff),)
    if per_row:
        mod_spec = pl.BlockSpec((1, tm, d), lambda i: (0, i, 0))
    else:
        mod_spec = pl.BlockSpec((1, 1, d), lambda i: (i // nt, 0, 0))
    row = lambda i: (i, 0)
    const2 = lambda shape: pl.BlockSpec(shape, lambda i: (0, 0), pipeline_mode=pl.Buffered(1))
    in_specs = [pl.BlockSpec((tm, d), row)]
    args = [x2d]
    for a, w in mixes:
        in_specs += [pl.BlockSpec((tm, a.shape[1]), row), const2(w.shape)]
        args += [a, w]
    in_specs += [mod_spec, const2((1, d)), mod_spec, mod_spec, mod_spec,
                 const2(wg.shape), const2(wu.shape), const2(wd.shape), const2((1, d))]
    args += [g1, nfg, sc2, sh2, g2, wg, wu, wd, fg]
    kern = functools.partial(_ffn_kernel, n_mix=len(mixes), final=final, splits=splits)
    return pl.pallas_call(
        kern,
        grid=(m // tm,),
        in_specs=in_specs,
        out_specs=pl.BlockSpec((tm, d), row),
        out_shape=jax.ShapeDtypeStruct((m, d), F32),
        compiler_params=_cp(("parallel",), 60),
        name="mix_ffn",
    )(*args)


def _normmod_mm_kernel(x_ref, sc_ref, sh_ref, g_ref, w_ref, o_ref):
    h = _rms(x_ref[...], g_ref[...]) * (1.0 + sc_ref[0]) + sh_ref[0]
    o_ref[...] = _dot(h.astype(BF16), w_ref[...])


def _normmod_mm(x2d, sc, sh, g, w, tq, tm, tn, per_row):
    m, d = x2d.shape
    n = w.shape[1]
    nt = tq // tm
    if per_row:
        mod_spec = pl.BlockSpec((1, tm, d), lambda i, j: (0, i, 0))
    else:
        mod_spec = pl.BlockSpec((1, 1, d), lambda i, j: (i // nt, 0, 0))
    return pl.pallas_call(
        _normmod_mm_kernel,
        grid=(m // tm, n // tn),
        in_specs=[pl.BlockSpec((tm, d), lambda i, j: (i, 0)), mod_spec, mod_spec,
                  pl.BlockSpec((1, d), lambda i, j: (0, 0)), pl.BlockSpec((d, tn), lambda i, j: (0, j))],
        out_specs=pl.BlockSpec((tm, tn), lambda i, j: (i, j)),
        out_shape=jax.ShapeDtypeStruct((m, n), F32),
        compiler_params=_cp(("parallel", "parallel")),
        name="odd_in_proj",
    )(x2d, sc, sh, g, w)


def _hgrn_consts(c):
    nlev = int(np.log2(c))
    t = np.arange(c)
    tril = (t[None, :] <= t[:, None]).astype(np.float32)
    mats = [tril]
    masks = [np.eye(c, dtype=np.float32)]
    for lev in range(1, nlev + 1):
        w = 2 ** (lev - 1)
        piv = (t // (2 * w)) * (2 * w) + w - 1
        mats.append(tril - tril[piv])
        same = (t[:, None] // (2 * w)) == (t[None, :] // (2 * w))
        right = ((t // w) % 2 == 1)[:, None]
        left = ((t // w) % 2 == 0)[None, :]
        masks.append((same & right & left).astype(np.float32))
    return np.concatenate(mats, axis=0), np.stack(masks), nlev


def _split2(x):
    a = x.astype(BF16)
    b = (x - a.astype(F32)).astype(BF16)
    return a, b


def _hgrn_kernel(zq_ref, zf_ref, zi_ref, zg_ref, lb_ref, og_ref, cm_ref, bm_ref, o_ref, st_ref, s_ref,
                 *, c, nlev, q_scale, hpb, kk, vd):
    ci = pl.program_id(2)

    @pl.when(ci == 0)
    def _():
        s_ref[...] = jnp.zeros(s_ref.shape, F32)

    for hh in range(hpb):
        ks = slice(hh * kk, (hh + 1) * kk)
        vs = slice(hh * vd, (hh + 1) * vd)
        lb = lb_ref[hh]
        zf = zf_ref[:, ks]
        e = jnp.exp(-jnp.abs(zf))
        inv = 1.0 / (1.0 + e)
        sig_pos = jnp.where(zf >= 0, inv, e * inv)
        sig_neg = jnp.where(zf >= 0, e * inv, inv)
        logf = jnp.log(lb + (1.0 - lb) * sig_pos)
        k = (1.0 - lb) * sig_neg
        q = _silu(zq_ref[:, ks]) * q_scale
        v = zi_ref[:, vs].astype(BF16)

        a2 = jnp.concatenate(_split2(logf), axis=1)
        gd = _dot(cm_ref[...], a2)
        gd = gd[:, 0:kk] + gd[:, kk:2 * kk]
        gcum = gd[0:c]
        glast = gcum[c - 1:c]

        amat = bm_ref[0] * lax.dot_general(q.astype(BF16), k.astype(BF16), NT, preferred_element_type=F32)
        for lev in range(1, nlev + 1):
            ed = jnp.exp(-jnp.abs(gd[lev * c:(lev + 1) * c]))
            amat = amat + bm_ref[lev] * lax.dot_general((q * ed).astype(BF16), (k * ed).astype(BF16), NT,
                                                        preferred_element_type=F32)
        s_prev = s_ref[hh]
        o = _dot(amat.astype(BF16), v) + _dot((q * jnp.exp(gcum)).astype(BF16), s_prev.astype(BF16))
        kdt = (k * jnp.exp(glast - gcum)).T.astype(BF16)
        decay_col = jnp.broadcast_to(jnp.exp(glast), (kk, kk)).T
        s_new = decay_col * s_prev + _dot(kdt, v)
        s_ref[hh] = s_new
        o_ref[:, vs] = (_rms(o, og_ref[...]) * _silu(zg_ref[:, vs])).astype(BF16)

    @pl.when(ci == pl.num_programs(2) - 1)
    def _():
        st_ref[...] = s_ref[...]


def _hgrn_prompt(z, lb, og, b, t, nh, c, q_scale):
    m = z.shape[0]
    kk = lb.shape[-1]
    vd = og.shape[-1]
    nc = t // c
    hpb = _pick(nh, (4, 2, 1))
    ng = nh // hpb
    cm, bm, nlev = _hgrn_consts(c)
    cm = jnp.asarray(cm, BF16)
    bm = jnp.asarray(bm, F32)
    seg = lambda s, w: pl.BlockSpec((c, hpb * w), lambda bi, h, ci: (bi * nc + ci, s * ng + h))
    kern = functools.partial(_hgrn_kernel, c=c, nlev=nlev, q_scale=q_scale, hpb=hpb, kk=kk, vd=vd)
    return pl.pallas_call(
        kern,
        grid=(b, ng, nc),
        in_specs=[seg(0, kk), seg(1, kk), seg(2, vd), seg(3, vd),
                  pl.BlockSpec((hpb, 1, kk), lambda bi, h, ci: (h, 0, 0)),
                  pl.BlockSpec((1, vd), lambda bi, h, ci: (0, 0)),
                  pl.BlockSpec(cm.shape, lambda bi, h, ci: (0, 0)),
                  pl.BlockSpec(bm.shape, lambda bi, h, ci: (0, 0, 0))],
        out_specs=(pl.BlockSpec((c, hpb * vd), lambda bi, h, ci: (bi * nc + ci, h)),
                   pl.BlockSpec((None, hpb, kk, vd), lambda bi, h, ci: (bi, h, 0, 0))),
        out_shape=(jax.ShapeDtypeStruct((m, nh * vd), BF16),
                   jax.ShapeDtypeStruct((b, nh, kk, vd), F32)),
        scratch_shapes=[pltpu.VMEM((hpb, kk, vd), F32)],
        compiler_params=_cp(("parallel", "parallel", "arbitrary")),
        name="hgrn_prompt",
    )(z, z, z, z, lb, og, cm, bm)


def _hgrn_step_kernel(z_ref, lb_ref, og_ref, s_ref, o_ref, st_ref, *, nh, kk, vd, bt, q_scale):
    z = z_ref[...]
    hf = nh * kk
    for h in range(nh):
        lb = lb_ref[h]
        zf = z[:, hf + h * kk:hf + (h + 1) * kk]
        e = jnp.exp(-jnp.abs(zf))
        inv = 1.0 / (1.0 + e)
        f = lb + (1.0 - lb) * jnp.where(zf >= 0, inv, e * inv)
        k = (1.0 - lb) * jnp.where(zf >= 0, e * inv, inv)
        q = _silu(z[:, h * kk:(h + 1) * kk]) * q_scale
        v = z[:, 2 * hf + h * vd:2 * hf + (h + 1) * vd]
        gate = _silu(z[:, 2 * hf + nh * vd + h * vd:2 * hf + nh * vd + (h + 1) * vd])
        ft, kt, qt = f.T, k.T, q.T
        for j in range(bt):
            s_new = ft[:, j:j + 1] * s_ref[j, h] + kt[:, j:j + 1] * v[j:j + 1, :]
            st_ref[j, h] = s_new
            o = jnp.sum(qt[:, j:j + 1] * s_new, axis=0, keepdims=True)
            o_ref[j:j + 1, h * vd:(h + 1) * vd] = (_rms(o, og_ref[...]) * gate[j:j + 1, :]).astype(BF16)


def _hgrn_step(z, lb, og, state, nh, q_scale):
    bsz = z.shape[0]
    kk, vd = state.shape[2], state.shape[3]
    bt = 8
    kern = functools.partial(_hgrn_step_kernel, nh=nh, kk=kk, vd=vd, bt=bt, q_scale=q_scale)
    return pl.pallas_call(
        kern,
        grid=(bsz // bt,),
        in_specs=[pl.BlockSpec((bt, z.shape[1]), lambda i: (i, 0)),
                  pl.BlockSpec(lb.shape, lambda i: (0, 0, 0)),
                  pl.BlockSpec((1, vd), lambda i: (0, 0)),
                  pl.BlockSpec((bt, nh, kk, vd), lambda i: (i, 0, 0, 0))],
        out_specs=(pl.BlockSpec((bt, nh * vd), lambda i: (i, 0)),
                   pl.BlockSpec((bt, nh, kk, vd), lambda i: (i, 0, 0, 0))),
        out_shape=(jax.ShapeDtypeStruct((bsz, nh * vd), BF16),
                   jax.ShapeDtypeStruct(state.shape, F32)),
        compiler_params=_cp(("parallel",)),
        name="hgrn_step",
    )(z, lb, og, state)


PAGE_RING_SLOTS = 4


def _fetch_pages(pt_ref, cache_hbm, buf, sem, pp):
    nslot = buf.shape[0]
    ahead = nslot - 1
    nsteps = pl.num_programs(0) * pl.num_programs(1)
    s = pl.program_id(0) * pl.num_programs(1) + pl.program_id(1)

    def copy(step, k):
        slot = step % nslot
        return pltpu.make_async_copy(cache_hbm.at[pt_ref[step * pp + k]], buf.at[slot, k], sem.at[slot])

    def start(step):
        for k in range(pp):
            copy(step, k).start()

    @pl.when(s == 0)
    def _():
        for d in range(ahead):
            start(d)

    @pl.when(s + ahead < nsteps)
    def _():
        start(s + ahead)

    for k in range(pp):
        copy(s, k).wait()
    slot = s % nslot
    return [buf[slot, k] for k in range(pp)]


def _stage_pages(pages):
    return jnp.concatenate([pg.astype(BF16) for pg in pages], axis=1)


def _paged_softmax_pv(s_ref, pg_ref, s_new, new_v, v_rows):
    ng, nh, w = s_ref.shape
    mp = s_ref[0]
    for gi in range(1, ng):
        mp = jnp.maximum(mp, s_ref[gi])
    m = jnp.maximum(jnp.max(_fold_lanes(mp, jnp.maximum), axis=1, keepdims=True),
                    jnp.max(s_new, axis=1, keepdims=True))
    mbt = jnp.concatenate([jnp.broadcast_to(m, (nh, LANES))] * (w // LANES), axis=1)
    lp = jnp.zeros((nh, LANES), F32)
    acc = None
    for gi in range(ng):
        p = jnp.exp2(s_ref[gi] - mbt)
        lp = lp + _fold_lanes(p, jnp.add)
        pv = lax.dot_general(p.astype(BF16), pg_ref[gi, v_rows, :], NT, preferred_element_type=F32)
        acc = pv if acc is None else acc + pv
    p_new = jnp.exp2(s_new - m)
    l = jnp.sum(lp, axis=1, keepdims=True) + jnp.sum(p_new, axis=1, keepdims=True)
    return (acc + _dot(p_new.astype(BF16), new_v)) / l


def _mla_dec2_kernel(pt_ref, q_ref, new_ref, wuv_ref, cache_ref, o_ref, pg_ref, s_ref, buf_ref, sem_ref,
                     *, pp, r, vd, nh):
    g = pl.program_id(1)
    q = q_ref[...]
    slab = _stage_pages(_fetch_pages(pt_ref, cache_ref, buf_ref, sem_ref, pp))
    pg_ref[g] = slab
    s_ref[g] = _dot(q, slab)

    @pl.when(g == pl.num_programs(1) - 1)
    def _():
        new = new_ref[...]
        s_new = lax.dot_general(q, new, NT, preferred_element_type=F32)
        s_new = jnp.where(lax.broadcasted_iota(I32, s_new.shape, 1) == 0, s_new, NEG)
        lat = _paged_softmax_pv(s_ref, pg_ref, s_new, new[:, 0:r], slice(0, r)).astype(BF16)
        for hd in range(nh):
            o_ref[0:1, hd * vd:(hd + 1) * vd] = _dot(lat[hd:hd + 1], wuv_ref[hd])
        o_ref[1:8, :] = jnp.zeros((7, nh * vd), F32)


def _mla_decode2(page_table, qdec, newrow, wuv, cache, pp):
    bsz, nh, dq = qdec.shape
    n_pages = page_table.shape[1]
    psz = cache.shape[2]
    r, vd = wuv.shape[1], wuv.shape[2]
    ng = n_pages // pp
    assert bsz * ng >= PAGE_RING_SLOTS
    kern = functools.partial(_mla_dec2_kernel, pp=pp, r=r, vd=vd, nh=nh)
    grid_spec = pltpu.PrefetchScalarGridSpec(
        num_scalar_prefetch=1,
        grid=(bsz, ng),
        in_specs=[pl.BlockSpec((None, nh, dq), lambda b, g, ptr: (b, 0, 0)),
                  pl.BlockSpec((None, 8, dq), lambda b, g, ptr: (b, 0, 0)),
                  pl.BlockSpec(wuv.shape, lambda b, g, ptr: (0, 0, 0)),
                  pl.BlockSpec(memory_space=pl.ANY)],
        out_specs=pl.BlockSpec((None, 8, nh * vd), lambda b, g, ptr: (b, 0, 0)),
        scratch_shapes=[pltpu.VMEM((ng, dq, pp * psz), BF16), pltpu.VMEM((ng, nh, pp * psz), F32),
                        pltpu.VMEM((PAGE_RING_SLOTS, pp, dq, psz), F32),
                        pltpu.SemaphoreType.DMA((PAGE_RING_SLOTS,))],
    )
    out = pl.pallas_call(
        kern,
        grid_spec=grid_spec,
        out_shape=jax.ShapeDtypeStruct((bsz, 8, nh * vd), F32),
        compiler_params=_cp(("arbitrary", "arbitrary")),
        name="mla_decode",
    )(page_table.reshape(-1), qdec, newrow, wuv, cache)
    return out[:, 0, :]


def _idx_score_kernel(pt_ref, iq_ref, iw_ref, cache_ref, keys_ref, buf_ref, sem_ref, *, pp):
    iq = iq_ref[...]
    iw = iw_ref[...]
    d = _dot(iq, _stage_pages(_fetch_pages(pt_ref, cache_ref, buf_ref, sem_ref, pp)))
    keys_ref[...] = _sort_key(jnp.sum(iw * jnp.maximum(d, 0.0), axis=0, keepdims=True))


def _idx_select_kernel(keys_ref, iq_ref, iw_ref, iknew_ref, bias_ref, bnew_ref, *, eb, nsel, pos_bits):
    ng, w = keys_ref.shape[1], keys_ref.shape[2]
    pos = lax.broadcasted_iota(I32, (ng, w), 0) * w + lax.broadcasted_iota(I32, (ng, w), 1)
    knew = []
    for e in range(eb):
        d = lax.dot_general(iq_ref[e], iknew_ref[e], NT, preferred_element_type=F32)
        snew = _sort_key(jnp.sum(iw_ref[e] * jnp.maximum(d, 0.0), axis=0, keepdims=True))
        knew.append(snew[0:1, 0:1])

    def count(pred):
        return jnp.sum(jnp.sum(pred.astype(I32), axis=0, keepdims=True), axis=1, keepdims=True)

    def bit_body(b, thrs):
        out = []
        for e in range(eb):
            cand = thrs[e] + lax.shift_left(jnp.int32(1), 31 - b)
            cnt = count(keys_ref[e] >= cand) + (knew[e] >= cand).astype(I32)
            out.append(jnp.where(cnt >= nsel, cand, thrs[e]))
        return tuple(out)

    thrs = lax.fori_loop(0, 32, bit_body, tuple(jnp.full((1, 1), INT_MIN, I32) for _ in range(eb)))
    needs = tuple(nsel - count(keys_ref[e] > thrs[e]) - (knew[e] > thrs[e]).astype(I32) for e in range(eb))

    def jbit_body(b, jcs):
        out = []
        for e in range(eb):
            cand = jcs[e] + lax.shift_left(jnp.int32(1), pos_bits - 1 - b)
            cnt = count((keys_ref[e] == thrs[e]) & (pos < cand))
            out.append(jnp.where(cnt <= needs[e], cand, jcs[e]))
        return tuple(out)

    jcs = lax.fori_loop(0, pos_bits, jbit_body, tuple(jnp.zeros((1, 1), I32) for _ in range(eb)))
    for e in range(eb):
        bias_ref[e] = jnp.where(_selected(keys_ref[e], pos, thrs[e], jcs[e]), 0.0, NEG)
        ties_taken = count((keys_ref[e] == thrs[e]) & (pos < jcs[e]))
        new_sel = (knew[e] > thrs[e]) | ((knew[e] == thrs[e]) & (ties_taken < needs[e]))
        slot0 = (lax.broadcasted_iota(I32, (8, LANES), 0) == 0) & (lax.broadcasted_iota(I32, (8, LANES), 1) == 0)
        bnew_ref[e] = jnp.where(slot0 & jnp.broadcast_to(new_sel, (8, LANES)), 0.0, NEG)


def _idx_decode2(page_table, iqd, iwd, iknew, cache, nsel, pp):
    bsz, _, idim = iqd.shape
    n_pages = page_table.shape[1]
    psz = cache.shape[2]
    ng, w = n_pages // pp, pp * psz
    pos_bits = int(n_pages * psz).bit_length()
    assert bsz * ng >= PAGE_RING_SLOTS
    grid_spec = pltpu.PrefetchScalarGridSpec(
        num_scalar_prefetch=1,
        grid=(bsz, ng),
        in_specs=[pl.BlockSpec((None, 8, idim), lambda b, g, ptr: (b, 0, 0)),
                  pl.BlockSpec((None, 8, 1), lambda b, g, ptr: (b, 0, 0)),
                  pl.BlockSpec(memory_space=pl.ANY)],
        out_specs=pl.BlockSpec((None, None, 1, w), lambda b, g, ptr: (b, g, 0, 0)),
        scratch_shapes=[pltpu.VMEM((PAGE_RING_SLOTS, pp, idim, psz), F32),
                        pltpu.SemaphoreType.DMA((PAGE_RING_SLOTS,))],
    )
    keys = pl.pallas_call(
        functools.partial(_idx_score_kernel, pp=pp),
        grid_spec=grid_spec,
        out_shape=jax.ShapeDtypeStruct((bsz, ng, 1, w), I32),
        compiler_params=_cp(("arbitrary", "arbitrary")),
        name="idx_score",
    )(page_table.reshape(-1), iqd, iwd, cache)

    eb = _pick(bsz, (8, 4, 2, 1))
    blk = lambda *shape: pl.BlockSpec((eb,) + shape, lambda i: (i,) + tuple(0 for _ in shape))
    return pl.pallas_call(
        functools.partial(_idx_select_kernel, eb=eb, nsel=nsel, pos_bits=pos_bits),
        grid=(bsz // eb,),
        in_specs=[blk(ng, w), blk(8, idim), blk(8, 1), blk(8, idim)],
        out_specs=(blk(ng, w), blk(8, LANES)),
        out_shape=(jax.ShapeDtypeStruct((bsz, ng, w), F32), jax.ShapeDtypeStruct((bsz, 8, LANES), F32)),
        compiler_params=_cp(("parallel",)),
        name="idx_select",
    )(keys.reshape(bsz, ng, w), iqd, iwd, iknew)


def _dsa_dec2_kernel(pt_ref, q_ref, new_ref, bias_ref, bnew_ref, cache_ref, o_ref, pg_ref, s_ref, buf_ref,
                     sem_ref, *, pp, hd, gsz):
    g = pl.program_id(1)
    q = q_ref[...]
    kw = 2 * hd
    slab = _stage_pages(_fetch_pages(pt_ref, cache_ref, buf_ref, sem_ref, pp))
    pg_ref[g] = slab
    s_ref[g] = _dot(q, slab[0:kw, :]) + bias_ref[g]

    @pl.when(g == pl.num_programs(1) - 1)
    def _():
        new = new_ref[...]
        s_new = lax.dot_general(q, new[:, 0:kw], NT, preferred_element_type=F32)
        s_new = s_new + bnew_ref[0:1, 0:8]
        o = _paged_softmax_pv(s_ref, pg_ref, s_new, new[:, kw:2 * kw], slice(kw, 2 * kw))
        rowi = lax.broadcasted_iota(I32, o.shape, 0)
        o_ref[...] = jnp.where(rowi < gsz, o, pltpu.roll(o, hd, 1))


def _dsa_decode2(page_table, qbd, newkv, bias, bias_new, cache, pp, hd, gsz):
    bsz = qbd.shape[0]
    n_pages = page_table.shape[1]
    psz = cache.shape[2]
    ng, w = n_pages // pp, pp * psz
    assert bsz * ng >= PAGE_RING_SLOTS
    kern = functools.partial(_dsa_dec2_kernel, pp=pp, hd=hd, gsz=gsz)
    grid_spec = pltpu.PrefetchScalarGridSpec(
        num_scalar_prefetch=1,
        grid=(bsz, ng),
        in_specs=[pl.BlockSpec((None, 8, 2 * hd), lambda b, g, ptr: (b, 0, 0)),
                  pl.BlockSpec((None, 8, 4 * hd), lambda b, g, ptr: (b, 0, 0)),
                  pl.BlockSpec((None, ng, 1, w), lambda b, g, ptr: (b, 0, 0, 0)),
                  pl.BlockSpec((None, 8, LANES), lambda b, g, ptr: (b, 0, 0)),
                  pl.BlockSpec(memory_space=pl.ANY)],
        out_specs=pl.BlockSpec((None, 8, 2 * hd), lambda b, g, ptr: (b, 0, 0)),
        scratch_shapes=[pltpu.VMEM((ng, 4 * hd, w), BF16), pltpu.VMEM((ng, 8, w), F32),
                        pltpu.VMEM((PAGE_RING_SLOTS, pp, 4 * hd, psz), F32),
                        pltpu.SemaphoreType.DMA((PAGE_RING_SLOTS,))],
    )
    return pl.pallas_call(
        kern,
        grid_spec=grid_spec,
        out_shape=jax.ShapeDtypeStruct((bsz, 8, 2 * hd), F32),
        compiler_params=_cp(("arbitrary", "arbitrary")),
        name="dsa_decode",
    )(page_table.reshape(-1), qbd, newkv, bias.reshape(bsz, ng, 1, w), bias_new, cache)


def _rope_tables(pos, half):
    inv_freq = ROPE_THETA ** (-jnp.arange(half, dtype=F32) / half)
    ang = pos.astype(F32)[:, None] * inv_freq[None, :]
    cos, sin = jnp.cos(ang), jnp.sin(ang)
    reps = LANES // (2 * half)
    return (jnp.tile(jnp.concatenate([cos, cos], axis=1), (1, reps)),
            jnp.tile(jnp.concatenate([-sin, sin], axis=1), (1, reps)))


def _pick(n, pref):
    for c in pref:
        if n % c == 0:
            return c
    return n


def kernel(x_prompt, x_sample, cache_mla, cache_dsa_kv, cache_idx, state_hgrn, page_table, c_prompt, c_sample,
           w_ada, b_ada, norm_mix_g, norm_ffn_g, w_in_even, mla_q_norm_g, mla_kv_norm_g, mla_w_q_nope,
           mla_w_q_rope, mla_w_uk, mla_w_uv, w_out_even, w_in_odd, hgrn_lower_bounds, hgrn_out_norm_g,
           w_out_odd, w_ffn_gate, w_ffn_up, w_ffn_down, final_norm_g):
    depth, d, _ = w_ada.shape
    bp, tp, _ = x_prompt.shape
    bs, ts, _ = x_sample.shape
    assert ts == 1
    q_lora, mla_heads, mla_nope = mla_w_q_nope.shape[1:]
    mla_rope = mla_w_q_rope.shape[3]
    kv_lora, _, mla_v = mla_w_uv.shape[1:]
    dsa_kv_heads, dsa_hd = cache_dsa_kv.shape[4:]
    idx_dim = cache_idx.shape[3]
    n_pages = page_table.shape[1]
    psz = cache_mla.shape[2]
    past_len = n_pages * psz
    sizes_tail = w_in_even.shape[2] - (q_lora + kv_lora + mla_rope + 2 * dsa_kv_heads * dsa_hd + idx_dim)
    dsa_heads = (w_out_even.shape[1] - mla_heads * mla_v) // dsa_hd
    idx_heads = (sizes_tail - dsa_heads * dsa_hd) // (idx_dim + 1)
    dm = EvenDims(q_lora, kv_lora, mla_rope, mla_heads, dsa_heads, dsa_kv_heads, dsa_hd, idx_heads, idx_dim)
    gsz = dsa_heads // dsa_kv_heads
    hg_f = hgrn_lower_bounds.shape[1]
    hg_heads = state_hgrn.shape[2]
    hg_k = state_hgrn.shape[3]
    hg_v = state_hgrn.shape[4]
    log2e = float(np.log2(np.e))
    scales = ((mla_nope + mla_rope) ** -0.5 * log2e, dsa_hd ** -0.5 * log2e, idx_dim ** -0.5, idx_heads ** -0.5)
    hg_q_scale = hg_k ** -0.5

    lb_all = jax.nn.softmax(hgrn_lower_bounds.astype(F32), axis=0)
    lb_all = jnp.cumsum(lb_all, axis=0) - lb_all[0]

    c_all = jnp.concatenate([c_prompt, c_sample], axis=0)
    mods = _adaln(c_all, w_ada, b_ada)

    def mods_for(l, sample):
        mm = mods[l, bp:] if sample else mods[l, :bp]
        parts = [mm[:, k * d:(k + 1) * d] for k in range(6)]
        return [p[None] if sample else p[:, None, :] for p in parts]

    bf = lambda a: a.astype(BF16)
    row1 = lambda a: a.reshape(1, -1).astype(F32)

    def even_weights(la):
        w = w_in_even[la]
        splits = np.cumsum([q_lora, kv_lora, mla_rope, dsa_heads * dsa_hd, dsa_kv_heads * dsa_hd,
                            dsa_kv_heads * dsa_hd, idx_heads * idx_dim, idx_dim])
        wq, wckv, wkr, wdq, wdk, wdv, wiq, wik, wiw = jnp.split(w, splits, axis=1)
        padc = lambda a, n: jnp.pad(a, ((0, 0), (0, n - a.shape[1])))
        w_cat = jnp.concatenate([wq, wckv, padc(wkr, LANES), wdq, wdk, wdv, wiq,
                                 padc(jnp.concatenate([wik, wiw], axis=1), LANES)], axis=1)
        assert w_cat.shape[1] == dm.n
        wabs = _absorb(mla_w_q_nope[la], mla_w_uk[la])
        wqr = mla_w_q_rope[la].reshape(q_lora, mla_heads * mla_rope)
        wuv = jnp.transpose(mla_w_uv[la], (1, 0, 2))
        return bf(w_cat), bf(wabs), bf(wqr), bf(wuv)

    def run(x, sample):
        b, t, _ = x.shape
        m = b * t
        x2d = x.reshape(m, d)
        if sample:
            bq, tq, per_row = 1, m, True
            tm = _pick(m, (128,))
            pos = jnp.full((m,), past_len, I32)
        else:
            bq, tq, per_row = b, t, False
            tm = _pick(t, (512, 256, 128))
            pos = jnp.arange(t)
        tabs = (*_rope_tables(pos, dsa_hd // 2), *_rope_tables(pos, mla_rope // 2))
        rows_mla, rows_dsa, rows_idx, states = [], [], [], []
        for l in range(depth):
            sh1, sc1, g1, sh2, sc2, g2 = mods_for(l, sample)
            nmg, nfg = row1(norm_mix_g[l]), row1(norm_ffn_g[l])
            final = l == depth - 1
            if l % 2 == 0:
                la = l // 2
                w_cat, wabs, wqr, wuv = even_weights(la)
                (qcat, mrow, kv16, dsarow, dq, dk, dvt, iq, ik, ik16, iwt) = _even_project(
                    x2d, sc1, sh1, nmg, w_cat, row1(mla_q_norm_g[la]), row1(mla_kv_norm_g[la]), wqr, wabs,
                    tabs, dm, bq, tq, tm, per_row, scales)
                if sample:
                    nsel = min(TOPK_MAX, (past_len + t) // 4)
                    pp = _pick(n_pages, (16, 8))
                    pad8 = lambda a: jnp.pad(a[:, None, :], ((0, 0), (0, 7), (0, 0)))
                    qdec = jnp.transpose(qcat[0], (1, 0, 2))
                    page_t = lambda c: jnp.transpose(c, (0, 2, 1))
                    mla_o = _mla_decode2(page_table, qdec, pad8(kv16), wuv, page_t(cache_mla[la]), pp)
                    iqd = jnp.pad(jnp.transpose(iq[0], (1, 0, 2)), ((0, 0), (0, 8 - idx_heads), (0, 0)))
                    iwd = jnp.transpose(iwt[0])[:, :, None]
                    iwd = jnp.where(jnp.arange(8)[None, :, None] < idx_heads, iwd, 0.0)
                    bias, bias_new = _idx_decode2(page_table, iqd, iwd, pad8(ik16), page_t(cache_idx[la]), nsel, pp)
                    dqd = jnp.transpose(dq[0], (1, 0, 2))
                    zeros = jnp.zeros_like(dqd[:, :gsz])
                    qbd = jnp.concatenate(
                        [jnp.concatenate([dqd[:, :gsz], zeros], axis=2),
                         jnp.concatenate([zeros, dqd[:, gsz:]], axis=2)], axis=1)
                    cache_kv = jnp.transpose(cache_dsa_kv[la], (0, 2, 3, 4, 1)).reshape(-1, 4 * dsa_hd, psz)
                    dsa_o8 = _dsa_decode2(page_table, qbd, pad8(bf(dsarow)), bias, bias_new, cache_kv, pp, dsa_hd,
                                          gsz)
                    dsa_o = dsa_o8[:, :, :dsa_hd].reshape(b, dsa_heads * dsa_hd)
                    mla_o, dsa_o = bf(mla_o), bf(dsa_o)
                else:
                    nsel = min(TOPK_MAX, t // 4)
                    mla_o = _mla_prompt(qcat, kv16.reshape(b, t, -1), wuv, _pick(t, (256, 128))).reshape(m, -1)
                    dsa_o = _dsa_prompt(iq, iwt, ik16, dq, dk, dvt, nsel, _pick(t, (256, 128))).reshape(m, -1)
                n_mla = mla_heads * mla_v
                mixes = [(mla_o, bf(w_out_even[la][:n_mla])), (dsa_o, bf(w_out_even[la][n_mla:]))]
                rows_mla.append(mrow.reshape(b, t, -1))
                if sample:
                    rows_dsa.append(dsarow.reshape(b, t, 2, dsa_kv_heads, dsa_hd))
                else:
                    rows_dsa.append(jnp.transpose(dsarow.reshape(b, 2, dsa_kv_heads, dsa_hd, t), (0, 4, 1, 2, 3)))
                rows_idx.append(ik.reshape(b, t, -1))
            else:
                lr = l // 2
                z = _normmod_mm(x2d, sc1, sh1, nmg, bf(w_in_odd[lr]), tq, tm, _pick(w_in_odd.shape[2], (2048,)),
                                per_row)
                lb = lb_all[l].reshape(hg_heads, 1, hg_k)
                og = row1(hgrn_out_norm_g[lr])
                if sample:
                    o_g, st = _hgrn_step(z, lb, og, state_hgrn[lr].astype(F32), hg_heads, hg_q_scale)
                else:
                    o_g, st = _hgrn_prompt(z, lb, og, b, t, hg_heads, _pick(t, (256, 128)), hg_q_scale)
                mixes = [(o_g, bf(w_out_odd[lr]))]
                states.append(st)
            x2d = _mix_ffn(x2d, mixes, g1, nfg, sc2, sh2, g2, bf(w_ffn_gate[l]), bf(w_ffn_up[l]),
                           bf(w_ffn_down[l]), row1(final_norm_g), tq, tm, per_row, final)
        return (x2d.reshape(b, t, d), jnp.stack(rows_mla), jnp.stack(rows_dsa), jnp.stack(rows_idx),
                jnp.stack(states))

    y_p, mla_p, dsa_p, idx_p, hg_p = run(x_prompt, False)
    y_s, mla_s, dsa_s, idx_s, hg_s = run(x_sample, True)
    return (y_p, y_s, mla_p, mla_s, dsa_p, dsa_s, idx_p, idx_s, hg_p, hg_s)
```

```python
import functools

import jax
import jax.numpy as jnp
import numpy as np
from jax import lax
from jax.experimental import pallas as pl
from jax.experimental.pallas import tpu as pltpu

F32 = jnp.float32
BF16 = jnp.bfloat16
I32 = jnp.int32
I16 = jnp.int16

EPS = 1e-6
ROPE_THETA = 10000.0
TOPK_MAX = 256
NEG = -1e30
INT_MIN = -(2 ** 31)
LANES = 128
DSA_KEY_CHUNK = 512
MIB = 1024 * 1024

NT = (((1,), (1,)), ((), ()))
TN = (((0,), (0,)), ((), ()))


def _cp(sem, vmem_mib=48):
    return pltpu.CompilerParams(dimension_semantics=sem, vmem_limit_bytes=vmem_mib * MIB)


def _sigmoid(x):
    return 1.0 / (1.0 + jnp.exp(-x))


def _silu(x):
    return x * _sigmoid(x)


def _rms(x, g):
    return x * lax.rsqrt(jnp.mean(x * x, axis=-1, keepdims=True) + EPS) * g


def _dot(a, b):
    return jnp.dot(a, b, preferred_element_type=F32)


def _rope_cols(x, cos, sin_signed, half):
    outs = []
    lane = lax.broadcasted_iota(I32, (x.shape[0], LANES), 1)
    first = (lane % (2 * half)) < half
    for c in range(x.shape[1] // LANES):
        xc = x[:, c * LANES:(c + 1) * LANES]
        rot = jnp.where(first, pltpu.roll(xc, LANES - half, 1), pltpu.roll(xc, half, 1))
        outs.append(xc * cos + rot * sin_signed)
    return outs[0] if len(outs) == 1 else jnp.concatenate(outs, axis=1)


def _ada_kernel(c_ref, w_ref, b_ref, o_ref):
    a = _silu(c_ref[...]).astype(BF16)
    o_ref[...] = _dot(a, w_ref[...].astype(BF16)) + b_ref[...]


def _adaln(c_all, w_ada, b_ada):
    depth, d, n = w_ada.shape
    bc = c_all.shape[0]
    tn = n // 4
    return pl.pallas_call(
        _ada_kernel,
        grid=(depth, n // tn),
        in_specs=[pl.BlockSpec((bc, d), lambda l, j: (0, 0)),
                  pl.BlockSpec((None, d, tn), lambda l, j: (l, 0, j)),
                  pl.BlockSpec((None, 1, tn), lambda l, j: (l, 0, j))],
        out_specs=pl.BlockSpec((None, bc, tn), lambda l, j: (l, 0, j)),
        out_shape=jax.ShapeDtypeStruct((depth, bc, n), F32),
        compiler_params=_cp(("parallel", "parallel")),
        name="adaln",
    )(c_all, w_ada, b_ada.reshape(depth, 1, n))


def _absorb_kernel(a_ref, b_ref, o_ref):
    o_ref[...] = lax.dot_general(a_ref[...].astype(BF16), b_ref[...].astype(BF16), NT,
                                 preferred_element_type=F32)


def _absorb(w_q_nope, w_uk):
    a = jnp.transpose(w_q_nope, (1, 0, 2))
    b = jnp.transpose(w_uk, (1, 0, 2))
    h, rq, dn = a.shape
    rkv = b.shape[1]
    out = pl.pallas_call(
        _absorb_kernel,
        grid=(h,),
        in_specs=[pl.BlockSpec((None, rq, dn), lambda i: (i, 0, 0)),
                  pl.BlockSpec((None, rkv, dn), lambda i: (i, 0, 0))],
        out_specs=pl.BlockSpec((None, rq, rkv), lambda i: (i, 0, 0)),
        out_shape=jax.ShapeDtypeStruct((h, rq, rkv), F32),
        compiler_params=_cp(("parallel",)),
        name="mla_absorb",
    )(a, b)
    return jnp.transpose(out, (1, 0, 2)).reshape(rq, h * rkv)


class EvenDims:
    def __init__(self, q_lora, kv_lora, mla_rope, mla_heads, dsa_heads, dsa_kv_heads, dsa_hd, idx_heads, idx_dim):
        assert q_lora % LANES == 0 and kv_lora == LANES and mla_rope <= LANES
        assert dsa_hd == 64 and idx_dim == 64 and dsa_kv_heads == 2 and idx_heads <= 8
        self.q_lora, self.kv_lora, self.mla_rope, self.mla_heads = q_lora, kv_lora, mla_rope, mla_heads
        self.dsa_heads, self.dsa_kv_heads, self.dsa_hd = dsa_heads, dsa_kv_heads, dsa_hd
        self.idx_heads, self.idx_dim = idx_heads, idx_dim
        self.o_zq = 0
        self.o_ckv = q_lora
        self.o_kr = self.o_ckv + kv_lora
        self.o_dq = self.o_kr + LANES
        self.o_dk = self.o_dq + dsa_heads * dsa_hd
        self.o_dv = self.o_dk + dsa_kv_heads * dsa_hd
        self.o_iq = self.o_dv + dsa_kv_heads * dsa_hd
        self.o_ik = self.o_iq + idx_heads * idx_dim
        self.n = self.o_ik + LANES
        self.mla_row = kv_lora + mla_rope


def _even_kernel(x_ref, sc_ref, sh_ref, g_ref, w_ref, qg_ref, kvg_ref, wqr_ref, wabs_ref,
                 c64_ref, s64_ref, c32_ref, s32_ref,
                 qcat_ref, row_ref, kv16_ref, dsarow_ref, dq_ref, dk_ref, dvt_ref, iq_ref, ik_ref, ik16_ref,
                 iwt_ref, *, dm, mla_scale, dsa_scale, idx_scale, idx_w_scale):
    h = _rms(x_ref[...], g_ref[...]) * (1.0 + sc_ref[0]) + sh_ref[0]
    z = _dot(h.astype(BF16), w_ref[...])
    c64, s64, c32, s32 = c64_ref[...], s64_ref[...], c32_ref[...], s32_ref[...]
    tm = z.shape[0]

    cq = _rms(z[:, dm.o_zq:dm.o_zq + dm.q_lora], qg_ref[...]).astype(BF16)
    qlat = _dot(cq, wabs_ref[...]) * mla_scale
    qrope = _rope_cols(_dot(cq, wqr_ref[...]), c32, s32, dm.mla_rope // 2) * mla_scale
    for hd in range(dm.mla_heads):
        qcat_ref[0, hd, :, 0:dm.kv_lora] = qlat[:, hd * dm.kv_lora:(hd + 1) * dm.kv_lora].astype(BF16)
        qcat_ref[0, hd, :, dm.kv_lora:dm.mla_row] = qrope[:, hd * dm.mla_rope:(hd + 1) * dm.mla_rope].astype(BF16)

    ckv = _rms(z[:, dm.o_ckv:dm.o_ckv + dm.kv_lora], kvg_ref[...])
    kr = _rope_cols(z[:, dm.o_kr:dm.o_kr + LANES], c32, s32, dm.mla_rope // 2)[:, 0:dm.mla_rope]
    row_ref[:, 0:dm.kv_lora] = ckv
    row_ref[:, dm.kv_lora:dm.mla_row] = kr
    kv16_ref[:, 0:dm.kv_lora] = ckv.astype(BF16)
    kv16_ref[:, dm.kv_lora:dm.mla_row] = kr.astype(BF16)

    hd_ = dm.dsa_hd
    dq = _rope_cols(z[:, dm.o_dq:dm.o_dk], c64, s64, hd_ // 2) * dsa_scale
    for hh in range(dm.dsa_heads):
        dq_ref[0, hh] = dq[:, hh * hd_:(hh + 1) * hd_].astype(BF16)
    dk = _rope_cols(z[:, dm.o_dk:dm.o_dv], c64, s64, hd_ // 2)
    dv = z[:, dm.o_dv:dm.o_iq]
    dvt = dv.T
    if len(dsarow_ref.shape) == 3:
        dsarow_ref[0, 0:LANES, :] = dk.T
        dsarow_ref[0, LANES:2 * LANES, :] = dvt
    else:
        dsarow_ref[:, 0:LANES] = dk
        dsarow_ref[:, LANES:2 * LANES] = dv
    for gg in range(dm.dsa_kv_heads):
        dk_ref[0, gg] = dk[:, gg * hd_:(gg + 1) * hd_].astype(BF16)
    vck = dvt_ref.shape[3]
    for c in range(tm // vck):
        dvt_ref[0, c] = dvt[:, c * vck:(c + 1) * vck].astype(BF16)

    iq = _rope_cols(z[:, dm.o_iq:dm.o_ik], c64, s64, dm.idx_dim // 2) * idx_scale
    for hh in range(dm.idx_heads):
        iq_ref[0, hh] = iq[:, hh * dm.idx_dim:(hh + 1) * dm.idx_dim].astype(BF16)
    last = z[:, dm.o_ik:dm.o_ik + LANES]
    ik = _rope_cols(last, c64, s64, dm.idx_dim // 2)[:, 0:dm.idx_dim]
    ik_ref[...] = ik
    ik16_ref[...] = ik.astype(BF16)
    iwt_ref[0] = last.T[dm.idx_dim:dm.idx_dim + 8, :] * idx_w_scale


def _even_project(x2d, sc, sh, g, w_in, qg, kvg, wqr, wabs, tabs, dm, bq, tq, tm, per_row, scales):
    rows_t = not per_row
    m, d = x2d.shape
    nt = tq // tm
    vck = min(DSA_KEY_CHUNK, tm)
    nh, kvh, ih = dm.mla_heads, dm.dsa_kv_heads, dm.idx_heads
    row2 = lambda i: (i, 0)
    hm4 = lambda i: (i // nt, 0, i % nt, 0)
    if per_row:
        mod_spec = pl.BlockSpec((1, tm, d), lambda i: (0, i, 0))
    else:
        mod_spec = pl.BlockSpec((1, 1, d), lambda i: (i // nt, 0, 0))
    const = lambda shape: pl.BlockSpec(shape, lambda i: tuple(0 for _ in shape))
    tab_spec = pl.BlockSpec((tm, LANES), lambda i: (i % nt, 0))
    out_shapes = (
        jax.ShapeDtypeStruct((bq, nh, tq, dm.mla_row), BF16),
        jax.ShapeDtypeStruct((m, dm.mla_row), F32),
        jax.ShapeDtypeStruct((m, dm.mla_row), BF16),
        (jax.ShapeDtypeStruct((bq, 2 * LANES, tq), F32) if rows_t
         else jax.ShapeDtypeStruct((m, 2 * LANES), F32)),
        jax.ShapeDtypeStruct((bq, dm.dsa_heads, tq, dm.dsa_hd), BF16),
        jax.ShapeDtypeStruct((bq, kvh, tq, dm.dsa_hd), BF16),
        jax.ShapeDtypeStruct((bq, tq // vck, LANES, vck), BF16),
        jax.ShapeDtypeStruct((bq, ih, tq, dm.idx_dim), BF16),
        jax.ShapeDtypeStruct((m, dm.idx_dim), F32),
        jax.ShapeDtypeStruct((m, dm.idx_dim), BF16),
        jax.ShapeDtypeStruct((bq, 8, tq), F32),
    )
    out_specs = (
        pl.BlockSpec((1, nh, tm, dm.mla_row), hm4),
        pl.BlockSpec((tm, dm.mla_row), row2),
        pl.BlockSpec((tm, dm.mla_row), row2),
        (pl.BlockSpec((1, 2 * LANES, tm), lambda i: (i // nt, 0, i % nt)) if rows_t
         else pl.BlockSpec((tm, 2 * LANES), row2)),
        pl.BlockSpec((1, dm.dsa_heads, tm, dm.dsa_hd), hm4),
        pl.BlockSpec((1, kvh, tm, dm.dsa_hd), hm4),
        pl.BlockSpec((1, tm // vck, LANES, vck), lambda i: (i // nt, i % nt, 0, 0)),
        pl.BlockSpec((1, ih, tm, dm.idx_dim), hm4),
        pl.BlockSpec((tm, dm.idx_dim), row2),
        pl.BlockSpec((tm, dm.idx_dim), row2),
        pl.BlockSpec((1, 8, tm), lambda i: (i // nt, 0, i % nt)),
    )
    kern = functools.partial(_even_kernel, dm=dm, mla_scale=scales[0], dsa_scale=scales[1],
                             idx_scale=scales[2], idx_w_scale=scales[3])
    return pl.pallas_call(
        kern,
        grid=(m // tm,),
        in_specs=[pl.BlockSpec((tm, d), row2), mod_spec, mod_spec, const((1, d)), const(w_in.shape),
                  const(qg.shape), const(kvg.shape), const(wqr.shape), const(wabs.shape),
                  tab_spec, tab_spec, tab_spec, tab_spec],
        out_specs=out_specs,
        out_shape=out_shapes,
        compiler_params=_cp(("parallel",), 56),
        name="even_project",
    )(x2d, sc, sh, g, w_in, qg, kvg, wqr, wabs, *tabs)


def _fold_lanes(x, op):
    out = x[:, 0:LANES]
    for c in range(1, x.shape[1] // LANES):
        out = op(out, x[:, c * LANES:(c + 1) * LANES])
    return out


def _mla_kernel(q_ref, kv_ref, wuv_ref, o_ref, s_ref, mp_ref, lp_ref, acc_ref, *, tq, tk, nh, r, vd):
    qi = pl.program_id(1)
    n = nh * tq
    q = q_ref[0].reshape(n, q_ref.shape[-1])
    nch = (qi * tq + tq + tk - 1) // tk
    reps = tk // LANES

    def keys(j):
        return kv_ref[0, pl.ds(pl.multiple_of(j * tk, tk), tk), :]

    mp_ref[...] = jnp.full(mp_ref.shape, NEG, F32)

    def pass_a(j, c):
        s = lax.dot_general(q, keys(j), NT, preferred_element_type=F32)
        s_ref[j] = s
        mp_ref[...] = jnp.maximum(mp_ref[...], _fold_lanes(s, jnp.maximum))
        return c

    lax.fori_loop(0, nch - 1, pass_a, 0)
    jl = nch - 1
    s = lax.dot_general(q, keys(jl), NT, preferred_element_type=F32)
    qpos = qi * tq + lax.broadcasted_iota(I32, s.shape, 0) % tq
    kpos = jl * tk + lax.broadcasted_iota(I32, s.shape, 1)
    s = jnp.where(kpos <= qpos, s, NEG)
    s_ref[jl] = s
    mp = jnp.maximum(mp_ref[...], _fold_lanes(s, jnp.maximum))
    mb = jnp.broadcast_to(jnp.max(mp, axis=1, keepdims=True), (n, LANES))
    mbt = jnp.concatenate([mb] * reps, axis=1)

    lp_ref[...] = jnp.zeros(lp_ref.shape, F32)
    acc_ref[...] = jnp.zeros(acc_ref.shape, F32)

    def pass_b(j, c):
        p = jnp.exp2(s_ref[j] - mbt)
        lp_ref[...] += _fold_lanes(p, jnp.add)
        acc_ref[...] += _dot(p.astype(BF16), keys(j)[:, 0:r])
        return c

    lax.fori_loop(0, nch, pass_b, 0)
    lat = (acc_ref[...] / jnp.sum(lp_ref[...], axis=1, keepdims=True)).astype(BF16)
    for hd in range(nh):
        o_ref[0, :, hd * vd:(hd + 1) * vd] = _dot(lat[hd * tq:(hd + 1) * tq], wuv_ref[hd]).astype(BF16)


def _mla_prompt(qcat, kv16, wuv, tq):
    b, nh, t, dq = qcat.shape
    r, vd = wuv.shape[1], wuv.shape[2]
    tk = _pick(t, (512, 256, 128))
    n = nh * tq
    kern = functools.partial(_mla_kernel, tq=tq, tk=tk, nh=nh, r=r, vd=vd)
    return pl.pallas_call(
        kern,
        grid=(b, t // tq),
        in_specs=[pl.BlockSpec((1, nh, tq, dq), lambda bi, qi: (bi, 0, qi, 0)),
                  pl.BlockSpec((1, t, dq), lambda bi, qi: (bi, 0, 0)),
                  pl.BlockSpec(wuv.shape, lambda bi, qi: (0, 0, 0))],
        out_specs=pl.BlockSpec((1, tq, nh * vd), lambda bi, qi: (bi, qi, 0)),
        out_shape=jax.ShapeDtypeStruct((b, t, nh * vd), BF16),
        scratch_shapes=[pltpu.VMEM((t // tk, n, tk), F32), pltpu.VMEM((n, LANES), F32),
                        pltpu.VMEM((n, LANES), F32), pltpu.VMEM((n, r), F32)],
        compiler_params=_cp(("parallel", "arbitrary")),
        name="mla_prompt",
    )(qcat, kv16, wuv)


def _sort_key(score):
    b = pltpu.bitcast(score, I32)
    return jnp.where(b < 0, (b ^ jnp.int32(0x7FFFFFFF)) + 1, b)


def _topk_select(read_keys, read_pos, half_ref, chunk, nch, rows, width, nsel, pos_bits):
    low = -(2 ** 15)

    def count(pred):
        def body(j, c):
            return c + jnp.sum(pred(j).astype(I32).reshape(rows // 8, 8, width), axis=0)
        c = lax.fori_loop(0, nch, body, jnp.zeros((8, width), I32))
        return jnp.sum(c, axis=0, keepdims=True)

    def count16(pred):
        def body(j, c):
            ind = pred(half_ref[chunk(j), :]).astype(I16)
            parts = [ind[a:a + 16, :] for a in range(0, rows, 16)]
            while len(parts) > 1:
                parts = [parts[a] + parts[a + 1] for a in range(0, len(parts) - 1, 2)] + parts[len(parts) & ~1:]
            return c + parts[0]
        c = lax.fori_loop(0, nch, body, jnp.zeros((16, width), I16))
        return jnp.sum(c.astype(I32), axis=0, keepdims=True)

    def stage(need):
        def bit_body(b, t):
            cand = t + lax.shift_left(jnp.int32(1), 15 - b)
            c16 = cand.astype(I16)
            return jnp.where(count16(lambda h: h >= c16) >= need, cand, t)
        return lax.fori_loop(0, 16, bit_body, jnp.full((1, width), low, I32))

    def put_hi(j, c):
        half_ref[chunk(j), :] = (read_keys(j) >> 16).astype(I16)
        return c

    lax.fori_loop(0, nch, put_hi, 0)
    t_hi = stage(nsel)
    t_hi16 = t_hi.astype(I16)
    need_lo = nsel - count16(lambda h: h > t_hi16)

    def put_lo(j, c):
        k = read_keys(j)
        in_bucket = half_ref[chunk(j), :] == t_hi16
        half_ref[chunk(j), :] = jnp.where(in_bucket, ((k & 0xFFFF) + low).astype(I16), jnp.int16(low))
        return c

    lax.fori_loop(0, nch, put_lo, 0)
    t_lo = stage(need_lo)
    thr = lax.shift_left(t_hi, 16) + (t_lo - low)
    c_thr = count(lambda j: read_keys(j) >= thr)

    def tie_search():
        need = nsel - count(lambda j: read_keys(j) > thr)

        def jbit_body(b, jc):
            cand = jc + lax.shift_left(jnp.int32(1), pos_bits - 1 - b)
            cnt = count(lambda j: (read_keys(j) == thr) & (read_pos(j) < cand))
            return jnp.where(cnt <= need, cand, jc)

        return lax.fori_loop(0, pos_bits, jbit_body, jnp.zeros((1, width), I32))

    excess = (c_thr > nsel) & (thr > INT_MIN)
    any_excess = jnp.max(excess.astype(I32)) > 0
    jcut = lax.cond(any_excess, tie_search, lambda: jnp.full((1, width), 2 ** pos_bits, I32))
    return thr, jcut


def _selected(keys, pos, thr, jcut):
    return ((keys > thr) | ((keys == thr) & (pos < jcut))) & (keys != INT_MIN)


def _dsa_kernel(iq_ref, iwt_ref, ik_ref, dq_ref, dk_ref, dvt_ref, o_ref, keys_ref, half_ref, bias_ref, s_ref,
                *, tq, ck, nsel, ih, gsz, kvh, hd, pos_bits):
    i = pl.program_id(1)
    nch = ((i + 1) * tq + ck - 1) // ck
    row = lax.broadcasted_iota(I32, (ck, tq), 0)
    qpos = i * tq + lax.broadcasted_iota(I32, (ck, tq), 1)

    def off(j):
        return pl.multiple_of(j * ck, ck)

    def score_chunk(j, carry):
        ikc = ik_ref[0, pl.ds(off(j), ck), :]
        acc = jnp.zeros((ck, tq), F32)
        for hh in range(ih):
            d = lax.dot_general(ikc, iq_ref[0, hh], NT, preferred_element_type=F32)
            acc = acc + iwt_ref[0, hh:hh + 1, :] * jnp.maximum(d, 0.0)
        valid = (j * ck + row) <= qpos
        keys_ref[pl.ds(off(j), ck), :] = jnp.where(valid, _sort_key(acc), INT_MIN)
        return carry

    lax.fori_loop(0, nch, score_chunk, 0)

    read_keys = lambda j: keys_ref[pl.ds(off(j), ck), :]
    read_pos = lambda j: j * ck + row
    thr, jcut = _topk_select(read_keys, read_pos, half_ref, lambda j: pl.ds(off(j), ck), nch, ck, tq, nsel,
                             pos_bits)

    def bias_chunk(j, carry):
        sel = _selected(read_keys(j), read_pos(j), thr, jcut)
        bias_ref[pl.ds(off(j), ck), :] = jnp.where(sel, 0.0, NEG)
        return carry

    lax.fori_loop(0, nch, bias_chunk, 0)

    qs = [dq_ref[0, g * gsz:(g + 1) * gsz].reshape(gsz * tq, hd) for g in range(kvh)]
    wq = gsz * tq

    def pass_a(j, ms):
        b = bias_ref[pl.ds(off(j), ck), :]
        bt = jnp.concatenate([b] * gsz, axis=1)
        out = []
        for g in range(kvh):
            s = lax.dot_general(dk_ref[0, g, pl.ds(off(j), ck), :], qs[g], NT, preferred_element_type=F32) + bt
            s_ref[g, pl.ds(off(j), ck), :] = s
            out.append(jnp.maximum(ms[g], jnp.max(s, axis=0, keepdims=True)))
        return tuple(out)

    ms = lax.fori_loop(0, nch, pass_a, tuple(jnp.full((1, wq), NEG, F32) for _ in range(kvh)))

    def pass_b(j, carry):
        out = []
        for g in range(kvh):
            l, acc = carry[g]
            p = jnp.exp2(s_ref[g, pl.ds(off(j), ck), :] - ms[g])
            l = l + jnp.sum(p, axis=0, keepdims=True)
            acc = acc + _dot(dvt_ref[0, j, g * hd:(g + 1) * hd, :], p.astype(BF16))
            out.append((l, acc))
        return tuple(out)

    res = lax.fori_loop(0, nch, pass_b,
                        tuple((jnp.zeros((1, wq), F32), jnp.zeros((hd, wq), F32)) for _ in range(kvh)))
    outs = []
    for g in range(kvh):
        ot = res[g][1] / res[g][0]
        for hh in range(gsz):
            outs.append(ot[:, hh * tq:(hh + 1) * tq])
    o_ref[0] = jnp.concatenate(outs, axis=0).T.astype(BF16)


def _dsa_prompt(iq, iwt, ik16, dq, dk, dvt, nsel, tq):
    b, ih, t, idim = iq.shape
    nhd, hd = dq.shape[1], dq.shape[3]
    kvh = dk.shape[1]
    gsz = nhd // kvh
    ck = dvt.shape[3]
    ik3 = ik16.reshape(b, t, idim)
    pos_bits = int(t).bit_length()
    kern = functools.partial(_dsa_kernel, tq=tq, ck=ck, nsel=nsel, ih=ih, gsz=gsz, kvh=kvh, hd=hd,
                             pos_bits=pos_bits)
    return pl.pallas_call(
        kern,
        grid=(b, t // tq),
        in_specs=[pl.BlockSpec((1, ih, tq, idim), lambda bi, i: (bi, 0, i, 0)),
                  pl.BlockSpec((1, 8, tq), lambda bi, i: (bi, 0, i)),
                  pl.BlockSpec((1, t, idim), lambda bi, i: (bi, 0, 0)),
                  pl.BlockSpec((1, nhd, tq, hd), lambda bi, i: (bi, 0, i, 0)),
                  pl.BlockSpec((1, kvh, t, hd), lambda bi, i: (bi, 0, 0, 0)),
                  pl.BlockSpec((1, t // ck, 2 * hd, ck), lambda bi, i: (bi, 0, 0, 0))],
        out_specs=pl.BlockSpec((1, tq, nhd * hd), lambda bi, i: (bi, i, 0)),
        out_shape=jax.ShapeDtypeStruct((b, t, nhd * hd), BF16),
        scratch_shapes=[pltpu.VMEM((t, tq), I32), pltpu.VMEM((t, tq), I16), pltpu.VMEM((t, tq), F32),
                        pltpu.VMEM((kvh, t, gsz * tq), F32)],
        compiler_params=_cp(("parallel", "arbitrary")),
        name="dsa_prompt",
    )(iq, iwt, ik3, dq, dk, dvt)


def _ffn_kernel(*refs, n_mix, final, splits):
    x_ref = refs[0]
    mix_refs = refs[1:1 + 2 * n_mix]
    (g1_ref, nfg_ref, sc_ref, sh_ref, g2_ref, wg_ref, wu_ref, wd_ref, fg_ref, o_ref) = refs[1 + 2 * n_mix:]
    mix = _dot(mix_refs[0][...], mix_refs[1][...])
    for k in range(1, n_mix):
        mix = mix + _dot(mix_refs[2 * k][...], mix_refs[2 * k + 1][...])
    x1 = x_ref[...] + g1_ref[0] * mix
    hb = (_rms(x1, nfg_ref[...]) * (1.0 + sc_ref[0]) + sh_ref[0]).astype(BF16)
    y = None
    for lo, hi in splits:
        act = _silu(_dot(hb, wg_ref[:, lo:hi])) * _dot(hb, wu_ref[:, lo:hi])
        part = _dot(act.astype(BF16), wd_ref[lo:hi, :])
        y = part if y is None else y + part
    y = x1 + g2_ref[0] * y
    if final:
        y = _rms(y, fg_ref[...])
    o_ref[...] = y


MXU_TILE = 256


def _mix_ffn(x2d, mixes, g1, nfg, sc2, sh2, g2, wg, wu, wd, fg, tq, tm, per_row, final):
    m, d = x2d.shape
    dff = wg.shape[1]
    nt = tq // tm
    half = (dff // MXU_TILE + 1) // 2 * MXU_TILE
    splits = ((0, half), (half, dff)) if 0 < half < dff else ((0, dff),)
    if per_row:
        mod_spec = pl.BlockSpec((1, tm, d), lambda i: (0, i, 0))
    else:
        mod_spec = pl.BlockSpec((1, 1, d), lambda i: (i // nt, 0, 0))
    row = lambda i: (i, 0)
    const2 = lambda shape: pl.BlockSpec(shape, lambda i: (0, 0), pipeline_mode=pl.Buffered(1))
    in_specs = [pl.BlockSpec((tm, d), row)]
    args = [x2d]
    for a, w in mixes:
        in_specs += [pl.BlockSpec((tm, a.shape[1]), row), const2(w.shape)]
        args += [a, w]
    in_specs += [mod_spec, const2((1, d)), mod_spec, mod_spec, mod_spec,
                 const2(wg.shape), const2(wu.shape), const2(wd.shape), const2((1, d))]
    args += [g1, nfg, sc2, sh2, g2, wg, wu, wd, fg]
    kern = functools.partial(_ffn_kernel, n_mix=len(mixes), final=final, splits=splits)
    return pl.pallas_call(
        kern,
        grid=(m // tm,),
        in_specs=in_specs,
        out_specs=pl.BlockSpec((tm, d), row),
        out_shape=jax.ShapeDtypeStruct((m, d), F32),
        compiler_params=_cp(("parallel",), 60),
        name="mix_ffn",
    )(*args)


def _normmod_mm_kernel(x_ref, sc_ref, sh_ref, g_ref, w_ref, o_ref):
    h = _rms(x_ref[...], g_ref[...]) * (1.0 + sc_ref[0]) + sh_ref[0]
    o_ref[...] = _dot(h.astype(BF16), w_ref[...])


def _normmod_mm(x2d, sc, sh, g, w, tq, tm, per_row):
    m, d = x2d.shape
    n = w.shape[1]
    nt = tq // tm
    if per_row:
        mod_spec = pl.BlockSpec((1, tm, d), lambda i: (0, i, 0))
    else:
        mod_spec = pl.BlockSpec((1, 1, d), lambda i: (i // nt, 0, 0))
    return pl.pallas_call(
        _normmod_mm_kernel,
        grid=(m // tm,),
        in_specs=[pl.BlockSpec((tm, d), lambda i: (i, 0)), mod_spec, mod_spec,
                  pl.BlockSpec((1, d), lambda i: (0, 0)),
                  pl.BlockSpec((d, n), lambda i: (0, 0), pipeline_mode=pl.Buffered(1))],
        out_specs=pl.BlockSpec((tm, n), lambda i: (i, 0)),
        out_shape=jax.ShapeDtypeStruct((m, n), F32),
        compiler_params=_cp(("parallel",)),
        name="odd_in_proj",
    )(x2d, sc, sh, g, w)


def _hgrn_consts(c):
    nlev = int(np.log2(c))
    t = np.arange(c)
    tril = (t[None, :] <= t[:, None]).astype(np.float32)
    mats = [tril]
    masks = [np.eye(c, dtype=np.float32)]
    for lev in range(1, nlev + 1):
        w = 2 ** (lev - 1)
        piv = (t // (2 * w)) * (2 * w) + w - 1
        mats.append(tril - tril[piv])
        same = (t[:, None] // (2 * w)) == (t[None, :] // (2 * w))
        right = ((t // w) % 2 == 1)[:, None]
        left = ((t // w) % 2 == 0)[None, :]
        masks.append((same & right & left).astype(np.float32))
    return np.concatenate(mats, axis=0), np.stack(masks), nlev


def _split2(x):
    a = x.astype(BF16)
    b = (x - a.astype(F32)).astype(BF16)
    return a, b


def _hgrn_kernel(zq_ref, zf_ref, zi_ref, zg_ref, lb_ref, og_ref, cm_ref, bm_ref, o_ref, st_ref, s_ref,
                 *, c, nlev, q_scale, hpb, kk, vd):
    ci = pl.program_id(2)

    @pl.when(ci == 0)
    def _():
        s_ref[...] = jnp.zeros(s_ref.shape, F32)

    for hh in range(hpb):
        ks = slice(hh * kk, (hh + 1) * kk)
        vs = slice(hh * vd, (hh + 1) * vd)
        lb = lb_ref[hh]
        zf = zf_ref[:, ks]
        e = jnp.exp(-jnp.abs(zf))
        inv = 1.0 / (1.0 + e)
        sig_pos = jnp.where(zf >= 0, inv, e * inv)
        sig_neg = jnp.where(zf >= 0, e * inv, inv)
        logf = jnp.log(lb + (1.0 - lb) * sig_pos)
        k = (1.0 - lb) * sig_neg
        q = _silu(zq_ref[:, ks]) * q_scale
        v = zi_ref[:, vs].astype(BF16)

        a2 = jnp.concatenate(_split2(logf), axis=1)
        gd = _dot(cm_ref[...], a2)
        gd = gd[:, 0:kk] + gd[:, kk:2 * kk]
        gcum = gd[0:c]
        glast = gcum[c - 1:c]

        amat = bm_ref[0] * lax.dot_general(q.astype(BF16), k.astype(BF16), NT, preferred_element_type=F32)
        for lev in range(1, nlev + 1):
            ed = jnp.exp(-jnp.abs(gd[lev * c:(lev + 1) * c]))
            amat = amat + bm_ref[lev] * lax.dot_general((q * ed).astype(BF16), (k * ed).astype(BF16), NT,
                                                        preferred_element_type=F32)
        s_prev = s_ref[hh]
        o = _dot(amat.astype(BF16), v) + _dot((q * jnp.exp(gcum)).astype(BF16), s_prev.astype(BF16))
        kdt = (k * jnp.exp(glast - gcum)).T.astype(BF16)
        decay_col = jnp.broadcast_to(jnp.exp(glast), (kk, kk)).T
        s_new = decay_col * s_prev + _dot(kdt, v)
        s_ref[hh] = s_new
        o_ref[:, vs] = (_rms(o, og_ref[...]) * _silu(zg_ref[:, vs])).astype(BF16)

    @pl.when(ci == pl.num_programs(2) - 1)
    def _():
        st_ref[...] = s_ref[...]


def _hgrn_prompt(z, lb, og, b, t, nh, c, q_scale):
    m = z.shape[0]
    kk = lb.shape[-1]
    vd = og.shape[-1]
    nc = t // c
    hpb = _pick(nh, (8, 4, 2, 1))
    ng = nh // hpb
    cm, bm, nlev = _hgrn_consts(c)
    cm = jnp.asarray(cm, BF16)
    bm = jnp.asarray(bm, F32)
    seg = lambda s, w: pl.BlockSpec((c, hpb * w), lambda bi, h, ci: (bi * nc + ci, s * ng + h))
    kern = functools.partial(_hgrn_kernel, c=c, nlev=nlev, q_scale=q_scale, hpb=hpb, kk=kk, vd=vd)
    return pl.pallas_call(
        kern,
        grid=(b, ng, nc),
        in_specs=[seg(0, kk), seg(1, kk), seg(2, vd), seg(3, vd),
                  pl.BlockSpec((hpb, 1, kk), lambda bi, h, ci: (h, 0, 0)),
                  pl.BlockSpec((1, vd), lambda bi, h, ci: (0, 0)),
                  pl.BlockSpec(cm.shape, lambda bi, h, ci: (0, 0)),
                  pl.BlockSpec(bm.shape, lambda bi, h, ci: (0, 0, 0))],
        out_specs=(pl.BlockSpec((c, hpb * vd), lambda bi, h, ci: (bi * nc + ci, h)),
                   pl.BlockSpec((None, hpb, kk, vd), lambda bi, h, ci: (bi, h, 0, 0))),
        out_shape=(jax.ShapeDtypeStruct((m, nh * vd), BF16),
                   jax.ShapeDtypeStruct((b, nh, kk, vd), F32)),
        scratch_shapes=[pltpu.VMEM((hpb, kk, vd), F32)],
        compiler_params=_cp(("parallel", "parallel", "arbitrary")),
        name="hgrn_prompt",
    )(z, z, z, z, lb, og, cm, bm)


def _hgrn_step_kernel(z_ref, lb_ref, og_ref, s_ref, o_ref, st_ref, *, nh, kk, vd, bt, q_scale):
    z = z_ref[...]
    hf = nh * kk
    for h in range(nh):
        lb = lb_ref[h]
        zf = z[:, hf + h * kk:hf + (h + 1) * kk]
        e = jnp.exp(-jnp.abs(zf))
        inv = 1.0 / (1.0 + e)
        f = lb + (1.0 - lb) * jnp.where(zf >= 0, inv, e * inv)
        k = (1.0 - lb) * jnp.where(zf >= 0, e * inv, inv)
        q = _silu(z[:, h * kk:(h + 1) * kk]) * q_scale
        v = z[:, 2 * hf + h * vd:2 * hf + (h + 1) * vd]
        gate = _silu(z[:, 2 * hf + nh * vd + h * vd:2 * hf + nh * vd + (h + 1) * vd])
        ft, kt, qt = f.T, k.T, q.T
        for j in range(bt):
            s_new = ft[:, j:j + 1] * s_ref[j, h] + kt[:, j:j + 1] * v[j:j + 1, :]
            st_ref[j, h] = s_new
            o = jnp.sum(qt[:, j:j + 1] * s_new, axis=0, keepdims=True)
            o_ref[j:j + 1, h * vd:(h + 1) * vd] = (_rms(o, og_ref[...]) * gate[j:j + 1, :]).astype(BF16)


def _hgrn_step(z, lb, og, state, nh, q_scale):
    bsz = z.shape[0]
    kk, vd = state.shape[2], state.shape[3]
    bt = 8
    kern = functools.partial(_hgrn_step_kernel, nh=nh, kk=kk, vd=vd, bt=bt, q_scale=q_scale)
    return pl.pallas_call(
        kern,
        grid=(bsz // bt,),
        in_specs=[pl.BlockSpec((bt, z.shape[1]), lambda i: (i, 0)),
                  pl.BlockSpec(lb.shape, lambda i: (0, 0, 0)),
                  pl.BlockSpec((1, vd), lambda i: (0, 0)),
                  pl.BlockSpec((bt, nh, kk, vd), lambda i: (i, 0, 0, 0))],
        out_specs=(pl.BlockSpec((bt, nh * vd), lambda i: (i, 0)),
                   pl.BlockSpec((bt, nh, kk, vd), lambda i: (i, 0, 0, 0))),
        out_shape=(jax.ShapeDtypeStruct((bsz, nh * vd), BF16),
                   jax.ShapeDtypeStruct(state.shape, F32)),
        compiler_params=_cp(("parallel",)),
        name="hgrn_step",
    )(z, lb, og, state)


PAGE_RING_SLOTS = 4


def _fetch_pages(pt_ref, cache_hbm, buf, sem, pp):
    nslot = buf.shape[0]
    ahead = nslot - 1
    nsteps = pl.num_programs(0) * pl.num_programs(1)
    s = pl.program_id(0) * pl.num_programs(1) + pl.program_id(1)

    def copy(step, k):
        slot = step % nslot
        return pltpu.make_async_copy(cache_hbm.at[pt_ref[step * pp + k]], buf.at[slot, k], sem.at[slot])

    def start(step):
        for k in range(pp):
            copy(step, k).start()

    @pl.when(s == 0)
    def _():
        for d in range(ahead):
            start(d)

    @pl.when(s + ahead < nsteps)
    def _():
        start(s + ahead)

    for k in range(pp):
        copy(s, k).wait()
    slot = s % nslot
    return [buf[slot, k] for k in range(pp)]


def _stage_pages(pages):
    return jnp.concatenate([pg.astype(BF16) for pg in pages], axis=1)


def _paged_softmax_pv(s_ref, pg_ref, s_new, new_v, v_rows):
    ng, nh, w = s_ref.shape
    mp = s_ref[0]
    for gi in range(1, ng):
        mp = jnp.maximum(mp, s_ref[gi])
    m = jnp.maximum(jnp.max(_fold_lanes(mp, jnp.maximum), axis=1, keepdims=True),
                    jnp.max(s_new, axis=1, keepdims=True))
    mbt = jnp.concatenate([jnp.broadcast_to(m, (nh, LANES))] * (w // LANES), axis=1)
    lp = jnp.zeros((nh, LANES), F32)
    acc = None
    for gi in range(ng):
        p = jnp.exp2(s_ref[gi] - mbt)
        lp = lp + _fold_lanes(p, jnp.add)
        pv = lax.dot_general(p.astype(BF16), pg_ref[gi, v_rows, :], NT, preferred_element_type=F32)
        acc = pv if acc is None else acc + pv
    p_new = jnp.exp2(s_new - m)
    l = jnp.sum(lp, axis=1, keepdims=True) + jnp.sum(p_new, axis=1, keepdims=True)
    return (acc + _dot(p_new.astype(BF16), new_v)) / l


def _mla_dec2_kernel(pt_ref, q_ref, new_ref, wuv_ref, cache_ref, o_ref, pg_ref, s_ref, buf_ref, sem_ref,
                     *, pp, r, vd, nh):
    g = pl.program_id(1)
    q = q_ref[...]
    slab = _stage_pages(_fetch_pages(pt_ref, cache_ref, buf_ref, sem_ref, pp))
    pg_ref[g] = slab
    s_ref[g] = _dot(q, slab)

    @pl.when(g == pl.num_programs(1) - 1)
    def _():
        new = new_ref[...]
        s_new = lax.dot_general(q, new, NT, preferred_element_type=F32)
        s_new = jnp.where(lax.broadcasted_iota(I32, s_new.shape, 1) == 0, s_new, NEG)
        lat = _paged_softmax_pv(s_ref, pg_ref, s_new, new[:, 0:r], slice(0, r)).astype(BF16)
        for hd in range(nh):
            o_ref[0:1, hd * vd:(hd + 1) * vd] = _dot(lat[hd:hd + 1], wuv_ref[hd])
        o_ref[1:8, :] = jnp.zeros((7, nh * vd), F32)


def _mla_decode2(page_table, qdec, newrow, wuv, cache, pp):
    bsz, nh, dq = qdec.shape
    n_pages = page_table.shape[1]
    psz = cache.shape[2]
    r, vd = wuv.shape[1], wuv.shape[2]
    ng = n_pages // pp
    assert bsz * ng >= PAGE_RING_SLOTS
    kern = functools.partial(_mla_dec2_kernel, pp=pp, r=r, vd=vd, nh=nh)
    grid_spec = pltpu.PrefetchScalarGridSpec(
        num_scalar_prefetch=1,
        grid=(bsz, ng),
        in_specs=[pl.BlockSpec((None, nh, dq), lambda b, g, ptr: (b, 0, 0)),
                  pl.BlockSpec((None, 8, dq), lambda b, g, ptr: (b, 0, 0)),
                  pl.BlockSpec(wuv.shape, lambda b, g, ptr: (0, 0, 0)),
                  pl.BlockSpec(memory_space=pl.ANY)],
        out_specs=pl.BlockSpec((None, 8, nh * vd), lambda b, g, ptr: (b, 0, 0)),
        scratch_shapes=[pltpu.VMEM((ng, dq, pp * psz), BF16), pltpu.VMEM((ng, nh, pp * psz), F32),
                        pltpu.VMEM((PAGE_RING_SLOTS, pp, dq, psz), F32),
                        pltpu.SemaphoreType.DMA((PAGE_RING_SLOTS,))],
    )
    out = pl.pallas_call(
        kern,
        grid_spec=grid_spec,
        out_shape=jax.ShapeDtypeStruct((bsz, 8, nh * vd), F32),
        compiler_params=_cp(("arbitrary", "arbitrary")),
        name="mla_decode",
    )(page_table.reshape(-1), qdec, newrow, wuv, cache)
    return out[:, 0, :]


def _idx_score_kernel(pt_ref, iq_ref, iw_ref, cache_ref, keys_ref, buf_ref, sem_ref, *, pp):
    iq = iq_ref[...]
    iw = iw_ref[...]
    d = _dot(iq, _stage_pages(_fetch_pages(pt_ref, cache_ref, buf_ref, sem_ref, pp)))
    keys_ref[...] = _sort_key(jnp.sum(iw * jnp.maximum(d, 0.0), axis=0, keepdims=True))


def _idx_select_kernel(keys_ref, iq_ref, iw_ref, iknew_ref, bias_ref, bnew_ref, *, eb, nsel, pos_bits):
    ng, w = keys_ref.shape[1], keys_ref.shape[2]
    pos = lax.broadcasted_iota(I32, (ng, w), 0) * w + lax.broadcasted_iota(I32, (ng, w), 1)
    knew = []
    for e in range(eb):
        d = lax.dot_general(iq_ref[e], iknew_ref[e], NT, preferred_element_type=F32)
        snew = _sort_key(jnp.sum(iw_ref[e] * jnp.maximum(d, 0.0), axis=0, keepdims=True))
        knew.append(snew[0:1, 0:1])

    def count(pred):
        return jnp.sum(jnp.sum(pred.astype(I32), axis=0, keepdims=True), axis=1, keepdims=True)

    def bit_body(b, thrs):
        out = []
        for e in range(eb):
            cand = thrs[e] + lax.shift_left(jnp.int32(1), 31 - b)
            cnt = count(keys_ref[e] >= cand) + (knew[e] >= cand).astype(I32)
            out.append(jnp.where(cnt >= nsel, cand, thrs[e]))
        return tuple(out)

    thrs = lax.fori_loop(0, 32, bit_body, tuple(jnp.full((1, 1), INT_MIN, I32) for _ in range(eb)))
    needs = tuple(nsel - count(keys_ref[e] > thrs[e]) - (knew[e] > thrs[e]).astype(I32) for e in range(eb))

    def jbit_body(b, jcs):
        out = []
        for e in range(eb):
            cand = jcs[e] + lax.shift_left(jnp.int32(1), pos_bits - 1 - b)
            cnt = count((keys_ref[e] == thrs[e]) & (pos < cand))
            out.append(jnp.where(cnt <= needs[e], cand, jcs[e]))
        return tuple(out)

    jcs = lax.fori_loop(0, pos_bits, jbit_body, tuple(jnp.zeros((1, 1), I32) for _ in range(eb)))
    for e in range(eb):
        bias_ref[e] = jnp.where(_selected(keys_ref[e], pos, thrs[e], jcs[e]), 0.0, NEG)
        ties_taken = count((keys_ref[e] == thrs[e]) & (pos < jcs[e]))
        new_sel = (knew[e] > thrs[e]) | ((knew[e] == thrs[e]) & (ties_taken < needs[e]))
        slot0 = (lax.broadcasted_iota(I32, (8, LANES), 0) == 0) & (lax.broadcasted_iota(I32, (8, LANES), 1) == 0)
        bnew_ref[e] = jnp.where(slot0 & jnp.broadcast_to(new_sel, (8, LANES)), 0.0, NEG)


def _idx_decode2(page_table, iqd, iwd, iknew, cache, nsel, pp):
    bsz, _, idim = iqd.shape
    n_pages = page_table.shape[1]
    psz = cache.shape[2]
    ng, w = n_pages // pp, pp * psz
    pos_bits = int(n_pages * psz).bit_length()
    assert bsz * ng >= PAGE_RING_SLOTS
    grid_spec = pltpu.PrefetchScalarGridSpec(
        num_scalar_prefetch=1,
        grid=(bsz, ng),
        in_specs=[pl.BlockSpec((None, 8, idim), lambda b, g, ptr: (b, 0, 0)),
                  pl.BlockSpec((None, 8, 1), lambda b, g, ptr: (b, 0, 0)),
                  pl.BlockSpec(memory_space=pl.ANY)],
        out_specs=pl.BlockSpec((None, None, 1, w), lambda b, g, ptr: (b, g, 0, 0)),
        scratch_shapes=[pltpu.VMEM((PAGE_RING_SLOTS, pp, idim, psz), F32),
                        pltpu.SemaphoreType.DMA((PAGE_RING_SLOTS,))],
    )
    keys = pl.pallas_call(
        functools.partial(_idx_score_kernel, pp=pp),
        grid_spec=grid_spec,
        out_shape=jax.ShapeDtypeStruct((bsz, ng, 1, w), I32),
        compiler_params=_cp(("arbitrary", "arbitrary")),
        name="idx_score",
    )(page_table.reshape(-1), iqd, iwd, cache)

    eb = _pick(bsz, (8, 4, 2, 1))
    blk = lambda *shape: pl.BlockSpec((eb,) + shape, lambda i: (i,) + tuple(0 for _ in shape))
    return pl.pallas_call(
        functools.partial(_idx_select_kernel, eb=eb, nsel=nsel, pos_bits=pos_bits),
        grid=(bsz // eb,),
        in_specs=[blk(ng, w), blk(8, idim), blk(8, 1), blk(8, idim)],
        out_specs=(blk(ng, w), blk(8, LANES)),
        out_shape=(jax.ShapeDtypeStruct((bsz, ng, w), F32), jax.ShapeDtypeStruct((bsz, 8, LANES), F32)),
        compiler_params=_cp(("parallel",)),
        name="idx_select",
    )(keys.reshape(bsz, ng, w), iqd, iwd, iknew)


def _dsa_dec2_kernel(pt_ref, q_ref, new_ref, bias_ref, bnew_ref, cache_ref, o_ref, pg_ref, s_ref, buf_ref,
                     sem_ref, *, pp, hd, gsz):
    g = pl.program_id(1)
    q = q_ref[...]
    kw = 2 * hd
    slab = _stage_pages(_fetch_pages(pt_ref, cache_ref, buf_ref, sem_ref, pp))
    pg_ref[g] = slab
    s_ref[g] = _dot(q, slab[0:kw, :]) + bias_ref[g]

    @pl.when(g == pl.num_programs(1) - 1)
    def _():
        new = new_ref[...]
        s_new = lax.dot_general(q, new[:, 0:kw], NT, preferred_element_type=F32)
        s_new = s_new + bnew_ref[0:1, 0:8]
        o = _paged_softmax_pv(s_ref, pg_ref, s_new, new[:, kw:2 * kw], slice(kw, 2 * kw))
        rowi = lax.broadcasted_iota(I32, o.shape, 0)
        o_ref[...] = jnp.where(rowi < gsz, o, pltpu.roll(o, hd, 1))


def _dsa_decode2(page_table, qbd, newkv, bias, bias_new, cache, pp, hd, gsz):
    bsz = qbd.shape[0]
    n_pages = page_table.shape[1]
    psz = cache.shape[2]
    ng, w = n_pages // pp, pp * psz
    assert bsz * ng >= PAGE_RING_SLOTS
    kern = functools.partial(_dsa_dec2_kernel, pp=pp, hd=hd, gsz=gsz)
    grid_spec = pltpu.PrefetchScalarGridSpec(
        num_scalar_prefetch=1,
        grid=(bsz, ng),
        in_specs=[pl.BlockSpec((None, 8, 2 * hd), lambda b, g, ptr: (b, 0, 0)),
                  pl.BlockSpec((None, 8, 4 * hd), lambda b, g, ptr: (b, 0, 0)),
                  pl.BlockSpec((None, ng, 1, w), lambda b, g, ptr: (b, 0, 0, 0)),
                  pl.BlockSpec((None, 8, LANES), lambda b, g, ptr: (b, 0, 0)),
                  pl.BlockSpec(memory_space=pl.ANY)],
        out_specs=pl.BlockSpec((None, 8, 2 * hd), lambda b, g, ptr: (b, 0, 0)),
        scratch_shapes=[pltpu.VMEM((ng, 4 * hd, w), BF16), pltpu.VMEM((ng, 8, w), F32),
                        pltpu.VMEM((PAGE_RING_SLOTS, pp, 4 * hd, psz), F32),
                        pltpu.SemaphoreType.DMA((PAGE_RING_SLOTS,))],
    )
    return pl.pallas_call(
        kern,
        grid_spec=grid_spec,
        out_shape=jax.ShapeDtypeStruct((bsz, 8, 2 * hd), F32),
        compiler_params=_cp(("arbitrary", "arbitrary")),
        name="dsa_decode",
    )(page_table.reshape(-1), qbd, newkv, bias.reshape(bsz, ng, 1, w), bias_new, cache)


def _rope_tables(pos, half):
    inv_freq = ROPE_THETA ** (-jnp.arange(half, dtype=F32) / half)
    ang = pos.astype(F32)[:, None] * inv_freq[None, :]
    cos, sin = jnp.cos(ang), jnp.sin(ang)
    reps = LANES // (2 * half)
    return (jnp.tile(jnp.concatenate([cos, cos], axis=1), (1, reps)),
            jnp.tile(jnp.concatenate([-sin, sin], axis=1), (1, reps)))


def _pick(n, pref):
    for c in pref:
        if n % c == 0:
            return c
    return n


def kernel(x_prompt, x_sample, cache_mla, cache_dsa_kv, cache_idx, state_hgrn, page_table, c_prompt, c_sample,
           w_ada, b_ada, norm_mix_g, norm_ffn_g, w_in_even, mla_q_norm_g, mla_kv_norm_g, mla_w_q_nope,
           mla_w_q_rope, mla_w_uk, mla_w_uv, w_out_even, w_in_odd, hgrn_lower_bounds, hgrn_out_norm_g,
           w_out_odd, w_ffn_gate, w_ffn_up, w_ffn_down, final_norm_g):
    depth, d, _ = w_ada.shape
    bp, tp, _ = x_prompt.shape
    bs, ts, _ = x_sample.shape
    assert ts == 1
    q_lora, mla_heads, mla_nope = mla_w_q_nope.shape[1:]
    mla_rope = mla_w_q_rope.shape[3]
    kv_lora, _, mla_v = mla_w_uv.shape[1:]
    dsa_kv_heads, dsa_hd = cache_dsa_kv.shape[4:]
    idx_dim = cache_idx.shape[3]
    n_pages = page_table.shape[1]
    psz = cache_mla.shape[2]
    past_len = n_pages * psz
    sizes_tail = w_in_even.shape[2] - (q_lora + kv_lora + mla_rope + 2 * dsa_kv_heads * dsa_hd + idx_dim)
    dsa_heads = (w_out_even.shape[1] - mla_heads * mla_v) // dsa_hd
    idx_heads = (sizes_tail - dsa_heads * dsa_hd) // (idx_dim + 1)
    dm = EvenDims(q_lora, kv_lora, mla_rope, mla_heads, dsa_heads, dsa_kv_heads, dsa_hd, idx_heads, idx_dim)
    gsz = dsa_heads // dsa_kv_heads
    hg_f = hgrn_lower_bounds.shape[1]
    hg_heads = state_hgrn.shape[2]
    hg_k = state_hgrn.shape[3]
    hg_v = state_hgrn.shape[4]
    log2e = float(np.log2(np.e))
    scales = ((mla_nope + mla_rope) ** -0.5 * log2e, dsa_hd ** -0.5 * log2e, idx_dim ** -0.5, idx_heads ** -0.5)
    hg_q_scale = hg_k ** -0.5

    lb_all = jax.nn.softmax(hgrn_lower_bounds.astype(F32), axis=0)
    lb_all = jnp.cumsum(lb_all, axis=0) - lb_all[0]

    c_all = jnp.concatenate([c_prompt, c_sample], axis=0)
    mods = _adaln(c_all, w_ada, b_ada)

    def mods_for(l, sample):
        mm = mods[l, bp:] if sample else mods[l, :bp]
        parts = [mm[:, k * d:(k + 1) * d] for k in range(6)]
        return [p[None] if sample else p[:, None, :] for p in parts]

    bf = lambda a: a.astype(BF16)
    row1 = lambda a: a.reshape(1, -1).astype(F32)

    def even_weights(la):
        w = w_in_even[la]
        splits = np.cumsum([q_lora, kv_lora, mla_rope, dsa_heads * dsa_hd, dsa_kv_heads * dsa_hd,
                            dsa_kv_heads * dsa_hd, idx_heads * idx_dim, idx_dim])
        wq, wckv, wkr, wdq, wdk, wdv, wiq, wik, wiw = jnp.split(w, splits, axis=1)
        padc = lambda a, n: jnp.pad(a, ((0, 0), (0, n - a.shape[1])))
        w_cat = jnp.concatenate([wq, wckv, padc(wkr, LANES), wdq, wdk, wdv, wiq,
                                 padc(jnp.concatenate([wik, wiw], axis=1), LANES)], axis=1)
        assert w_cat.shape[1] == dm.n
        wabs = _absorb(mla_w_q_nope[la], mla_w_uk[la])
        wqr = mla_w_q_rope[la].reshape(q_lora, mla_heads * mla_rope)
        wuv = jnp.transpose(mla_w_uv[la], (1, 0, 2))
        return bf(w_cat), bf(wabs), bf(wqr), bf(wuv)

    def run(x, sample):
        b, t, _ = x.shape
        m = b * t
        x2d = x.reshape(m, d)
        if sample:
            bq, tq, per_row = 1, m, True
            tm = _pick(m, (128,))
            pos = jnp.full((m,), past_len, I32)
        else:
            bq, tq, per_row = b, t, False
            tm = _pick(t, (512, 256, 128))
            pos = jnp.arange(t)
        tabs = (*_rope_tables(pos, dsa_hd // 2), *_rope_tables(pos, mla_rope // 2))
        rows_mla, rows_dsa, rows_idx, states = [], [], [], []
        for l in range(depth):
            sh1, sc1, g1, sh2, sc2, g2 = mods_for(l, sample)
            nmg, nfg = row1(norm_mix_g[l]), row1(norm_ffn_g[l])
            final = l == depth - 1
            if l % 2 == 0:
                la = l // 2
                w_cat, wabs, wqr, wuv = even_weights(la)
                (qcat, mrow, kv16, dsarow, dq, dk, dvt, iq, ik, ik16, iwt) = _even_project(
                    x2d, sc1, sh1, nmg, w_cat, row1(mla_q_norm_g[la]), row1(mla_kv_norm_g[la]), wqr, wabs,
                    tabs, dm, bq, tq, tm, per_row, scales)
                if sample:
                    nsel = min(TOPK_MAX, (past_len + t) // 4)
                    pp = _pick(n_pages, (16, 8))
                    pad8 = lambda a: jnp.pad(a[:, None, :], ((0, 0), (0, 7), (0, 0)))
                    qdec = jnp.transpose(qcat[0], (1, 0, 2))
                    page_t = lambda c: jnp.transpose(c, (0, 2, 1))
                    mla_o = _mla_decode2(page_table, qdec, pad8(kv16), wuv, page_t(cache_mla[la]), pp)
                    iqd = jnp.pad(jnp.transpose(iq[0], (1, 0, 2)), ((0, 0), (0, 8 - idx_heads), (0, 0)))
                    iwd = jnp.transpose(iwt[0])[:, :, None]
                    iwd = jnp.where(jnp.arange(8)[None, :, None] < idx_heads, iwd, 0.0)
                    bias, bias_new = _idx_decode2(page_table, iqd, iwd, pad8(ik16), page_t(cache_idx[la]), nsel, pp)
                    dqd = jnp.transpose(dq[0], (1, 0, 2))
                    zeros = jnp.zeros_like(dqd[:, :gsz])
                    qbd = jnp.concatenate(
                        [jnp.concatenate([dqd[:, :gsz], zeros], axis=2),
                         jnp.concatenate([zeros, dqd[:, gsz:]], axis=2)], axis=1)
                    cache_kv = jnp.transpose(cache_dsa_kv[la], (0, 2, 3, 4, 1)).reshape(-1, 4 * dsa_hd, psz)
                    dsa_o8 = _dsa_decode2(page_table, qbd, pad8(bf(dsarow)), bias, bias_new, cache_kv, pp, dsa_hd,
                                          gsz)
                    dsa_o = dsa_o8[:, :, :dsa_hd].reshape(b, dsa_heads * dsa_hd)
                    mla_o, dsa_o = bf(mla_o), bf(dsa_o)
                else:
                    nsel = min(TOPK_MAX, t // 4)
                    mla_o = _mla_prompt(qcat, kv16.reshape(b, t, -1), wuv, _pick(t, (256, 128))).reshape(m, -1)
                    dsa_o = _dsa_prompt(iq, iwt, ik16, dq, dk, dvt, nsel, _pick(t, (256, 128))).reshape(m, -1)
                n_mla = mla_heads * mla_v
                mixes = [(mla_o, bf(w_out_even[la][:n_mla])), (dsa_o, bf(w_out_even[la][n_mla:]))]
                rows_mla.append(mrow.reshape(b, t, -1))
                if sample:
                    rows_dsa.append(dsarow.reshape(b, t, 2, dsa_kv_heads, dsa_hd))
                else:
                    rows_dsa.append(jnp.transpose(dsarow.reshape(b, 2, dsa_kv_heads, dsa_hd, t), (0, 4, 1, 2, 3)))
                rows_idx.append(ik.reshape(b, t, -1))
            else:
                lr = l // 2
                z = _normmod_mm(x2d, sc1, sh1, nmg, bf(w_in_odd[lr]), tq, tm, per_row)
                lb = lb_all[l].reshape(hg_heads, 1, hg_k)
                og = row1(hgrn_out_norm_g[lr])
                if sample:
                    o_g, st = _hgrn_step(z, lb, og, state_hgrn[lr].astype(F32), hg_heads, hg_q_scale)
                else:
                    o_g, st = _hgrn_prompt(z, lb, og, b, t, hg_heads, _pick(t, (256, 128)), hg_q_scale)
                mixes = [(o_g, bf(w_out_odd[lr]))]
                states.append(st)
            x2d = _mix_ffn(x2d, mixes, g1, nfg, sc2, sh2, g2, bf(w_ffn_gate[l]), bf(w_ffn_up[l]),
                           bf(w_ffn_down[l]), row1(final_norm_g), tq, tm, per_row, final)
        return (x2d.reshape(b, t, d), jnp.stack(rows_mla), jnp.stack(rows_dsa), jnp.stack(rows_idx),
                jnp.stack(states))

    y_p, mla_p, dsa_p, idx_p, hg_p = run(x_prompt, False)
    y_s, mla_s, dsa_s, idx_s, hg_s = run(x_sample, True)
    return (y_p, y_s, mla_p, mla_s, dsa_p, dsa_s, idx_p, idx_s, hg_p, hg_s)
```

```python
import functools

import jax
import jax.numpy as jnp
import numpy as np
from jax import lax
from jax.experimental import pallas as pl
from jax.experimental.pallas import tpu as pltpu

F32 = jnp.float32
BF16 = jnp.bfloat16
I32 = jnp.int32
I16 = jnp.int16

EPS = 1e-6
ROPE_THETA = 10000.0
TOPK_MAX = 256
NEG = -1e30
INT_MIN = -(2 ** 31)
LANES = 128
DSA_KEY_CHUNK = 512
HGRN_PIVOT_BCAST_MIN = 8
MIB = 1024 * 1024

NT = (((1,), (1,)), ((), ()))
TN = (((0,), (0,)), ((), ()))


def _cp(sem, vmem_mib=48):
    return pltpu.CompilerParams(dimension_semantics=sem, vmem_limit_bytes=vmem_mib * MIB)


def _sigmoid(x):
    return 1.0 / (1.0 + jnp.exp(-x))


def _silu(x):
    return x * _sigmoid(x)


def _rms(x, g):
    return x * lax.rsqrt(jnp.mean(x * x, axis=-1, keepdims=True) + EPS) * g


def _dot(a, b):
    return jnp.dot(a, b, preferred_element_type=F32)


def _rope_cols(x, cos, sin_signed, half):
    outs = []
    lane = lax.broadcasted_iota(I32, (x.shape[0], LANES), 1)
    first = (lane % (2 * half)) < half
    for c in range(x.shape[1] // LANES):
        xc = x[:, c * LANES:(c + 1) * LANES]
        rot = jnp.where(first, pltpu.roll(xc, LANES - half, 1), pltpu.roll(xc, half, 1))
        outs.append(xc * cos + rot * sin_signed)
    return outs[0] if len(outs) == 1 else jnp.concatenate(outs, axis=1)


def _ada_kernel(c_ref, w_ref, b_ref, o_ref):
    a = _silu(c_ref[...]).astype(BF16)
    o_ref[...] = _dot(a, w_ref[...].astype(BF16)) + b_ref[...]


def _adaln(c_all, w_ada, b_ada):
    depth, d, n = w_ada.shape
    bc = c_all.shape[0]
    tn = n // 4
    return pl.pallas_call(
        _ada_kernel,
        grid=(depth, n // tn),
        in_specs=[pl.BlockSpec((bc, d), lambda l, j: (0, 0)),
                  pl.BlockSpec((None, d, tn), lambda l, j: (l, 0, j)),
                  pl.BlockSpec((None, 1, tn), lambda l, j: (l, 0, j))],
        out_specs=pl.BlockSpec((None, bc, tn), lambda l, j: (l, 0, j)),
        out_shape=jax.ShapeDtypeStruct((depth, bc, n), F32),
        compiler_params=_cp(("parallel", "parallel")),
        name="adaln",
    )(c_all, w_ada, b_ada.reshape(depth, 1, n))


def _absorb_kernel(a_ref, b_ref, o_ref):
    o_ref[...] = lax.dot_general(a_ref[...].astype(BF16), b_ref[...].astype(BF16), NT,
                                 preferred_element_type=F32)


def _absorb(w_q_nope, w_uk):
    a = jnp.transpose(w_q_nope, (1, 0, 2))
    b = jnp.transpose(w_uk, (1, 0, 2))
    h, rq, dn = a.shape
    rkv = b.shape[1]
    out = pl.pallas_call(
        _absorb_kernel,
        grid=(h,),
        in_specs=[pl.BlockSpec((None, rq, dn), lambda i: (i, 0, 0)),
                  pl.BlockSpec((None, rkv, dn), lambda i: (i, 0, 0))],
        out_specs=pl.BlockSpec((None, rq, rkv), lambda i: (i, 0, 0)),
        out_shape=jax.ShapeDtypeStruct((h, rq, rkv), F32),
        compiler_params=_cp(("parallel",)),
        name="mla_absorb",
    )(a, b)
    return jnp.transpose(out, (1, 0, 2)).reshape(rq, h * rkv)


class EvenDims:
    def __init__(self, q_lora, kv_lora, mla_rope, mla_heads, dsa_heads, dsa_kv_heads, dsa_hd, idx_heads, idx_dim):
        assert q_lora % LANES == 0 and kv_lora == LANES and mla_rope <= LANES
        assert dsa_hd == 64 and idx_dim == 64 and dsa_kv_heads == 2 and idx_heads <= 8
        self.q_lora, self.kv_lora, self.mla_rope, self.mla_heads = q_lora, kv_lora, mla_rope, mla_heads
        self.dsa_heads, self.dsa_kv_heads, self.dsa_hd = dsa_heads, dsa_kv_heads, dsa_hd
        self.idx_heads, self.idx_dim = idx_heads, idx_dim
        self.o_zq = 0
        self.o_ckv = q_lora
        self.o_kr = self.o_ckv + kv_lora
        self.o_dq = self.o_kr + LANES
        self.o_dk = self.o_dq + dsa_heads * dsa_hd
        self.o_dv = self.o_dk + dsa_kv_heads * dsa_hd
        self.o_iq = self.o_dv + dsa_kv_heads * dsa_hd
        self.o_ik = self.o_iq + idx_heads * idx_dim
        self.n = self.o_ik + LANES
        self.mla_row = kv_lora + mla_rope


def _even_kernel(x_ref, sc_ref, sh_ref, g_ref, w_ref, qg_ref, kvg_ref, wqr_ref, wabs_ref,
                 c64_ref, s64_ref, c32_ref, s32_ref,
                 qcat_ref, row_ref, kv16_ref, dsarow_ref, dq_ref, dk_ref, dvt_ref, iq_ref, ik_ref, ik16_ref,
                 iwt_ref, *, dm, mla_scale, dsa_scale, idx_scale, idx_w_scale):
    h = _rms(x_ref[...], g_ref[...]) * (1.0 + sc_ref[0]) + sh_ref[0]
    z = _dot(h.astype(BF16), w_ref[...])
    c64, s64, c32, s32 = c64_ref[...], s64_ref[...], c32_ref[...], s32_ref[...]
    tm = z.shape[0]

    cq = _rms(z[:, dm.o_zq:dm.o_zq + dm.q_lora], qg_ref[...]).astype(BF16)
    qlat = _dot(cq, wabs_ref[...]) * mla_scale
    qrope = _rope_cols(_dot(cq, wqr_ref[...]), c32, s32, dm.mla_rope // 2) * mla_scale
    for hd in range(dm.mla_heads):
        qcat_ref[0, hd, :, 0:dm.kv_lora] = qlat[:, hd * dm.kv_lora:(hd + 1) * dm.kv_lora].astype(BF16)
        qcat_ref[0, hd, :, dm.kv_lora:dm.mla_row] = qrope[:, hd * dm.mla_rope:(hd + 1) * dm.mla_rope].astype(BF16)

    ckv = _rms(z[:, dm.o_ckv:dm.o_ckv + dm.kv_lora], kvg_ref[...])
    kr = _rope_cols(z[:, dm.o_kr:dm.o_kr + LANES], c32, s32, dm.mla_rope // 2)[:, 0:dm.mla_rope]
    row_ref[:, 0:dm.kv_lora] = ckv
    row_ref[:, dm.kv_lora:dm.mla_row] = kr
    kv16_ref[:, 0:dm.kv_lora] = ckv.astype(BF16)
    kv16_ref[:, dm.kv_lora:dm.mla_row] = kr.astype(BF16)

    hd_ = dm.dsa_hd
    dq = _rope_cols(z[:, dm.o_dq:dm.o_dk], c64, s64, hd_ // 2) * dsa_scale
    for hh in range(dm.dsa_heads):
        dq_ref[0, hh] = dq[:, hh * hd_:(hh + 1) * hd_].astype(BF16)
    dk = _rope_cols(z[:, dm.o_dk:dm.o_dv], c64, s64, hd_ // 2)
    dv = z[:, dm.o_dv:dm.o_iq]
    dvt = dv.T
    if len(dsarow_ref.shape) == 3:
        dsarow_ref[0, 0:LANES, :] = dk.T
        dsarow_ref[0, LANES:2 * LANES, :] = dvt
    else:
        dsarow_ref[:, 0:LANES] = dk
        dsarow_ref[:, LANES:2 * LANES] = dv
    for gg in range(dm.dsa_kv_heads):
        dk_ref[0, gg] = dk[:, gg * hd_:(gg + 1) * hd_].astype(BF16)
    vck = dvt_ref.shape[3]
    for c in range(tm // vck):
        dvt_ref[0, c] = dvt[:, c * vck:(c + 1) * vck].astype(BF16)

    iq = _rope_cols(z[:, dm.o_iq:dm.o_ik], c64, s64, dm.idx_dim // 2) * idx_scale
    for hh in range(dm.idx_heads):
        iq_ref[0, hh] = iq[:, hh * dm.idx_dim:(hh + 1) * dm.idx_dim].astype(BF16)
    last = z[:, dm.o_ik:dm.o_ik + LANES]
    ik = _rope_cols(last, c64, s64, dm.idx_dim // 2)[:, 0:dm.idx_dim]
    ik_ref[...] = ik
    ik16_ref[...] = ik.astype(BF16)
    iwt_ref[0] = last.T[dm.idx_dim:dm.idx_dim + 8, :] * idx_w_scale


def _even_project(x2d, sc, sh, g, w_in, qg, kvg, wqr, wabs, tabs, dm, bq, tq, tm, per_row, scales):
    rows_t = not per_row
    m, d = x2d.shape
    nt = tq // tm
    vck = min(DSA_KEY_CHUNK, tm)
    nh, kvh, ih = dm.mla_heads, dm.dsa_kv_heads, dm.idx_heads
    row2 = lambda i: (i, 0)
    hm4 = lambda i: (i // nt, 0, i % nt, 0)
    if per_row:
        mod_spec = pl.BlockSpec((1, tm, d), lambda i: (0, i, 0))
    else:
        mod_spec = pl.BlockSpec((1, 1, d), lambda i: (i // nt, 0, 0))
    const = lambda shape: pl.BlockSpec(shape, lambda i: tuple(0 for _ in shape))
    tab_spec = pl.BlockSpec((tm, LANES), lambda i: (i % nt, 0))
    out_shapes = (
        jax.ShapeDtypeStruct((bq, nh, tq, dm.mla_row), BF16),
        jax.ShapeDtypeStruct((m, dm.mla_row), F32),
        jax.ShapeDtypeStruct((m, dm.mla_row), BF16),
        (jax.ShapeDtypeStruct((bq, 2 * LANES, tq), F32) if rows_t
         else jax.ShapeDtypeStruct((m, 2 * LANES), F32)),
        jax.ShapeDtypeStruct((bq, dm.dsa_heads, tq, dm.dsa_hd), BF16),
        jax.ShapeDtypeStruct((bq, kvh, tq, dm.dsa_hd), BF16),
        jax.ShapeDtypeStruct((bq, tq // vck, LANES, vck), BF16),
        jax.ShapeDtypeStruct((bq, ih, tq, dm.idx_dim), BF16),
        jax.ShapeDtypeStruct((m, dm.idx_dim), F32),
        jax.ShapeDtypeStruct((m, dm.idx_dim), BF16),
        jax.ShapeDtypeStruct((bq, 8, tq), F32),
    )
    out_specs = (
        pl.BlockSpec((1, nh, tm, dm.mla_row), hm4),
        pl.BlockSpec((tm, dm.mla_row), row2),
        pl.BlockSpec((tm, dm.mla_row), row2),
        (pl.BlockSpec((1, 2 * LANES, tm), lambda i: (i // nt, 0, i % nt)) if rows_t
         else pl.BlockSpec((tm, 2 * LANES), row2)),
        pl.BlockSpec((1, dm.dsa_heads, tm, dm.dsa_hd), hm4),
        pl.BlockSpec((1, kvh, tm, dm.dsa_hd), hm4),
        pl.BlockSpec((1, tm // vck, LANES, vck), lambda i: (i // nt, i % nt, 0, 0)),
        pl.BlockSpec((1, ih, tm, dm.idx_dim), hm4),
        pl.BlockSpec((tm, dm.idx_dim), row2),
        pl.BlockSpec((tm, dm.idx_dim), row2),
        pl.BlockSpec((1, 8, tm), lambda i: (i // nt, 0, i % nt)),
    )
    kern = functools.partial(_even_kernel, dm=dm, mla_scale=scales[0], dsa_scale=scales[1],
                             idx_scale=scales[2], idx_w_scale=scales[3])
    return pl.pallas_call(
        kern,
        grid=(m // tm,),
        in_specs=[pl.BlockSpec((tm, d), row2), mod_spec, mod_spec, const((1, d)), const(w_in.shape),
                  const(qg.shape), const(kvg.shape), const(wqr.shape), const(wabs.shape),
                  tab_spec, tab_spec, tab_spec, tab_spec],
        out_specs=out_specs,
        out_shape=out_shapes,
        compiler_params=_cp(("parallel",), 56),
        name="even_project",
    )(x2d, sc, sh, g, w_in, qg, kvg, wqr, wabs, *tabs)


def _fold_lanes(x, op):
    out = x[:, 0:LANES]
    for c in range(1, x.shape[1] // LANES):
        out = op(out, x[:, c * LANES:(c + 1) * LANES])
    return out


def _mla_kernel(q_ref, kv_ref, wuv_ref, o_ref, s_ref, mp_ref, lp_ref, acc_ref, *, tq, tk, nh, r, vd):
    qi = pl.program_id(1)
    n = nh * tq
    q = q_ref[0].reshape(n, q_ref.shape[-1])
    nch = (qi * tq + tq + tk - 1) // tk
    reps = tk // LANES

    def keys(j):
        return kv_ref[0, pl.ds(pl.multiple_of(j * tk, tk), tk), :]

    mp_ref[...] = jnp.full(mp_ref.shape, NEG, F32)

    def pass_a(j, c):
        s = lax.dot_general(q, keys(j), NT, preferred_element_type=F32)
        s_ref[j] = s
        mp_ref[...] = jnp.maximum(mp_ref[...], _fold_lanes(s, jnp.maximum))
        return c

    lax.fori_loop(0, nch - 1, pass_a, 0)
    jl = nch - 1
    s = lax.dot_general(q, keys(jl), NT, preferred_element_type=F32)
    qpos = qi * tq + lax.broadcasted_iota(I32, s.shape, 0) % tq
    kpos = jl * tk + lax.broadcasted_iota(I32, s.shape, 1)
    s = jnp.where(kpos <= qpos, s, NEG)
    s_ref[jl] = s
    mp = jnp.maximum(mp_ref[...], _fold_lanes(s, jnp.maximum))
    mb = jnp.broadcast_to(jnp.max(mp, axis=1, keepdims=True), (n, LANES))
    mbt = jnp.concatenate([mb] * reps, axis=1)

    lp_ref[...] = jnp.zeros(lp_ref.shape, F32)
    acc_ref[...] = jnp.zeros(acc_ref.shape, F32)

    def pass_b(j, c):
        p = jnp.exp2(s_ref[j] - mbt)
        lp_ref[...] += _fold_lanes(p, jnp.add)
        acc_ref[...] += _dot(p.astype(BF16), keys(j)[:, 0:r])
        return c

    lax.fori_loop(0, nch, pass_b, 0)
    lat = (acc_ref[...] / jnp.sum(lp_ref[...], axis=1, keepdims=True)).astype(BF16)
    for hd in range(nh):
        o_ref[0, :, hd * vd:(hd + 1) * vd] = _dot(lat[hd * tq:(hd + 1) * tq], wuv_ref[hd]).astype(BF16)


def _mla_prompt(qcat, kv16, wuv, tq):
    b, nh, t, dq = qcat.shape
    r, vd = wuv.shape[1], wuv.shape[2]
    tk = _pick(t, (512, 256, 128))
    n = nh * tq
    kern = functools.partial(_mla_kernel, tq=tq, tk=tk, nh=nh, r=r, vd=vd)
    return pl.pallas_call(
        kern,
        grid=(b, t // tq),
        in_specs=[pl.BlockSpec((1, nh, tq, dq), lambda bi, qi: (bi, 0, qi, 0)),
                  pl.BlockSpec((1, t, dq), lambda bi, qi: (bi, 0, 0)),
                  pl.BlockSpec(wuv.shape, lambda bi, qi: (0, 0, 0))],
        out_specs=pl.BlockSpec((1, tq, nh * vd), lambda bi, qi: (bi, qi, 0)),
        out_shape=jax.ShapeDtypeStruct((b, t, nh * vd), BF16),
        scratch_shapes=[pltpu.VMEM((t // tk, n, tk), F32), pltpu.VMEM((n, LANES), F32),
                        pltpu.VMEM((n, LANES), F32), pltpu.VMEM((n, r), F32)],
        compiler_params=_cp(("parallel", "arbitrary")),
        name="mla_prompt",
    )(qcat, kv16, wuv)


def _sort_key(score):
    b = pltpu.bitcast(score, I32)
    return jnp.where(b < 0, (b ^ jnp.int32(0x7FFFFFFF)) + 1, b)


def _topk_select(read_keys, read_pos, half_ref, chunk, nch, rows, width, nsel, pos_bits):
    low = -(2 ** 15)

    def count(pred):
        def body(j, c):
            return c + jnp.sum(pred(j).astype(I32).reshape(rows // 8, 8, width), axis=0)
        c = lax.fori_loop(0, nch, body, jnp.zeros((8, width), I32))
        return jnp.sum(c, axis=0, keepdims=True)

    def count16(pred):
        def body(j, c):
            ind = pred(half_ref[chunk(j), :]).astype(I16)
            parts = [ind[a:a + 16, :] for a in range(0, rows, 16)]
            while len(parts) > 1:
                parts = [parts[a] + parts[a + 1] for a in range(0, len(parts) - 1, 2)] + parts[len(parts) & ~1:]
            return c + parts[0]
        c = lax.fori_loop(0, nch, body, jnp.zeros((16, width), I16))
        return jnp.sum(c.astype(I32), axis=0, keepdims=True)

    def stage(need):
        def bit_body(b, t):
            cand = t + lax.shift_left(jnp.int32(1), 15 - b)
            c16 = cand.astype(I16)
            return jnp.where(count16(lambda h: h >= c16) >= need, cand, t)
        return lax.fori_loop(0, 16, bit_body, jnp.full((1, width), low, I32))

    def put_hi(j, c):
        half_ref[chunk(j), :] = (read_keys(j) >> 16).astype(I16)
        return c

    lax.fori_loop(0, nch, put_hi, 0)
    t_hi = stage(nsel)
    t_hi16 = t_hi.astype(I16)
    need_lo = nsel - count16(lambda h: h > t_hi16)

    def put_lo(j, c):
        k = read_keys(j)
        in_bucket = half_ref[chunk(j), :] == t_hi16
        half_ref[chunk(j), :] = jnp.where(in_bucket, ((k & 0xFFFF) + low).astype(I16), jnp.int16(low))
        return c

    lax.fori_loop(0, nch, put_lo, 0)
    t_lo = stage(need_lo)
    thr = lax.shift_left(t_hi, 16) + (t_lo - low)
    c_thr = count(lambda j: read_keys(j) >= thr)

    def tie_search():
        need = nsel - count(lambda j: read_keys(j) > thr)

        def jbit_body(b, jc):
            cand = jc + lax.shift_left(jnp.int32(1), pos_bits - 1 - b)
            cnt = count(lambda j: (read_keys(j) == thr) & (read_pos(j) < cand))
            return jnp.where(cnt <= need, cand, jc)

        return lax.fori_loop(0, pos_bits, jbit_body, jnp.zeros((1, width), I32))

    excess = (c_thr > nsel) & (thr > INT_MIN)
    any_excess = jnp.max(excess.astype(I32)) > 0
    jcut = lax.cond(any_excess, tie_search, lambda: jnp.full((1, width), 2 ** pos_bits, I32))
    return thr, jcut


def _selected(keys, pos, thr, jcut):
    return ((keys > thr) | ((keys == thr) & (pos < jcut))) & (keys != INT_MIN)


def _dsa_kernel(iq_ref, iwt_ref, ik_ref, dq_ref, dk_ref, dvt_ref, o_ref, keys_ref, half_ref, bias_ref, s_ref,
                *, tq, ck, nsel, ih, gsz, kvh, hd, pos_bits):
    i = pl.program_id(1)
    nch = ((i + 1) * tq + ck - 1) // ck
    row = lax.broadcasted_iota(I32, (ck, tq), 0)
    qpos = i * tq + lax.broadcasted_iota(I32, (ck, tq), 1)

    def off(j):
        return pl.multiple_of(j * ck, ck)

    def score_chunk(j, carry):
        ikc = ik_ref[0, pl.ds(off(j), ck), :]
        acc = jnp.zeros((ck, tq), F32)
        for hh in range(ih):
            d = lax.dot_general(ikc, iq_ref[0, hh], NT, preferred_element_type=F32)
            acc = acc + iwt_ref[0, hh:hh + 1, :] * jnp.maximum(d, 0.0)
        valid = (j * ck + row) <= qpos
        keys_ref[pl.ds(off(j), ck), :] = jnp.where(valid, _sort_key(acc), INT_MIN)
        return carry

    lax.fori_loop(0, nch, score_chunk, 0)

    read_keys = lambda j: keys_ref[pl.ds(off(j), ck), :]
    read_pos = lambda j: j * ck + row
    thr, jcut = _topk_select(read_keys, read_pos, half_ref, lambda j: pl.ds(off(j), ck), nch, ck, tq, nsel,
                             pos_bits)

    def bias_chunk(j, carry):
        sel = _selected(read_keys(j), read_pos(j), thr, jcut)
        bias_ref[pl.ds(off(j), ck), :] = jnp.where(sel, 0.0, NEG)
        return carry

    lax.fori_loop(0, nch, bias_chunk, 0)

    qs = [dq_ref[0, g * gsz:(g + 1) * gsz].reshape(gsz * tq, hd) for g in range(kvh)]
    wq = gsz * tq

    def pass_a(j, ms):
        b = bias_ref[pl.ds(off(j), ck), :]
        bt = jnp.concatenate([b] * gsz, axis=1)
        out = []
        for g in range(kvh):
            s = lax.dot_general(dk_ref[0, g, pl.ds(off(j), ck), :], qs[g], NT, preferred_element_type=F32) + bt
            s_ref[g, pl.ds(off(j), ck), :] = s
            out.append(jnp.maximum(ms[g], jnp.max(s, axis=0, keepdims=True)))
        return tuple(out)

    ms = lax.fori_loop(0, nch, pass_a, tuple(jnp.full((1, wq), NEG, F32) for _ in range(kvh)))

    def pass_b(j, carry):
        out = []
        for g in range(kvh):
            l, acc = carry[g]
            p = jnp.exp2(s_ref[g, pl.ds(off(j), ck), :] - ms[g])
            l = l + jnp.sum(p, axis=0, keepdims=True)
            acc = acc + _dot(dvt_ref[0, j, g * hd:(g + 1) * hd, :], p.astype(BF16))
            out.append((l, acc))
        return tuple(out)

    res = lax.fori_loop(0, nch, pass_b,
                        tuple((jnp.zeros((1, wq), F32), jnp.zeros((hd, wq), F32)) for _ in range(kvh)))
    outs = []
    for g in range(kvh):
        ot = res[g][1] / res[g][0]
        for hh in range(gsz):
            outs.append(ot[:, hh * tq:(hh + 1) * tq])
    o_ref[0] = jnp.concatenate(outs, axis=0).T.astype(BF16)


def _dsa_prompt(iq, iwt, ik16, dq, dk, dvt, nsel, tq):
    b, ih, t, idim = iq.shape
    nhd, hd = dq.shape[1], dq.shape[3]
    kvh = dk.shape[1]
    gsz = nhd // kvh
    ck = dvt.shape[3]
    ik3 = ik16.reshape(b, t, idim)
    pos_bits = int(t).bit_length()
    kern = functools.partial(_dsa_kernel, tq=tq, ck=ck, nsel=nsel, ih=ih, gsz=gsz, kvh=kvh, hd=hd,
                             pos_bits=pos_bits)
    return pl.pallas_call(
        kern,
        grid=(b, t // tq),
        in_specs=[pl.BlockSpec((1, ih, tq, idim), lambda bi, i: (bi, 0, i, 0)),
                  pl.BlockSpec((1, 8, tq), lambda bi, i: (bi, 0, i)),
                  pl.BlockSpec((1, t, idim), lambda bi, i: (bi, 0, 0)),
                  pl.BlockSpec((1, nhd, tq, hd), lambda bi, i: (bi, 0, i, 0)),
                  pl.BlockSpec((1, kvh, t, hd), lambda bi, i: (bi, 0, 0, 0)),
                  pl.BlockSpec((1, t // ck, 2 * hd, ck), lambda bi, i: (bi, 0, 0, 0))],
        out_specs=pl.BlockSpec((1, tq, nhd * hd), lambda bi, i: (bi, i, 0)),
        out_shape=jax.ShapeDtypeStruct((b, t, nhd * hd), BF16),
        scratch_shapes=[pltpu.VMEM((t, tq), I32), pltpu.VMEM((t, tq), I16), pltpu.VMEM((t, tq), F32),
                        pltpu.VMEM((kvh, t, gsz * tq), F32)],
        compiler_params=_cp(("parallel", "arbitrary")),
        name="dsa_prompt",
    )(iq, iwt, ik3, dq, dk, dvt)


def _ffn_kernel(*refs, n_mix, final, splits):
    x_ref = refs[0]
    mix_refs = refs[1:1 + 2 * n_mix]
    (g1_ref, nfg_ref, sc_ref, sh_ref, g2_ref, wg_ref, wu_ref, wd_ref, fg_ref, o_ref) = refs[1 + 2 * n_mix:]
    mix = _dot(mix_refs[0][...], mix_refs[1][...])
    for k in range(1, n_mix):
        mix = mix + _dot(mix_refs[2 * k][...], mix_refs[2 * k + 1][...])
    x1 = x_ref[...] + g1_ref[0] * mix
    hb = (_rms(x1, nfg_ref[...]) * (1.0 + sc_ref[0]) + sh_ref[0]).astype(BF16)
    y = None
    for lo, hi in splits:
        act = _silu(_dot(hb, wg_ref[:, lo:hi])) * _dot(hb, wu_ref[:, lo:hi])
        part = _dot(act.astype(BF16), wd_ref[lo:hi, :])
        y = part if y is None else y + part
    y = x1 + g2_ref[0] * y
    if final:
        y = _rms(y, fg_ref[...])
    o_ref[...] = y


MXU_TILE = 256


def _mix_ffn(x2d, mixes, g1, nfg, sc2, sh2, g2, wg, wu, wd, fg, tq, tm, per_row, final):
    m, d = x2d.shape
    dff = wg.shape[1]
    nt = tq // tm
    half = (dff // MXU_TILE + 1) // 2 * MXU_TILE
    splits = ((0, half), (half, dff)) if 0 < half < dff else ((0, dff),)
    if per_row:
        mod_spec = pl.BlockSpec((1, tm, d), lambda i: (0, i, 0))
    else:
        mod_spec = pl.BlockSpec((1, 1, d), lambda i: (i // nt, 0, 0))
    row = lambda i: (i, 0)
    const2 = lambda shape: pl.BlockSpec(shape, lambda i: (0, 0), pipeline_mode=pl.Buffered(1))
    in_specs = [pl.BlockSpec((tm, d), row)]
    args = [x2d]
    for a, w in mixes:
        in_specs += [pl.BlockSpec((tm, a.shape[1]), row), const2(w.shape)]
        args += [a, w]
    in_specs += [mod_spec, const2((1, d)), mod_spec, mod_spec, mod_spec,
                 const2(wg.shape), const2(wu.shape), const2(wd.shape), const2((1, d))]
    args += [g1, nfg, sc2, sh2, g2, wg, wu, wd, fg]
    kern = functools.partial(_ffn_kernel, n_mix=len(mixes), final=final, splits=splits)
    return pl.pallas_call(
        kern,
        grid=(m // tm,),
        in_specs=in_specs,
        out_specs=pl.BlockSpec((tm, d), row),
        out_shape=jax.ShapeDtypeStruct((m, d), F32),
        compiler_params=_cp(("parallel",), 60),
        name="mix_ffn",
    )(*args)


def _normmod_mm_kernel(x_ref, sc_ref, sh_ref, g_ref, w_ref, o_ref):
    h = _rms(x_ref[...], g_ref[...]) * (1.0 + sc_ref[0]) + sh_ref[0]
    o_ref[...] = _dot(h.astype(BF16), w_ref[...])


def _normmod_mm(x2d, sc, sh, g, w, tq, tm, per_row):
    m, d = x2d.shape
    n = w.shape[1]
    nt = tq // tm
    if per_row:
        mod_spec = pl.BlockSpec((1, tm, d), lambda i: (0, i, 0))
    else:
        mod_spec = pl.BlockSpec((1, 1, d), lambda i: (i // nt, 0, 0))
    return pl.pallas_call(
        _normmod_mm_kernel,
        grid=(m // tm,),
        in_specs=[pl.BlockSpec((tm, d), lambda i: (i, 0)), mod_spec, mod_spec,
                  pl.BlockSpec((1, d), lambda i: (0, 0)),
                  pl.BlockSpec((d, n), lambda i: (0, 0), pipeline_mode=pl.Buffered(1))],
        out_specs=pl.BlockSpec((tm, n), lambda i: (i, 0)),
        out_shape=jax.ShapeDtypeStruct((m, n), F32),
        compiler_params=_cp(("parallel",)),
        name="odd_in_proj",
    )(x2d, sc, sh, g, w)


def _hgrn_consts(c):
    nlev = int(np.log2(c))
    t = np.arange(c)
    tril = (t[None, :] <= t[:, None]).astype(np.float32)
    mats = [tril]
    masks = [np.eye(c, dtype=np.float32)]
    for lev in range(1, nlev + 1):
        w = 2 ** (lev - 1)
        piv = (t // (2 * w)) * (2 * w) + w - 1
        if w < HGRN_PIVOT_BCAST_MIN:
            mats.append(tril - tril[piv])
        same = (t[:, None] // (2 * w)) == (t[None, :] // (2 * w))
        right = ((t // w) % 2 == 1)[:, None]
        left = ((t // w) % 2 == 0)[None, :]
        masks.append((same & right & left).astype(np.float32))
    return np.concatenate(mats, axis=0), np.stack(masks), nlev


def _split2(x):
    a = x.astype(BF16)
    b = (x - a.astype(F32)).astype(BF16)
    return a, b


def _hgrn_kernel(zq_ref, zf_ref, zi_ref, zg_ref, lb_ref, og_ref, cm_ref, bm_ref, o_ref, st_ref, s_ref,
                 *, c, nlev, q_scale, hpb, kk, vd):
    ci = pl.program_id(2)

    @pl.when(ci == 0)
    def _():
        s_ref[...] = jnp.zeros(s_ref.shape, F32)

    for hh in range(hpb):
        ks = slice(hh * kk, (hh + 1) * kk)
        vs = slice(hh * vd, (hh + 1) * vd)
        lb = lb_ref[hh]
        zf = zf_ref[:, ks]
        e = jnp.exp(-jnp.abs(zf))
        inv = 1.0 / (1.0 + e)
        sig_pos = jnp.where(zf >= 0, inv, e * inv)
        sig_neg = jnp.where(zf >= 0, e * inv, inv)
        logf = jnp.log(lb + (1.0 - lb) * sig_pos)
        k = (1.0 - lb) * sig_neg
        q = _silu(zq_ref[:, ks]) * q_scale
        v = zi_ref[:, vs].astype(BF16)

        a2 = jnp.concatenate(_split2(logf), axis=1)
        gd = _dot(cm_ref[...], a2)
        gd = gd[:, 0:kk] + gd[:, kk:2 * kk]
        gcum = gd[0:c]
        glast = gcum[c - 1:c]

        amat = bm_ref[0] * lax.dot_general(q.astype(BF16), k.astype(BF16), NT, preferred_element_type=F32)
        for lev in range(1, nlev + 1):
            w = 2 ** (lev - 1)
            if w < HGRN_PIVOT_BCAST_MIN:
                dlev = gd[lev * c:(lev + 1) * c]
            else:
                piv = [jnp.broadcast_to(gcum[a + w - 1:a + w, :], (2 * w, kk)) for a in range(0, c, 2 * w)]
                dlev = gcum - (piv[0] if len(piv) == 1 else jnp.concatenate(piv, axis=0))
            ed = jnp.exp(-jnp.abs(dlev))
            amat = amat + bm_ref[lev] * lax.dot_general((q * ed).astype(BF16), (k * ed).astype(BF16), NT,
                                                        preferred_element_type=F32)
        s_prev = s_ref[hh]
        o = _dot(amat.astype(BF16), v) + _dot((q * jnp.exp(gcum)).astype(BF16), s_prev.astype(BF16))
        kdt = (k * jnp.exp(glast - gcum)).T.astype(BF16)
        decay_col = jnp.broadcast_to(jnp.exp(glast), (kk, kk)).T
        s_new = decay_col * s_prev + _dot(kdt, v)
        s_ref[hh] = s_new
        o_ref[:, vs] = (_rms(o, og_ref[...]) * _silu(zg_ref[:, vs])).astype(BF16)

    @pl.when(ci == pl.num_programs(2) - 1)
    def _():
        st_ref[...] = s_ref[...]


def _hgrn_prompt(z, lb, og, b, t, nh, c, q_scale):
    m = z.shape[0]
    kk = lb.shape[-1]
    vd = og.shape[-1]
    nc = t // c
    hpb = _pick(nh, (8, 4, 2, 1))
    ng = nh // hpb
    cm, bm, nlev = _hgrn_consts(c)
    cm = jnp.asarray(cm, BF16)
    bm = jnp.asarray(bm, F32)
    seg = lambda s, w: pl.BlockSpec((c, hpb * w), lambda bi, h, ci: (bi * nc + ci, s * ng + h))
    kern = functools.partial(_hgrn_kernel, c=c, nlev=nlev, q_scale=q_scale, hpb=hpb, kk=kk, vd=vd)
    return pl.pallas_call(
        kern,
        grid=(b, ng, nc),
        in_specs=[seg(0, kk), seg(1, kk), seg(2, vd), seg(3, vd),
                  pl.BlockSpec((hpb, 1, kk), lambda bi, h, ci: (h, 0, 0)),
                  pl.BlockSpec((1, vd), lambda bi, h, ci: (0, 0)),
                  pl.BlockSpec(cm.shape, lambda bi, h, ci: (0, 0)),
                  pl.BlockSpec(bm.shape, lambda bi, h, ci: (0, 0, 0))],
        out_specs=(pl.BlockSpec((c, hpb * vd), lambda bi, h, ci: (bi * nc + ci, h)),
                   pl.BlockSpec((None, hpb, kk, vd), lambda bi, h, ci: (bi, h, 0, 0))),
        out_shape=(jax.ShapeDtypeStruct((m, nh * vd), BF16),
                   jax.ShapeDtypeStruct((b, nh, kk, vd), F32)),
        scratch_shapes=[pltpu.VMEM((hpb, kk, vd), F32)],
        compiler_params=_cp(("parallel", "parallel", "arbitrary")),
        name="hgrn_prompt",
    )(z, z, z, z, lb, og, cm, bm)


def _hgrn_step_kernel(z_ref, lb_ref, og_ref, s_ref, o_ref, st_ref, *, nh, kk, vd, bt, q_scale):
    z = z_ref[...]
    hf = nh * kk
    for h in range(nh):
        lb = lb_ref[h]
        zf = z[:, hf + h * kk:hf + (h + 1) * kk]
        e = jnp.exp(-jnp.abs(zf))
        inv = 1.0 / (1.0 + e)
        f = lb + (1.0 - lb) * jnp.where(zf >= 0, inv, e * inv)
        k = (1.0 - lb) * jnp.where(zf >= 0, e * inv, inv)
        q = _silu(z[:, h * kk:(h + 1) * kk]) * q_scale
        v = z[:, 2 * hf + h * vd:2 * hf + (h + 1) * vd]
        gate = _silu(z[:, 2 * hf + nh * vd + h * vd:2 * hf + nh * vd + (h + 1) * vd])
        ft, kt, qt = f.T, k.T, q.T
        for j in range(bt):
            s_new = ft[:, j:j + 1] * s_ref[j, h] + kt[:, j:j + 1] * v[j:j + 1, :]
            st_ref[j, h] = s_new
            o = jnp.sum(qt[:, j:j + 1] * s_new, axis=0, keepdims=True)
            o_ref[j:j + 1, h * vd:(h + 1) * vd] = (_rms(o, og_ref[...]) * gate[j:j + 1, :]).astype(BF16)


def _hgrn_step(z, lb, og, state, nh, q_scale):
    bsz = z.shape[0]
    kk, vd = state.shape[2], state.shape[3]
    bt = 8
    kern = functools.partial(_hgrn_step_kernel, nh=nh, kk=kk, vd=vd, bt=bt, q_scale=q_scale)
    return pl.pallas_call(
        kern,
        grid=(bsz // bt,),
        in_specs=[pl.BlockSpec((bt, z.shape[1]), lambda i: (i, 0)),
                  pl.BlockSpec(lb.shape, lambda i: (0, 0, 0)),
                  pl.BlockSpec((1, vd), lambda i: (0, 0)),
                  pl.BlockSpec((bt, nh, kk, vd), lambda i: (i, 0, 0, 0))],
        out_specs=(pl.BlockSpec((bt, nh * vd), lambda i: (i, 0)),
                   pl.BlockSpec((bt, nh, kk, vd), lambda i: (i, 0, 0, 0))),
        out_shape=(jax.ShapeDtypeStruct((bsz, nh * vd), BF16),
                   jax.ShapeDtypeStruct(state.shape, F32)),
        compiler_params=_cp(("parallel",)),
        name="hgrn_step",
    )(z, lb, og, state)


PAGE_RING_SLOTS = 4


def _fetch_pages(pt_ref, cache_hbm, buf, sem, pp):
    nslot = buf.shape[0]
    ahead = nslot - 1
    nsteps = pl.num_programs(0) * pl.num_programs(1)
    s = pl.program_id(0) * pl.num_programs(1) + pl.program_id(1)

    def copy(step, k):
        slot = step % nslot
        return pltpu.make_async_copy(cache_hbm.at[pt_ref[step * pp + k]], buf.at[slot, k], sem.at[slot])

    def start(step):
        for k in range(pp):
            copy(step, k).start()

    @pl.when(s == 0)
    def _():
        for d in range(ahead):
            start(d)

    @pl.when(s + ahead < nsteps)
    def _():
        start(s + ahead)

    for k in range(pp):
        copy(s, k).wait()
    slot = s % nslot
    return [buf[slot, k] for k in range(pp)]


def _stage_pages(pages):
    return jnp.concatenate([pg.astype(BF16) for pg in pages], axis=1)


def _paged_softmax_pv(s_ref, pg_ref, s_new, new_v, v_rows):
    ng, nh, w = s_ref.shape
    mp = s_ref[0]
    for gi in range(1, ng):
        mp = jnp.maximum(mp, s_ref[gi])
    m = jnp.maximum(jnp.max(_fold_lanes(mp, jnp.maximum), axis=1, keepdims=True),
                    jnp.max(s_new, axis=1, keepdims=True))
    mbt = jnp.concatenate([jnp.broadcast_to(m, (nh, LANES))] * (w // LANES), axis=1)
    lp = jnp.zeros((nh, LANES), F32)
    acc = None
    for gi in range(ng):
        p = jnp.exp2(s_ref[gi] - mbt)
        lp = lp + _fold_lanes(p, jnp.add)
        pv = lax.dot_general(p.astype(BF16), pg_ref[gi, v_rows, :], NT, preferred_element_type=F32)
        acc = pv if acc is None else acc + pv
    p_new = jnp.exp2(s_new - m)
    l = jnp.sum(lp, axis=1, keepdims=True) + jnp.sum(p_new, axis=1, keepdims=True)
    return (acc + _dot(p_new.astype(BF16), new_v)) / l


def _mla_dec2_kernel(pt_ref, q_ref, new_ref, wuv_ref, cache_ref, o_ref, pg_ref, s_ref, buf_ref, sem_ref,
                     *, pp, r, vd, nh):
    g = pl.program_id(1)
    q = q_ref[...]
    slab = _stage_pages(_fetch_pages(pt_ref, cache_ref, buf_ref, sem_ref, pp))
    pg_ref[g] = slab
    s_ref[g] = _dot(q, slab)

    @pl.when(g == pl.num_programs(1) - 1)
    def _():
        new = new_ref[...]
        s_new = lax.dot_general(q, new, NT, preferred_element_type=F32)
        s_new = jnp.where(lax.broadcasted_iota(I32, s_new.shape, 1) == 0, s_new, NEG)
        lat = _paged_softmax_pv(s_ref, pg_ref, s_new, new[:, 0:r], slice(0, r)).astype(BF16)
        for hd in range(nh):
            o_ref[0:1, hd * vd:(hd + 1) * vd] = _dot(lat[hd:hd + 1], wuv_ref[hd])
        o_ref[1:8, :] = jnp.zeros((7, nh * vd), F32)


def _mla_decode2(page_table, qdec, newrow, wuv, cache, pp):
    bsz, nh, dq = qdec.shape
    n_pages = page_table.shape[1]
    psz = cache.shape[2]
    r, vd = wuv.shape[1], wuv.shape[2]
    ng = n_pages // pp
    assert bsz * ng >= PAGE_RING_SLOTS
    kern = functools.partial(_mla_dec2_kernel, pp=pp, r=r, vd=vd, nh=nh)
    grid_spec = pltpu.PrefetchScalarGridSpec(
        num_scalar_prefetch=1,
        grid=(bsz, ng),
        in_specs=[pl.BlockSpec((None, nh, dq), lambda b, g, ptr: (b, 0, 0)),
                  pl.BlockSpec((None, 8, dq), lambda b, g, ptr: (b, 0, 0)),
                  pl.BlockSpec(wuv.shape, lambda b, g, ptr: (0, 0, 0)),
                  pl.BlockSpec(memory_space=pl.ANY)],
        out_specs=pl.BlockSpec((None, 8, nh * vd), lambda b, g, ptr: (b, 0, 0)),
        scratch_shapes=[pltpu.VMEM((ng, dq, pp * psz), BF16), pltpu.VMEM((ng, nh, pp * psz), F32),
                        pltpu.VMEM((PAGE_RING_SLOTS, pp, dq, psz), F32),
                        pltpu.SemaphoreType.DMA((PAGE_RING_SLOTS,))],
    )
    out = pl.pallas_call(
        kern,
        grid_spec=grid_spec,
        out_shape=jax.ShapeDtypeStruct((bsz, 8, nh * vd), F32),
        compiler_params=_cp(("arbitrary", "arbitrary")),
        name="mla_decode",
    )(page_table.reshape(-1), qdec, newrow, wuv, cache)
    return out[:, 0, :]


def _idx_score_kernel(pt_ref, iq_ref, iw_ref, cache_ref, keys_ref, buf_ref, sem_ref, *, pp):
    iq = iq_ref[...]
    iw = iw_ref[...]
    d = _dot(iq, _stage_pages(_fetch_pages(pt_ref, cache_ref, buf_ref, sem_ref, pp)))
    keys_ref[...] = _sort_key(jnp.sum(iw * jnp.maximum(d, 0.0), axis=0, keepdims=True))


def _idx_select_kernel(keys_ref, iq_ref, iw_ref, iknew_ref, bias_ref, bnew_ref, *, eb, nsel, pos_bits):
    ng, w = keys_ref.shape[1], keys_ref.shape[2]
    pos = lax.broadcasted_iota(I32, (ng, w), 0) * w + lax.broadcasted_iota(I32, (ng, w), 1)
    knew = []
    for e in range(eb):
        d = lax.dot_general(iq_ref[e], iknew_ref[e], NT, preferred_element_type=F32)
        snew = _sort_key(jnp.sum(iw_ref[e] * jnp.maximum(d, 0.0), axis=0, keepdims=True))
        knew.append(snew[0:1, 0:1])

    def count(pred):
        return jnp.sum(jnp.sum(pred.astype(I32), axis=0, keepdims=True), axis=1, keepdims=True)

    def bit_body(b, thrs):
        out = []
        for e in range(eb):
            cand = thrs[e] + lax.shift_left(jnp.int32(1), 31 - b)
            cnt = count(keys_ref[e] >= cand) + (knew[e] >= cand).astype(I32)
            out.append(jnp.where(cnt >= nsel, cand, thrs[e]))
        return tuple(out)

    thrs = lax.fori_loop(0, 32, bit_body, tuple(jnp.full((1, 1), INT_MIN, I32) for _ in range(eb)))
    needs = tuple(nsel - count(keys_ref[e] > thrs[e]) - (knew[e] > thrs[e]).astype(I32) for e in range(eb))

    def jbit_body(b, jcs):
        out = []
        for e in range(eb):
            cand = jcs[e] + lax.shift_left(jnp.int32(1), pos_bits - 1 - b)
            cnt = count((keys_ref[e] == thrs[e]) & (pos < cand))
            out.append(jnp.where(cnt <= needs[e], cand, jcs[e]))
        return tuple(out)

    jcs = lax.fori_loop(0, pos_bits, jbit_body, tuple(jnp.zeros((1, 1), I32) for _ in range(eb)))
    for e in range(eb):
        bias_ref[e] = jnp.where(_selected(keys_ref[e], pos, thrs[e], jcs[e]), 0.0, NEG)
        ties_taken = count((keys_ref[e] == thrs[e]) & (pos < jcs[e]))
        new_sel = (knew[e] > thrs[e]) | ((knew[e] == thrs[e]) & (ties_taken < needs[e]))
        slot0 = (lax.broadcasted_iota(I32, (8, LANES), 0) == 0) & (lax.broadcasted_iota(I32, (8, LANES), 1) == 0)
        bnew_ref[e] = jnp.where(slot0 & jnp.broadcast_to(new_sel, (8, LANES)), 0.0, NEG)


def _idx_decode2(page_table, iqd, iwd, iknew, cache, nsel, pp):
    bsz, _, idim = iqd.shape
    n_pages = page_table.shape[1]
    psz = cache.shape[2]
    ng, w = n_pages // pp, pp * psz
    pos_bits = int(n_pages * psz).bit_length()
    assert bsz * ng >= PAGE_RING_SLOTS
    grid_spec = pltpu.PrefetchScalarGridSpec(
        num_scalar_prefetch=1,
        grid=(bsz, ng),
        in_specs=[pl.BlockSpec((None, 8, idim), lambda b, g, ptr: (b, 0, 0)),
                  pl.BlockSpec((None, 8, 1), lambda b, g, ptr: (b, 0, 0)),
                  pl.BlockSpec(memory_space=pl.ANY)],
        out_specs=pl.BlockSpec((None, None, 1, w), lambda b, g, ptr: (b, g, 0, 0)),
        scratch_shapes=[pltpu.VMEM((PAGE_RING_SLOTS, pp, idim, psz), F32),
                        pltpu.SemaphoreType.DMA((PAGE_RING_SLOTS,))],
    )
    keys = pl.pallas_call(
        functools.partial(_idx_score_kernel, pp=pp),
        grid_spec=grid_spec,
        out_shape=jax.ShapeDtypeStruct((bsz, ng, 1, w), I32),
        compiler_params=_cp(("arbitrary", "arbitrary")),
        name="idx_score",
    )(page_table.reshape(-1), iqd, iwd, cache)

    eb = _pick(bsz, (8, 4, 2, 1))
    blk = lambda *shape: pl.BlockSpec((eb,) + shape, lambda i: (i,) + tuple(0 for _ in shape))
    return pl.pallas_call(
        functools.partial(_idx_select_kernel, eb=eb, nsel=nsel, pos_bits=pos_bits),
        grid=(bsz // eb,),
        in_specs=[blk(ng, w), blk(8, idim), blk(8, 1), blk(8, idim)],
        out_specs=(blk(ng, w), blk(8, LANES)),
        out_shape=(jax.ShapeDtypeStruct((bsz, ng, w), F32), jax.ShapeDtypeStruct((bsz, 8, LANES), F32)),
        compiler_params=_cp(("parallel",)),
        name="idx_select",
    )(keys.reshape(bsz, ng, w), iqd, iwd, iknew)


def _dsa_dec2_kernel(pt_ref, q_ref, new_ref, bias_ref, bnew_ref, cache_ref, o_ref, pg_ref, s_ref, buf_ref,
                     sem_ref, *, pp, hd, gsz):
    g = pl.program_id(1)
    q = q_ref[...]
    kw = 2 * hd
    slab = _stage_pages(_fetch_pages(pt_ref, cache_ref, buf_ref, sem_ref, pp))
    pg_ref[g] = slab
    s_ref[g] = _dot(q, slab[0:kw, :]) + bias_ref[g]

    @pl.when(g == pl.num_programs(1) - 1)
    def _():
        new = new_ref[...]
        s_new = lax.dot_general(q, new[:, 0:kw], NT, preferred_element_type=F32)
        s_new = s_new + bnew_ref[0:1, 0:8]
        o = _paged_softmax_pv(s_ref, pg_ref, s_new, new[:, kw:2 * kw], slice(kw, 2 * kw))
        rowi = lax.broadcasted_iota(I32, o.shape, 0)
        o_ref[...] = jnp.where(rowi < gsz, o, pltpu.roll(o, hd, 1))


def _dsa_decode2(page_table, qbd, newkv, bias, bias_new, cache, pp, hd, gsz):
    bsz = qbd.shape[0]
    n_pages = page_table.shape[1]
    psz = cache.shape[2]
    ng, w = n_pages // pp, pp * psz
    assert bsz * ng >= PAGE_RING_SLOTS
    kern = functools.partial(_dsa_dec2_kernel, pp=pp, hd=hd, gsz=gsz)
    grid_spec = pltpu.PrefetchScalarGridSpec(
        num_scalar_prefetch=1,
        grid=(bsz, ng),
        in_specs=[pl.BlockSpec((None, 8, 2 * hd), lambda b, g, ptr: (b, 0, 0)),
                  pl.BlockSpec((None, 8, 4 * hd), lambda b, g, ptr: (b, 0, 0)),
                  pl.BlockSpec((None, ng, 1, w), lambda b, g, ptr: (b, 0, 0, 0)),
                  pl.BlockSpec((None, 8, LANES), lambda b, g, ptr: (b, 0, 0)),
                  pl.BlockSpec(memory_space=pl.ANY)],
        out_specs=pl.BlockSpec((None, 8, 2 * hd), lambda b, g, ptr: (b, 0, 0)),
        scratch_shapes=[pltpu.VMEM((ng, 4 * hd, w), BF16), pltpu.VMEM((ng, 8, w), F32),
                        pltpu.VMEM((PAGE_RING_SLOTS, pp, 4 * hd, psz), F32),
                        pltpu.SemaphoreType.DMA((PAGE_RING_SLOTS,))],
    )
    return pl.pallas_call(
        kern,
        grid_spec=grid_spec,
        out_shape=jax.ShapeDtypeStruct((bsz, 8, 2 * hd), F32),
        compiler_params=_cp(("arbitrary", "arbitrary")),
        name="dsa_decode",
    )(page_table.reshape(-1), qbd, newkv, bias.reshape(bsz, ng, 1, w), bias_new, cache)


def _rope_tables(pos, half):
    inv_freq = ROPE_THETA ** (-jnp.arange(half, dtype=F32) / half)
    ang = pos.astype(F32)[:, None] * inv_freq[None, :]
    cos, sin = jnp.cos(ang), jnp.sin(ang)
    reps = LANES // (2 * half)
    return (jnp.tile(jnp.concatenate([cos, cos], axis=1), (1, reps)),
            jnp.tile(jnp.concatenate([-sin, sin], axis=1), (1, reps)))


def _pick(n, pref):
    for c in pref:
        if n % c == 0:
            return c
    return n


def kernel(x_prompt, x_sample, cache_mla, cache_dsa_kv, cache_idx, state_hgrn, page_table, c_prompt, c_sample,
           w_ada, b_ada, norm_mix_g, norm_ffn_g, w_in_even, mla_q_norm_g, mla_kv_norm_g, mla_w_q_nope,
           mla_w_q_rope, mla_w_uk, mla_w_uv, w_out_even, w_in_odd, hgrn_lower_bounds, hgrn_out_norm_g,
           w_out_odd, w_ffn_gate, w_ffn_up, w_ffn_down, final_norm_g):
    depth, d, _ = w_ada.shape
    bp, tp, _ = x_prompt.shape
    bs, ts, _ = x_sample.shape
    assert ts == 1
    q_lora, mla_heads, mla_nope = mla_w_q_nope.shape[1:]
    mla_rope = mla_w_q_rope.shape[3]
    kv_lora, _, mla_v = mla_w_uv.shape[1:]
    dsa_kv_heads, dsa_hd = cache_dsa_kv.shape[4:]
    idx_dim = cache_idx.shape[3]
    n_pages = page_table.shape[1]
    psz = cache_mla.shape[2]
    past_len = n_pages * psz
    sizes_tail = w_in_even.shape[2] - (q_lora + kv_lora + mla_rope + 2 * dsa_kv_heads * dsa_hd + idx_dim)
    dsa_heads = (w_out_even.shape[1] - mla_heads * mla_v) // dsa_hd
    idx_heads = (sizes_tail - dsa_heads * dsa_hd) // (idx_dim + 1)
    dm = EvenDims(q_lora, kv_lora, mla_rope, mla_heads, dsa_heads, dsa_kv_heads, dsa_hd, idx_heads, idx_dim)
    gsz = dsa_heads // dsa_kv_heads
    hg_f = hgrn_lower_bounds.shape[1]
    hg_heads = state_hgrn.shape[2]
    hg_k = state_hgrn.shape[3]
    hg_v = state_hgrn.shape[4]
    log2e = float(np.log2(np.e))
    scales = ((mla_nope + mla_rope) ** -0.5 * log2e, dsa_hd ** -0.5 * log2e, idx_dim ** -0.5, idx_heads ** -0.5)
    hg_q_scale = hg_k ** -0.5

    lb_all = jax.nn.softmax(hgrn_lower_bounds.astype(F32), axis=0)
    lb_all = jnp.cumsum(lb_all, axis=0) - lb_all[0]

    c_all = jnp.concatenate([c_prompt, c_sample], axis=0)
    mods = _adaln(c_all, w_ada, b_ada)

    def mods_for(l, sample):
        mm = mods[l, bp:] if sample else mods[l, :bp]
        parts = [mm[:, k * d:(k + 1) * d] for k in range(6)]
        return [p[None] if sample else p[:, None, :] for p in parts]

    bf = lambda a: a.astype(BF16)
    row1 = lambda a: a.reshape(1, -1).astype(F32)

    def even_weights(la):
        w = w_in_even[la]
        splits = np.cumsum([q_lora, kv_lora, mla_rope, dsa_heads * dsa_hd, dsa_kv_heads * dsa_hd,
                            dsa_kv_heads * dsa_hd, idx_heads * idx_dim, idx_dim])
        wq, wckv, wkr, wdq, wdk, wdv, wiq, wik, wiw = jnp.split(w, splits, axis=1)
        padc = lambda a, n: jnp.pad(a, ((0, 0), (0, n - a.shape[1])))
        w_cat = jnp.concatenate([wq, wckv, padc(wkr, LANES), wdq, wdk, wdv, wiq,
                                 padc(jnp.concatenate([wik, wiw], axis=1), LANES)], axis=1)
        assert w_cat.shape[1] == dm.n
        wabs = _absorb(mla_w_q_nope[la], mla_w_uk[la])
        wqr = mla_w_q_rope[la].reshape(q_lora, mla_heads * mla_rope)
        wuv = jnp.transpose(mla_w_uv[la], (1, 0, 2))
        return bf(w_cat), bf(wabs), bf(wqr), bf(wuv)

    def run(x, sample):
        b, t, _ = x.shape
        m = b * t
        x2d = x.reshape(m, d)
        if sample:
            bq, tq, per_row = 1, m, True
            tm = _pick(m, (128,))
            pos = jnp.full((m,), past_len, I32)
        else:
            bq, tq, per_row = b, t, False
            tm = _pick(t, (512, 256, 128))
            pos = jnp.arange(t)
        tabs = (*_rope_tables(pos, dsa_hd // 2), *_rope_tables(pos, mla_rope // 2))
        rows_mla, rows_dsa, rows_idx, states = [], [], [], []
        for l in range(depth):
            sh1, sc1, g1, sh2, sc2, g2 = mods_for(l, sample)
            nmg, nfg = row1(norm_mix_g[l]), row1(norm_ffn_g[l])
            final = l == depth - 1
            if l % 2 == 0:
                la = l // 2
                w_cat, wabs, wqr, wuv = even_weights(la)
                (qcat, mrow, kv16, dsarow, dq, dk, dvt, iq, ik, ik16, iwt) = _even_project(
                    x2d, sc1, sh1, nmg, w_cat, row1(mla_q_norm_g[la]), row1(mla_kv_norm_g[la]), wqr, wabs,
                    tabs, dm, bq, tq, tm, per_row, scales)
                if sample:
                    nsel = min(TOPK_MAX, (past_len + t) // 4)
                    pp = _pick(n_pages, (16, 8))
                    pad8 = lambda a: jnp.pad(a[:, None, :], ((0, 0), (0, 7), (0, 0)))
                    qdec = jnp.transpose(qcat[0], (1, 0, 2))
                    page_t = lambda c: jnp.transpose(c, (0, 2, 1))
                    mla_o = _mla_decode2(page_table, qdec, pad8(kv16), wuv, page_t(cache_mla[la]), pp)
                    iqd = jnp.pad(jnp.transpose(iq[0], (1, 0, 2)), ((0, 0), (0, 8 - idx_heads), (0, 0)))
                    iwd = jnp.transpose(iwt[0])[:, :, None]
                    iwd = jnp.where(jnp.arange(8)[None, :, None] < idx_heads, iwd, 0.0)
                    bias, bias_new = _idx_decode2(page_table, iqd, iwd, pad8(ik16), page_t(cache_idx[la]), nsel, pp)
                    dqd = jnp.transpose(dq[0], (1, 0, 2))
                    zeros = jnp.zeros_like(dqd[:, :gsz])
                    qbd = jnp.concatenate(
                        [jnp.concatenate([dqd[:, :gsz], zeros], axis=2),
                         jnp.concatenate([zeros, dqd[:, gsz:]], axis=2)], axis=1)
                    cache_kv = jnp.transpose(cache_dsa_kv[la], (0, 2, 3, 4, 1)).reshape(-1, 4 * dsa_hd, psz)
                    dsa_o8 = _dsa_decode2(page_table, qbd, pad8(bf(dsarow)), bias, bias_new, cache_kv, pp, dsa_hd,
                                          gsz)
                    dsa_o = dsa_o8[:, :, :dsa_hd].reshape(b, dsa_heads * dsa_hd)
                    mla_o, dsa_o = bf(mla_o), bf(dsa_o)
                else:
                    nsel = min(TOPK_MAX, t // 4)
                    mla_o = _mla_prompt(qcat, kv16.reshape(b, t, -1), wuv, _pick(t, (256, 128))).reshape(m, -1)
                    dsa_o = _dsa_prompt(iq, iwt, ik16, dq, dk, dvt, nsel, _pick(t, (256, 128))).reshape(m, -1)
                n_mla = mla_heads * mla_v
                mixes = [(mla_o, bf(w_out_even[la][:n_mla])), (dsa_o, bf(w_out_even[la][n_mla:]))]
                rows_mla.append(mrow.reshape(b, t, -1))
                if sample:
                    rows_dsa.append(dsarow.reshape(b, t, 2, dsa_kv_heads, dsa_hd))
                else:
                    rows_dsa.append(jnp.transpose(dsarow.reshape(b, 2, dsa_kv_heads, dsa_hd, t), (0, 4, 1, 2, 3)))
                rows_idx.append(ik.reshape(b, t, -1))
            else:
                lr = l // 2
                z = _normmod_mm(x2d, sc1, sh1, nmg, bf(w_in_odd[lr]), tq, tm, per_row)
                lb = lb_all[l].reshape(hg_heads, 1, hg_k)
                og = row1(hgrn_out_norm_g[lr])
                if sample:
                    o_g, st = _hgrn_step(z, lb, og, state_hgrn[lr].astype(F32), hg_heads, hg_q_scale)
                else:
                    o_g, st = _hgrn_prompt(z, lb, og, b, t, hg_heads, _pick(t, (256, 128)), hg_q_scale)
                mixes = [(o_g, bf(w_out_odd[lr]))]
                states.append(st)
            x2d = _mix_ffn(x2d, mixes, g1, nfg, sc2, sh2, g2, bf(w_ffn_gate[l]), bf(w_ffn_up[l]),
                           bf(w_ffn_down[l]), row1(final_norm_g), tq, tm, per_row, final)
        return (x2d.reshape(b, t, d), jnp.stack(rows_mla), jnp.stack(rows_dsa), jnp.stack(rows_idx),
                jnp.stack(states))

    y_p, mla_p, dsa_p, idx_p, hg_p = run(x_prompt, False)
    y_s, mla_s, dsa_s, idx_s, hg_s = run(x_sample, True)
    return (y_p, y_s, mla_p, mla_s, dsa_p, dsa_s, idx_p, idx_s, hg_p, hg_s)
```
